```python
import jax, jax.numpy as jnp
from jax import lax
import numpy as np

D_MODEL = 1024
BATCH = 8
SEQ = 4096
DEPTH = 2

CHUNK = 64
HEAD_DIM = 64
DSA_HEADS = 8
IDX_HEADS = 8
IDX_DIM = 64
DSA_TOPK_MAX = 256
DSA_BLOCK = 64
FOX_HEADS = 8
FOX_BLOCK = 128
ROPE_THETA = 10000.0
D_FF = 3584
N_EXPERTS = 8
TOP_K = 2
MOE_BLOCK = 256
LN_EPS = 1e-5
DSA_WIDTH = DSA_HEADS * HEAD_DIM
FOX_WIDTH = FOX_HEADS * HEAD_DIM
PROJ_SIZES = (DSA_WIDTH, DSA_WIDTH, DSA_WIDTH, IDX_HEADS * IDX_DIM, IDX_DIM, IDX_HEADS,
              FOX_WIDTH, FOX_WIDTH, FOX_WIDTH, FOX_HEADS, D_MODEL, D_MODEL)
PROJ_TOTAL = sum(PROJ_SIZES)

kernel_name = "hybrid_dsa_fox_gated_moe_deepnorm"


def layer_norm(x, g, b):
    xf = x.astype(jnp.float32)
    mu = jnp.mean(xf, axis=-1, keepdims=True)
    xc = xf - mu
    var = jnp.mean(xc * xc, axis=-1, keepdims=True)
    y = xc * lax.rsqrt(var + LN_EPS)
    return (y * g.astype(jnp.float32) + b.astype(jnp.float32)).astype(x.dtype)


def rotary_tables(positions, dim):
    inv_freq = ROPE_THETA ** (-jnp.arange(0, dim, 2, dtype=jnp.float32) / dim)
    ang = positions.astype(jnp.float32)[..., None] * inv_freq
    return jnp.cos(ang), jnp.sin(ang)


def apply_rotary(x, cos, sin):
    c = cos[:, :, None, :].astype(x.dtype)
    s = sin[:, :, None, :].astype(x.dtype)
    x1, x2 = jnp.split(x, 2, axis=-1)
    return jnp.concatenate([x1 * c - x2 * s, x2 * c + x1 * s], axis=-1)


def split_projection(proj):
    points = [int(p) for p in np.cumsum(PROJ_SIZES)[:-1]]
    return jnp.split(proj, points, axis=-1)


def to_blocks(a, block):
    B, T = a.shape[0], a.shape[1]
    return a.reshape((B, T // block, block) + a.shape[2:]).swapaxes(0, 1)


def from_blocks(a):
    a = a.swapaxes(0, 1)
    return a.reshape((a.shape[0], a.shape[1] * a.shape[2]) + a.shape[3:])


def dsa_attention(q, k, v, iq, ik, iw, top_k):
    B, T, H, dh = q.shape
    nb = T // DSA_BLOCK
    key_pos = jnp.arange(T)
    ik_f = ik.astype(jnp.float32)

    def body(args):
        qb, iqb, iwb, blk = args
        t = blk * DSA_BLOCK + jnp.arange(DSA_BLOCK)
        limit = (t // CHUNK + 1) * CHUNK
        admissible = key_pos[None, :] < limit[:, None]
        head_sc = jnp.einsum('bqhd,bsd->bqhs', iqb.astype(jnp.float32), ik_f) * (IDX_DIM ** -0.5)
        score = jnp.einsum('bqhs,bqh->bqs', jax.nn.relu(head_sc), iwb.astype(jnp.float32))
        score = jnp.where(admissible[None], score, -jnp.inf)
        _, sel = lax.top_k(score, top_k)
        valid = sel < limit[None, :, None]
        k_sel = jax.vmap(lambda kb, ib: kb[ib])(k, sel)
        v_sel = jax.vmap(lambda vb, ib: vb[ib])(v, sel)
        logits = jnp.einsum('bqhd,bqkhd->bhqk', qb, k_sel).astype(jnp.float32) * (dh ** -0.5)
        logits = jnp.where(valid[:, None], logits, -jnp.inf)
        p = jax.nn.softmax(logits, axis=-1).astype(v.dtype)
        return jnp.einsum('bhqk,bqkhd->bqhd', p, v_sel)

    out = lax.map(body, (to_blocks(q, DSA_BLOCK), to_blocks(iq, DSA_BLOCK),
                         to_blocks(iw, DSA_BLOCK), jnp.arange(nb)))
    return from_blocks(out).reshape(B, T, H * dh)


def forgetting_attention(q, k, v, log_f):
    B, T, H, dh = q.shape
    nb = T // FOX_BLOCK
    key_pos = jnp.arange(T)
    c = jnp.cumsum(log_f, axis=1)
    c_k = c.transpose(0, 2, 1)

    def body(args):
        qb, cqb, blk = args
        t = blk * FOX_BLOCK + jnp.arange(FOX_BLOCK)
        causal = key_pos[None, :] <= t[:, None]
        decay = cqb.transpose(0, 2, 1)[..., None] - c_k[:, :, None, :]
        logits = jnp.einsum('bqhd,bshd->bhqs', qb, k).astype(jnp.float32) * (dh ** -0.5) + decay
        logits = jnp.where(causal, logits, -jnp.inf)
        p = jax.nn.softmax(logits, axis=-1).astype(v.dtype)
        return jnp.einsum('bhqs,bshd->bqhd', p, v)

    out = lax.map(body, (to_blocks(q, FOX_BLOCK), to_blocks(c, FOX_BLOCK), jnp.arange(nb)))
    return from_blocks(out).reshape(B, T, H * dh)


def hybrid_mixer(h, cos, sin, w_in, b_forget, w_branch_a, w_branch_b, w_out, top_k):
    B, T, _ = h.shape
    proj = h @ w_in
    dq, dk, dv, iq, ik, iw, fq, fk, fv, fl, ga, gb = split_projection(proj)
    dq = apply_rotary(dq.reshape(B, T, DSA_HEADS, HEAD_DIM), cos, sin)
    dk = apply_rotary(dk.reshape(B, T, DSA_HEADS, HEAD_DIM), cos, sin)
    dv = dv.reshape(B, T, DSA_HEADS, HEAD_DIM)
    iq = apply_rotary(iq.reshape(B, T, IDX_HEADS, IDX_DIM), cos, sin)
    ik = apply_rotary(ik.reshape(B, T, 1, IDX_DIM), cos, sin)[:, :, 0, :]
    iw = iw * (IDX_HEADS ** -0.5)
    o_a = dsa_attention(dq, dk, dv, iq, ik, iw, top_k)

    fq = fq.reshape(B, T, FOX_HEADS, HEAD_DIM)
    fk = fk.reshape(B, T, FOX_HEADS, HEAD_DIM)
    fv = fv.reshape(B, T, FOX_HEADS, HEAD_DIM)
    log_f = jax.nn.log_sigmoid((fl + b_forget).astype(jnp.float32))
    o_b = forgetting_attention(fq, fk, fv, log_f)

    merged = jax.nn.sigmoid(ga) * (o_a @ w_branch_a) + jax.nn.sigmoid(gb) * (o_b @ w_branch_b)
    return merged @ w_out


def dense_swiglu(h, w_gate, w_up, w_down):
    return (jax.nn.silu(h @ w_gate) * (h @ w_up)) @ w_down


def moe_swiglu(h, router, w_gate, w_up, w_down):
    B, T, D = h.shape
    N = B * T
    hf = h.reshape(N, D)
    logits = (hf @ router).astype(jnp.float32)
    top_vals, top_idx = lax.top_k(logits, TOP_K)
    gates = jax.nn.softmax(top_vals, axis=-1).astype(h.dtype)
    A = N * TOP_K
    e_flat = top_idx.reshape(A)
    tok_flat = jnp.repeat(jnp.arange(N, dtype=jnp.int32), TOP_K)
    g_flat = gates.reshape(A)
    order = jnp.argsort(e_flat)
    e_sorted = e_flat[order]
    tok_sorted = tok_flat[order]
    g_sorted = g_flat[order]
    counts = jnp.bincount(e_flat, length=N_EXPERTS).astype(jnp.int32)
    start = jnp.cumsum(counts) - counts
    padded = ((counts + MOE_BLOCK - 1) // MOE_BLOCK) * MOE_BLOCK
    pad_end = jnp.cumsum(padded)
    pad_start = pad_end - padded
    rank = jnp.arange(A, dtype=jnp.int32) - start[e_sorted]
    slot = pad_start[e_sorted] + rank
    P = A + N_EXPERTS * MOE_BLOCK
    slot_tok = jnp.zeros((P,), jnp.int32).at[slot].set(tok_sorted)
    slot_gate = jnp.zeros((P,), h.dtype).at[slot].set(g_sorted)
    n_blocks = P // MOE_BLOCK
    block_expert = jnp.clip(
        jnp.searchsorted(pad_end, jnp.arange(n_blocks, dtype=jnp.int32) * MOE_BLOCK, side='right'),
        0, N_EXPERTS - 1)

    def run_block(args):
        toks, gts, e = args
        xb = hf[toks]
        hid = jax.nn.silu(xb @ w_gate[e]) * (xb @ w_up[e])
        return (hid @ w_down[e]) * gts[:, None]

    yb = lax.map(run_block, (slot_tok.reshape(n_blocks, MOE_BLOCK),
                             slot_gate.reshape(n_blocks, MOE_BLOCK), block_expert))
    out = jnp.zeros((N, D), h.dtype).at[slot_tok].add(yb.reshape(P, D))
    return out.reshape(B, T, D)


def setup_inputs(seed: int = 0) -> dict:
    key = jax.random.key(seed)
    ks = jax.random.split(key, 20)
    n_dense = (DEPTH + 1) // 2
    n_moe = DEPTH // 2
    beta = (8.0 * DEPTH) ** -0.25
    nrm = jax.random.normal
    x = nrm(ks[0], (BATCH, SEQ, D_MODEL), jnp.float32)
    positions = jnp.tile(jnp.arange(SEQ, dtype=jnp.int32)[None, :], (BATCH, 1))
    w_in = nrm(ks[1], (DEPTH, D_MODEL, PROJ_TOTAL), jnp.float32) * D_MODEL ** -0.5
    b_forget = 3.0 + 0.5 * nrm(ks[2], (DEPTH, FOX_HEADS), jnp.float32)
    w_branch_a = nrm(ks[3], (DEPTH, DSA_WIDTH, D_MODEL), jnp.float32) * DSA_WIDTH ** -0.5
    w_branch_b = nrm(ks[4], (DEPTH, FOX_WIDTH, D_MODEL), jnp.float32) * FOX_WIDTH ** -0.5
    w_out = nrm(ks[5], (DEPTH, D_MODEL, D_MODEL), jnp.float32) * (D_MODEL ** -0.5) * beta
    ln_mix_g = 1.0 + 0.02 * nrm(ks[6], (DEPTH, D_MODEL), jnp.float32)
    ln_mix_b = 0.02 * nrm(ks[7], (DEPTH, D_MODEL), jnp.float32)
    ln_ffn_g = 1.0 + 0.02 * nrm(ks[8], (DEPTH, D_MODEL), jnp.float32)
    ln_ffn_b = 0.02 * nrm(ks[9], (DEPTH, D_MODEL), jnp.float32)
    ffn_w_gate = nrm(ks[10], (n_dense, D_MODEL, D_FF), jnp.float32) * D_MODEL ** -0.5
    ffn_w_up = nrm(ks[11], (n_dense, D_MODEL, D_FF), jnp.float32) * D_MODEL ** -0.5
    ffn_w_down = nrm(ks[12], (n_dense, D_FF, D_MODEL), jnp.float32) * (D_FF ** -0.5) * beta
    moe_router = nrm(ks[13], (n_moe, D_MODEL, N_EXPERTS), jnp.float32) * D_MODEL ** -0.5
    moe_w_gate = nrm(ks[14], (n_moe, N_EXPERTS, D_MODEL, D_FF), jnp.float32) * D_MODEL ** -0.5
    moe_w_up = nrm(ks[15], (n_moe, N_EXPERTS, D_MODEL, D_FF), jnp.float32) * D_MODEL ** -0.5
    moe_w_down = nrm(ks[16], (n_moe, N_EXPERTS, D_FF, D_MODEL), jnp.float32) * (D_FF ** -0.5) * beta
    return {"x": x, "positions": positions, "w_in": w_in, "b_forget": b_forget,
            "w_branch_a": w_branch_a, "w_branch_b": w_branch_b, "w_out": w_out,
            "ln_mix_g": ln_mix_g, "ln_mix_b": ln_mix_b, "ln_ffn_g": ln_ffn_g, "ln_ffn_b": ln_ffn_b,
            "ffn_w_gate": ffn_w_gate, "ffn_w_up": ffn_w_up, "ffn_w_down": ffn_w_down,
            "moe_router": moe_router, "moe_w_gate": moe_w_gate, "moe_w_up": moe_w_up,
            "moe_w_down": moe_w_down}


def reference(x, positions, w_in, b_forget, w_branch_a, w_branch_b, w_out,
              ln_mix_g, ln_mix_b, ln_ffn_g, ln_ffn_b, ffn_w_gate, ffn_w_up, ffn_w_down,
              moe_router, moe_w_gate, moe_w_up, moe_w_down):
    T = x.shape[1]
    top_k = min(DSA_TOPK_MAX, T // 4)
    alpha = (2.0 * DEPTH) ** 0.25
    cos, sin = rotary_tables(positions, HEAD_DIM)
    for layer in range(DEPTH):
        mix = hybrid_mixer(x, cos, sin, w_in[layer], b_forget[layer], w_branch_a[layer],
                           w_branch_b[layer], w_out[layer], top_k)
        x = layer_norm(alpha * x + mix, ln_mix_g[layer], ln_mix_b[layer])
        j = layer // 2
        if layer % 2 == 0:
            ff = dense_swiglu(x, ffn_w_gate[j], ffn_w_up[j], ffn_w_down[j])
        else:
            ff = moe_swiglu(x, moe_router[j], moe_w_gate[j], moe_w_up[j], moe_w_down[j])
        x = layer_norm(alpha * x + ff, ln_ffn_g[layer], ln_ffn_b[layer])
    return x
```

```python
import functools

import jax
import jax.numpy as jnp
from jax import lax
from jax.experimental import pallas as pl
from jax.experimental.pallas import tpu as pltpu

F32, BF16, I32 = jnp.float32, jnp.bfloat16, jnp.int32

D_MODEL = 1024
HEAD_DIM = 64
N_HEADS = 8
HEADS_W = N_HEADS * HEAD_DIM
N_PAIRS = N_HEADS // 2
CHUNK = 64
TOPK_MAX = 256
ROPE_THETA = 10000.0
D_FF = 3584
N_EXPERTS = 8
LN_EPS = 1e-5
DEPTH = 2
LANES = 128
SUBLANES = 8
INT_MIN = -(2**31)
NEG = -1e30
VMEM_LIMIT = 56 * 1024 * 1024
NT_DIMS = (((1,), (1,)), ((), ()))


def _cparams(*sem):
    return pltpu.CompilerParams(dimension_semantics=sem, vmem_limit_bytes=VMEM_LIMIT)


def _split3(v):
    hi = v.astype(BF16)
    r1 = v - hi.astype(F32)
    mid = r1.astype(BF16)
    lo = (r1 - mid.astype(F32)).astype(BF16)
    return hi, mid, lo


def _layer_norm(y, g, b):
    mu = jnp.mean(y, axis=-1, keepdims=True)
    yc = y - mu
    var = jnp.mean(yc * yc, axis=-1, keepdims=True)
    return yc * lax.rsqrt(var + LN_EPS) * g + b


def _rope_kernel(pos_ref, freq_ref, sign_ref, cos_ref, sin_ref):
    ang = pos_ref[...].astype(F32) * freq_ref[...]
    cos_ref[...] = jnp.cos(ang)
    sin_ref[...] = jnp.sin(ang) * sign_ref[...]


def _rope_tables(positions):
    B, T = positions.shape
    n = B * T
    inv_freq = ROPE_THETA ** (-jnp.arange(0, HEAD_DIM, 2, dtype=F32) / HEAD_DIM)
    freq = jnp.tile(inv_freq, LANES // (HEAD_DIM // 2))[None, :]
    half = HEAD_DIM // 2
    sign = jnp.tile(jnp.concatenate([-jnp.ones((half,), F32), jnp.ones((half,), F32)]), LANES // HEAD_DIM)[None, :]
    pos_b = jnp.broadcast_to(positions.reshape(n, 1), (n, LANES))
    tm = min(n, 1024)
    row = pl.BlockSpec((tm, LANES), lambda i: (i, 0))
    const = pl.BlockSpec((1, LANES), lambda i: (0, 0))
    cos, sin = pl.pallas_call(
        _rope_kernel, grid=(n // tm,), in_specs=[row, const, const], out_specs=[row, row],
        out_shape=[jax.ShapeDtypeStruct((n, LANES), F32)] * 2, compiler_params=_cparams("parallel"),
        name="rope_tables")(pos_b, freq, sign)
    return cos.reshape(B, T, LANES), sin.reshape(B, T, LANES)


_Q_SCALE = HEAD_DIM ** -0.5
_IW_SCALE = (N_HEADS ** -0.5) * (HEAD_DIM ** -0.5)
_OFF_DQ, _OFF_DK, _OFF_IQ, _OFF_FQ, _OFF_FK, _OFF_GA, _OFF_GB = 0, 512, 1024, 1536, 2048, 2560, 3584
_W_MAIN = 4608
_AUX_USED = 6


def _inproj_kernel(x_ref, cos_ref, sin_ref, wm_ref, wik_ref, wvt_ref, wit_ref, wfl_ref, bfl_ref,
                   dq_ref, dk_ref, iq_ref, fq_ref, fk_ref, ga_ref, gb_ref, ik2_ref, dvt_ref, fvt_ref,
                   iwt_ref, lf_ref):
    xb = x_ref[...].astype(BF16)
    cos = cos_ref[...]
    sin = sin_ref[...]
    lane = lax.broadcasted_iota(I32, cos.shape, 1)
    first_half = (lane & (HEAD_DIM - 1)) < HEAD_DIM // 2

    def rot(y):
        partner = jnp.where(first_half, pltpu.roll(y, LANES - HEAD_DIM // 2, 1), pltpu.roll(y, HEAD_DIM // 2, 1))
        return y * cos + partner * sin

    def proj(off, width):
        return jnp.dot(xb, wm_ref[:, off:off + width], preferred_element_type=F32)

    def rot_group(off, scale, out_ref):
        y = proj(off, HEADS_W)
        for j in range(N_PAIRS):
            r = rot(y[:, LANES * j:LANES * (j + 1)])
            if scale != 1.0:
                r = r * scale
            out_ref[:, LANES * j:LANES * (j + 1)] = r.astype(BF16)

    rot_group(_OFF_DQ, _Q_SCALE, dq_ref)
    rot_group(_OFF_DK, 1.0, dk_ref)
    rot_group(_OFF_IQ, 1.0, iq_ref)
    fq_ref[...] = (proj(_OFF_FQ, HEADS_W) * _Q_SCALE).astype(BF16)
    fk_ref[...] = proj(_OFF_FK, HEADS_W).astype(BF16)
    ga_ref[...] = jax.nn.sigmoid(proj(_OFF_GA, D_MODEL)).astype(BF16)
    gb_ref[...] = jax.nn.sigmoid(proj(_OFF_GB, D_MODEL)).astype(BF16)
    ik2_ref[...] = rot(jnp.dot(xb, wik_ref[...], preferred_element_type=F32)).astype(BF16)
    vt = lax.dot_general(wvt_ref[...], xb, NT_DIMS, preferred_element_type=F32)
    dvt_ref[...] = vt[:HEADS_W].astype(BF16)
    fvt_ref[...] = vt[HEADS_W:].astype(BF16)
    it = lax.dot_general(wit_ref[...], xb, NT_DIMS, preferred_element_type=F32)
    iwt_ref[...] = it[:N_HEADS] * _IW_SCALE
    z = jnp.dot(xb, wfl_ref[...], preferred_element_type=F32) + bfl_ref[...]
    lf = jnp.minimum(z, 0.0) - jnp.log1p(jnp.exp(-jnp.abs(z)))
    lane_w = lax.broadcasted_iota(I32, lf.shape, 1) & (LANES - 1)
    lf_ref[...] = jnp.where(lane_w < _AUX_USED, lf, 0.0)


def _prep_mix_weights(w_in, b_forget):
    sizes = (HEADS_W, HEADS_W, HEADS_W, HEADS_W, HEAD_DIM, N_HEADS, HEADS_W, HEADS_W, HEADS_W, N_HEADS, D_MODEL, D_MODEL)
    offs = [0]
    for s in sizes:
        offs.append(offs[-1] + s)
    dq, dk, dv, iq, ik, iw, fq, fk, fv, fl, ga, gb = [w_in[:, offs[i]:offs[i + 1]] for i in range(12)]
    w_main = jnp.concatenate([dq, dk, iq, fq, fk, ga, gb], axis=1).astype(BF16)
    w_ik2 = jnp.concatenate([ik, ik], axis=1).astype(BF16)
    w_vt = jnp.concatenate([dv, fv], axis=1).T.astype(BF16)
    w_it = jnp.concatenate([iw.T, jnp.zeros((16 - N_HEADS, D_MODEL), F32)], axis=0).astype(BF16)
    src = []
    for j in range(N_PAIRS):
        src += [2 * j] * 3 + [2 * j + 1] * 3 + [-1] * (LANES - _AUX_USED)
    src = jnp.array(src, I32)
    used = src >= 0
    w_fl = jnp.where(used[None, :], fl[:, jnp.maximum(src, 0)], 0.0).astype(BF16)
    b_fl = jnp.where(used, b_forget[jnp.maximum(src, 0)], 0.0)[None, :].astype(F32)
    return w_main, w_ik2, w_vt, w_it, w_fl, b_fl


def _in_projection(x, cos_t, sin_t, weights, tm):
    B, T, _ = x.shape
    w_main, w_ik2, w_vt, w_it, w_fl, b_fl = weights
    nt = T // tm

    def rows(width):
        return pl.BlockSpec((None, tm, width), lambda b, i: (b, i, 0))

    def cols(height):
        return pl.BlockSpec((None, height, tm), lambda b, i: (b, 0, i))

    def const(shape):
        return pl.BlockSpec(shape, lambda b, i: (0, 0), pipeline_mode=pl.Buffered(1))

    bt = lambda w, dt: jax.ShapeDtypeStruct((B, T, w), dt)
    tb = lambda h, dt: jax.ShapeDtypeStruct((B, h, T), dt)
    return pl.pallas_call(
        _inproj_kernel, grid=(B, nt),
        in_specs=[rows(D_MODEL), rows(LANES), rows(LANES), const(w_main.shape), const(w_ik2.shape),
                  const(w_vt.shape), const(w_it.shape), const(w_fl.shape), const(b_fl.shape)],
        out_specs=[rows(HEADS_W)] * 5 + [rows(D_MODEL)] * 2 + [rows(LANES), cols(HEADS_W), cols(HEADS_W),
                                                                cols(N_HEADS), rows(HEADS_W)],
        out_shape=[bt(HEADS_W, BF16)] * 5 + [bt(D_MODEL, BF16)] * 2 + [bt(LANES, BF16), tb(HEADS_W, BF16),
                                                                      tb(HEADS_W, BF16), tb(N_HEADS, F32),
                                                                      bt(HEADS_W, F32)],
        compiler_params=_cparams("parallel", "parallel"), name="in_projection",
    )(x, cos_t, sin_t, w_main, w_ik2, w_vt, w_it, w_fl, b_fl)


def _cumsum_kernel(lf_ref, aux_ref, carry_ref):
    @pl.when(pl.program_id(1) == 0)
    def _():
        carry_ref[...] = jnp.zeros_like(carry_ref)

    lf = lf_ref[...]
    tb = lf.shape[0]
    r = lax.broadcasted_iota(I32, (tb, tb), 0)
    c = lax.broadcasted_iota(I32, (tb, tb), 1)
    tri = jnp.where(c <= r, 1.0, 0.0).astype(BF16)
    hi, mid, lo = _split3(lf)
    cs = (jnp.dot(tri, lo, preferred_element_type=F32) + jnp.dot(tri, mid, preferred_element_type=F32)
          + jnp.dot(tri, hi, preferred_element_type=F32) + carry_ref[...])
    carry_ref[...] = cs[tb - 1:tb, :]
    nh, nm, nl = _split3(-cs)
    lane_w = lax.broadcasted_iota(I32, cs.shape, 1) & (LANES - 1)
    k = jnp.where(lane_w >= 3, lane_w - 3, lane_w)
    piece = jnp.where(k == 0, nh.astype(F32), jnp.where(k == 1, nm.astype(F32), nl.astype(F32)))
    aux_ref[...] = jnp.where(lane_w < _AUX_USED, piece, 0.0).astype(BF16)


def _forget_aux(lf, tb):
    B, T, W = lf.shape
    spec = pl.BlockSpec((None, tb, W), lambda b, i: (b, i, 0))
    return pl.pallas_call(
        _cumsum_kernel, grid=(B, T // tb), in_specs=[spec], out_specs=spec,
        out_shape=jax.ShapeDtypeStruct((B, T, W), BF16), scratch_shapes=[pltpu.VMEM((1, W), F32)],
        compiler_params=_cparams("parallel", "arbitrary"), name="forget_prefix")(lf)


def _stack_pair(qb):
    qf = qb.astype(F32)
    lane = lax.broadcasted_iota(I32, qf.shape, 1)
    lo = lane < HEAD_DIM
    return jnp.concatenate([jnp.where(lo, qf, 0.0), jnp.where(lo, 0.0, qf)], axis=0).astype(BF16)


def _softmax_step(s, vb, m_sc, l_sc, acc_sc):
    m_old = m_sc[...]
    m_new = jnp.maximum(m_old, jnp.max(s, axis=0, keepdims=True))
    alpha = jnp.exp(m_old - m_new)
    p = jnp.exp(s - m_new)
    l_sc[...] = alpha * l_sc[...] + jnp.sum(p, axis=0, keepdims=True)
    acc_sc[...] = alpha * acc_sc[...] + jnp.dot(vb, p.astype(BF16), preferred_element_type=F32)
    m_sc[...] = m_new


def _init_softmax(m_sc, l_sc, acc_sc):
    m_sc[...] = jnp.full(m_sc.shape, -jnp.inf, F32)
    l_sc[...] = jnp.zeros(l_sc.shape, F32)
    acc_sc[...] = jnp.zeros(acc_sc.shape, F32)


def _finish_pair(o_ref, j, tq, l_sc, acc_sc):
    out_t = acc_sc[...] / l_sc[...]
    blk_t = jnp.concatenate([out_t[:HEAD_DIM, :tq], out_t[HEAD_DIM:, tq:]], axis=0)
    o_ref[:, LANES * j:LANES * (j + 1)] = blk_t.T.astype(BF16)


def _fox_kernel(q_ref, k_ref, aux_ref, vt_ref, o_ref, m_sc, l_sc, acc_sc, *, tq):
    qi = pl.program_id(1)
    q0 = qi * tq
    row2 = lax.broadcasted_iota(I32, (2 * tq, LANES), 0)
    lane2 = lax.broadcasted_iota(I32, (2 * tq, LANES), 1)
    sel = jnp.where(row2 < tq, jnp.where(lane2 < 3, 1.0, 0.0),
                    jnp.where(lane2 < 3, 0.0, jnp.where(lane2 < _AUX_USED, 1.0, 0.0))).astype(BF16)
    for j in range(N_PAIRS):
        cs = slice(LANES * j, LANES * (j + 1))
        qa = jnp.concatenate([_stack_pair(q_ref[:, cs]), sel], axis=1)
        _init_softmax(m_sc, l_sc, acc_sc)

        def step(kv, masked, cs=cs, qa=qa):
            ks = pl.multiple_of(kv * tq, tq)
            kb = jnp.concatenate([k_ref[pl.ds(ks, tq), cs], aux_ref[pl.ds(ks, tq), cs]], axis=1)
            s = lax.dot_general(kb, qa, NT_DIMS, preferred_element_type=F32)
            if masked:
                krow = ks + lax.broadcasted_iota(I32, s.shape, 0)
                qcol = q0 + (lax.broadcasted_iota(I32, s.shape, 1) & (tq - 1))
                s = jnp.where(krow <= qcol, s, NEG)
            _softmax_step(s, vt_ref[cs, pl.ds(ks, tq)], m_sc, l_sc, acc_sc)

        def body(kv, carry, step=step):
            step(kv, False)
            return carry

        lax.fori_loop(0, qi, body, 0)
        step(qi, True)
        _finish_pair(o_ref, j, tq, l_sc, acc_sc)


def _fox_attention(fq, fk, aux, fvt, tq):
    B, T, W = fq.shape
    qspec = pl.BlockSpec((None, tq, W), lambda b, i: (b, i, 0))
    full = pl.BlockSpec((None, T, W), lambda b, i: (b, 0, 0))
    fullt = pl.BlockSpec((None, W, T), lambda b, i: (b, 0, 0))
    return pl.pallas_call(
        functools.partial(_fox_kernel, tq=tq), grid=(B, T // tq),
        in_specs=[qspec, full, full, fullt], out_specs=qspec,
        out_shape=jax.ShapeDtypeStruct((B, T, W), BF16),
        scratch_shapes=[pltpu.VMEM((1, 2 * tq), F32), pltpu.VMEM((1, 2 * tq), F32), pltpu.VMEM((LANES, 2 * tq), F32)],
        compiler_params=_cparams("parallel", "arbitrary"), name="fox_attention")(fq, fk, aux, fvt)


_COUNT_ROWS = 64


def _dsa_kernel(iq_ref, ik2_ref, iwt_ref, q_ref, k_ref, vt_ref, o_ref, key_sc, bias_sc, m_sc, l_sc, acc_sc,
                *, tq, top_k):
    qi = pl.program_id(1)
    q0 = qi * tq
    col = lax.broadcasted_iota(I32, (1, tq), 1)
    limit = (lax.shift_right_logical(q0 + col, CHUNK.bit_length() - 1) + 1) * CHUNK
    iw = iwt_ref[...]

    iqs = [_stack_pair(iq_ref[:, LANES * j:LANES * (j + 1)]) for j in range(N_PAIRS)]

    def score_tile(kv, masked):
        ks = pl.multiple_of(kv * tq, tq)
        kb = ik2_ref[pl.ds(ks, tq), :]
        sc = jnp.zeros((tq, tq), F32)
        for j in range(N_PAIRS):
            s2 = lax.dot_general(kb, iqs[j], NT_DIMS, preferred_element_type=F32)
            sc = sc + jnp.maximum(s2[:, :tq], 0.0) * iw[2 * j:2 * j + 1, :]
            sc = sc + jnp.maximum(s2[:, tq:], 0.0) * iw[2 * j + 1:2 * j + 2, :]
        bits = pltpu.bitcast(sc, I32)
        key = bits ^ ((bits >> 31) & jnp.int32(0x7FFFFFFF))
        key = jnp.where(bits == jnp.int32(INT_MIN), 0, key)
        if masked:
            krow = ks + lax.broadcasted_iota(I32, key.shape, 0)
            key = jnp.where(krow < limit, key, jnp.int32(INT_MIN))
        key_sc[pl.ds(ks, tq), :] = key

    def score_body(kv, carry):
        score_tile(kv, False)
        return carry

    lax.fori_loop(0, qi, score_body, 0)
    score_tile(qi, True)

    n_groups = (q0 + tq) // _COUNT_ROWS

    def count(pred):
        def body(g, acc):
            base = pl.multiple_of(g * _COUNT_ROWS, _COUNT_ROWS)
            for r in range(_COUNT_ROWS // SUBLANES):
                blk = key_sc[pl.ds(base + SUBLANES * r, SUBLANES), :]
                acc = acc + jnp.where(pred(blk), 1, 0)
            return acc

        acc = lax.fori_loop(0, n_groups, body, jnp.zeros((SUBLANES, tq), I32))
        return jnp.sum(acc, axis=0, keepdims=True)

    def search_body(i, v):
        cand = v + lax.shift_left(jnp.int32(1), 31 - i)
        cand8 = jnp.broadcast_to(cand, (SUBLANES, tq))
        cnt = count(lambda blk: blk >= cand8)
        return jnp.where(cnt >= top_k, cand, v)

    thr = lax.fori_loop(0, 32, search_body, jnp.full((1, tq), INT_MIN, I32))
    thr8 = jnp.broadcast_to(thr, (SUBLANES, tq))
    need = (top_k - count(lambda blk: blk > thr8)).astype(F32)

    r = lax.broadcasted_iota(I32, (tq, tq), 0)
    c = lax.broadcasted_iota(I32, (tq, tq), 1)
    tri = jnp.where(c < r, 1.0, 0.0).astype(BF16)

    def bias_tile(kv, seen, masked):
        ks = pl.multiple_of(kv * tq, tq)
        key = key_sc[pl.ds(ks, tq), :]
        eqf = jnp.where(key == thr, 1.0, 0.0)
        before = jnp.dot(tri, eqf.astype(BF16), preferred_element_type=F32) + seen
        b = jnp.where(key > thr, 0.0, jnp.where(key == thr, jnp.where(before < need, 0.0, NEG), NEG))
        if masked:
            krow = ks + lax.broadcasted_iota(I32, key.shape, 0)
            b = jnp.where(krow < limit, b, NEG)
        bias_sc[pl.ds(ks, tq), :] = b
        return seen + jnp.sum(eqf, axis=0, keepdims=True)

    seen = lax.fori_loop(0, qi, lambda kv, s: bias_tile(kv, s, False), jnp.zeros((1, tq), F32))
    bias_tile(qi, seen, True)

    for j in range(N_PAIRS):
        cs = slice(LANES * j, LANES * (j + 1))
        qs = _stack_pair(q_ref[:, cs])
        _init_softmax(m_sc, l_sc, acc_sc)

        def body(kv, carry, cs=cs, qs=qs):
            ks = pl.multiple_of(kv * tq, tq)
            s = lax.dot_general(k_ref[pl.ds(ks, tq), cs], qs, NT_DIMS, preferred_element_type=F32)
            b = bias_sc[pl.ds(ks, tq), :]
            s = s + jnp.concatenate([b, b], axis=1)
            _softmax_step(s, vt_ref[cs, pl.ds(ks, tq)], m_sc, l_sc, acc_sc)
            return carry

        lax.fori_loop(0, qi + 1, body, 0)
        _finish_pair(o_ref, j, tq, l_sc, acc_sc)


def _dsa_attention(iq, ik2, iwt, dq, dk, dvt, tq, top_k):
    B, T, W = dq.shape
    qspec = pl.BlockSpec((None, tq, W), lambda b, i: (b, i, 0))
    full = pl.BlockSpec((None, T, W), lambda b, i: (b, 0, 0))
    fullt = pl.BlockSpec((None, W, T), lambda b, i: (b, 0, 0))
    return pl.pallas_call(
        functools.partial(_dsa_kernel, tq=tq, top_k=top_k), grid=(B, T // tq),
        in_specs=[qspec, pl.BlockSpec((None, T, LANES), lambda b, i: (b, 0, 0)),
                  pl.BlockSpec((None, N_HEADS, tq), lambda b, i: (b, 0, i)), qspec, full, fullt],
        out_specs=qspec, out_shape=jax.ShapeDtypeStruct((B, T, W), BF16),
        scratch_shapes=[pltpu.VMEM((T, tq), I32), pltpu.VMEM((T, tq), F32), pltpu.VMEM((1, 2 * tq), F32),
                        pltpu.VMEM((1, 2 * tq), F32), pltpu.VMEM((LANES, 2 * tq), F32)],
        compiler_params=_cparams("parallel", "arbitrary"), name="dsa_attention")(iq, ik2, iwt, dq, dk, dvt)


def _merge_kernel(oa_ref, ob_ref, ga_ref, gb_ref, x_ref, wa_ref, wb_ref, wo_ref, g_ref, b_ref, o_ref, ob16_ref,
                  *, alpha):
    a = jnp.dot(oa_ref[...], wa_ref[...], preferred_element_type=F32)
    b = jnp.dot(ob_ref[...], wb_ref[...], preferred_element_type=F32)
    merged = ga_ref[...].astype(F32) * a + gb_ref[...].astype(F32) * b
    mix = jnp.dot(merged.astype(BF16), wo_ref[...], preferred_element_type=F32)
    y = _layer_norm(alpha * x_ref[...] + mix, g_ref[...], b_ref[...])
    o_ref[...] = y
    ob16_ref[...] = y.astype(BF16)


def _merge_project(oa, ob, ga, gb, x, wa, wb, wo, g, b, alpha, tm):
    n = x.shape[0]

    def rows(w):
        return pl.BlockSpec((tm, w), lambda i: (i, 0))

    def const(shape):
        return pl.BlockSpec(shape, lambda i: (0, 0), pipeline_mode=pl.Buffered(1))

    return pl.pallas_call(
        functools.partial(_merge_kernel, alpha=alpha), grid=(n // tm,),
        in_specs=[rows(HEADS_W), rows(HEADS_W), rows(D_MODEL), rows(D_MODEL), rows(D_MODEL), const(wa.shape),
                  const(wb.shape), const(wo.shape), const(g.shape), const(b.shape)],
        out_specs=[rows(D_MODEL), rows(D_MODEL)],
        out_shape=[jax.ShapeDtypeStruct((n, D_MODEL), F32), jax.ShapeDtypeStruct((n, D_MODEL), BF16)],
        compiler_params=_cparams("parallel"), name="merge_project")(oa, ob, ga, gb, x, wa, wb, wo, g, b)


def _swiglu_partial(xb, wg, wu, wd):
    h = jnp.dot(xb, wg, preferred_element_type=F32)
    u = jnp.dot(xb, wu, preferred_element_type=F32)
    hid = (h * jax.nn.sigmoid(h)) * u
    return jnp.dot(hid.astype(BF16), wd, preferred_element_type=F32)


def _ffn_kernel(xb_ref, x_ref, wg_ref, wu_ref, wd_ref, g_ref, b_ref, o_ref, acc_sc, *, alpha):
    f = pl.program_id(1)

    @pl.when(f == 0)
    def _():
        acc_sc[...] = jnp.zeros_like(acc_sc)

    acc_sc[...] += _swiglu_partial(xb_ref[...], wg_ref[...], wu_ref[...], wd_ref[...])

    @pl.when(f == pl.num_programs(1) - 1)
    def _():
        o_ref[...] = _layer_norm(alpha * x_ref[...] + acc_sc[...], g_ref[...], b_ref[...])


def _dense_ffn(xb, x, wg, wu, wd, g, b, alpha, tm, tf):
    n = x.shape[0]
    rows = pl.BlockSpec((tm, D_MODEL), lambda i, f: (i, 0))
    vec = pl.BlockSpec((1, D_MODEL), lambda i, f: (0, 0))
    return pl.pallas_call(
        functools.partial(_ffn_kernel, alpha=alpha), grid=(n // tm, D_FF // tf),
        in_specs=[rows, rows, pl.BlockSpec((D_MODEL, tf), lambda i, f: (0, f)),
                  pl.BlockSpec((D_MODEL, tf), lambda i, f: (0, f)), pl.BlockSpec((tf, D_MODEL), lambda i, f: (f, 0)),
                  vec, vec],
        out_specs=rows, out_shape=jax.ShapeDtypeStruct((n, D_MODEL), F32),
        scratch_shapes=[pltpu.VMEM((tm, D_MODEL), F32)],
        compiler_params=_cparams("parallel", "arbitrary"), name="dense_ffn")(xb, x, wg, wu, wd, g, b)


def _router_kernel(x_ref, r_ref, idx_ref, gate_ref, cnt_ref, seen_sc):
    i = pl.program_id(0)

    @pl.when(i == 0)
    def _():
        seen_sc[...] = jnp.zeros_like(seen_sc)

    xs = _split3(x_ref[...])
    logits = jnp.zeros((x_ref.shape[0], LANES), F32)
    for a, b in ((2, 0), (1, 1), (0, 2), (1, 0), (0, 1), (0, 0)):
        logits = logits + jnp.dot(xs[a], r_ref[b], preferred_element_type=F32)
    lane = lax.broadcasted_iota(I32, logits.shape, 1)
    lg = jnp.where(lane < N_EXPERTS, logits, -jnp.inf)
    v1 = jnp.max(lg, axis=-1, keepdims=True)
    i1 = jnp.min(jnp.where(lg == v1, lane, LANES), axis=-1, keepdims=True)
    lg2 = jnp.where(lane == i1, -jnp.inf, lg)
    v2 = jnp.max(lg2, axis=-1, keepdims=True)
    i2 = jnp.min(jnp.where(lg2 == v2, lane, LANES), axis=-1, keepdims=True)
    e = jnp.exp(v2 - v1)
    g1 = 1.0 / (1.0 + e)
    g2 = e / (1.0 + e)
    hit1 = lane == i1
    hit2 = lane == i2
    cnt = jnp.where(hit1, 1.0, jnp.where(hit2, 1.0, 0.0))
    tm = cnt.shape[0]
    r = lax.broadcasted_iota(I32, (tm, tm), 0)
    c = lax.broadcasted_iota(I32, (tm, tm), 1)
    tri = jnp.where(c < r, 1.0, 0.0).astype(BF16)
    before = jnp.dot(tri, cnt.astype(BF16), preferred_element_type=F32) + seen_sc[...]
    rank1 = jnp.sum(jnp.where(hit1, before, 0.0), axis=-1, keepdims=True).astype(I32)
    rank2 = jnp.sum(jnp.where(hit2, before, 0.0), axis=-1, keepdims=True).astype(I32)
    seen = seen_sc[...] + jnp.sum(cnt, axis=0, keepdims=True)
    seen_sc[...] = seen
    idx_ref[...] = jnp.where(lane == 0, i1, jnp.where(lane == 1, i2, jnp.where(lane == 2, rank1,
                                                                                  jnp.where(lane == 3, rank2, 0))))
    gate_ref[...] = jnp.where(lane == 0, g1, jnp.where(lane == 1, g2, 0.0))
    cnt_ref[...] = jnp.broadcast_to(seen, cnt_ref.shape).astype(I32)


def _route(x, router, tm):
    n = x.shape[0]
    r_pad = jnp.concatenate([router, jnp.zeros((D_MODEL, LANES - N_EXPERTS), F32)], axis=1)
    r3 = jnp.stack(_split3(r_pad))
    rows = pl.BlockSpec((tm, LANES), lambda i: (i, 0))
    return pl.pallas_call(
        _router_kernel, grid=(n // tm,),
        in_specs=[pl.BlockSpec((tm, D_MODEL), lambda i: (i, 0)), pl.BlockSpec(r3.shape, lambda i: (0, 0, 0))],
        out_specs=[rows, rows, pl.BlockSpec((SUBLANES, LANES), lambda i: (0, 0))],
        out_shape=[jax.ShapeDtypeStruct((n, LANES), I32), jax.ShapeDtypeStruct((n, LANES), F32),
                   jax.ShapeDtypeStruct((SUBLANES, LANES), I32)],
        scratch_shapes=[pltpu.VMEM((1, LANES), F32)],
        compiler_params=_cparams("arbitrary"), name="moe_router")(x, r3)


def _row_copy(src_ref, src_row, dst_ref, dst_row, sem):
    return pltpu.make_async_copy(src_ref.at[pl.ds(src_row, 1), :], dst_ref.at[pl.ds(dst_row, 1), :], sem)


def _dispatch_kernel(slot_ref, x_ref, init_ref, xs_ref, sem, *, tm):
    del init_ref
    base = pl.program_id(0) * tm

    def start(r, carry):
        t = base + r
        _row_copy(x_ref, t, xs_ref, slot_ref[2 * t], sem).start()
        _row_copy(x_ref, t, xs_ref, slot_ref[2 * t + 1], sem).start()
        return carry

    lax.fori_loop(0, tm, start, 0)

    def wait(r, carry):
        _row_copy(x_ref, 0, xs_ref, 0, sem).wait()
        _row_copy(x_ref, 0, xs_ref, 0, sem).wait()
        return carry

    lax.fori_loop(0, tm, wait, 0)


def _dispatch(slots_flat, x, n_slots, tm):
    n = x.shape[0]
    any_spec = pl.BlockSpec(memory_space=pl.ANY)
    gs = pltpu.PrefetchScalarGridSpec(num_scalar_prefetch=1, grid=(n // tm,), in_specs=[any_spec, any_spec],
                                      out_specs=any_spec, scratch_shapes=[pltpu.SemaphoreType.DMA(())])
    return pl.pallas_call(
        functools.partial(_dispatch_kernel, tm=tm), grid_spec=gs,
        out_shape=jax.ShapeDtypeStruct((n_slots, D_MODEL), F32), input_output_aliases={2: 0},
        compiler_params=_cparams("arbitrary"), name="moe_dispatch",
    )(slots_flat, x, jnp.zeros((n_slots, D_MODEL), F32))


def _return_kernel(slot_ref, y_ref, y0_ref, y1_ref, sem, *, tm):
    base = pl.program_id(0) * tm

    def start(r, carry):
        t = base + r
        _row_copy(y_ref, slot_ref[2 * t], y0_ref, t, sem).start()
        _row_copy(y_ref, slot_ref[2 * t + 1], y1_ref, t, sem).start()
        return carry

    lax.fori_loop(0, tm, start, 0)

    def wait(r, carry):
        _row_copy(y_ref, 0, y0_ref, 0, sem).wait()
        _row_copy(y_ref, 0, y1_ref, 0, sem).wait()
        return carry

    lax.fori_loop(0, tm, wait, 0)


def _return_rows(slots_flat, y, n, tm):
    any_spec = pl.BlockSpec(memory_space=pl.ANY)
    gs = pltpu.PrefetchScalarGridSpec(num_scalar_prefetch=1, grid=(n // tm,), in_specs=[any_spec],
                                      out_specs=[any_spec, any_spec], scratch_shapes=[pltpu.SemaphoreType.DMA(())])
    return pl.pallas_call(
        functools.partial(_return_kernel, tm=tm), grid_spec=gs,
        out_shape=[jax.ShapeDtypeStruct((n, D_MODEL), F32)] * 2,
        compiler_params=_cparams("arbitrary"), name="moe_return")(slots_flat, y)


def _moe_ffn_kernel(be_ref, nv_ref, xs_ref, wg_ref, wu_ref, wd_ref, y_ref, xb_sc, acc_sc):
    del be_ref
    i = pl.program_id(0)
    f = pl.program_id(1)
    live = i < nv_ref[0]

    @pl.when(f == 0)
    def _():
        acc_sc[...] = jnp.zeros_like(acc_sc)
        xb_sc[...] = xs_ref[...].astype(BF16)

    @pl.when(live)
    def _():
        acc_sc[...] += _swiglu_partial(xb_sc[...], wg_ref[...], wu_ref[...], wd_ref[...])

    @pl.when(f == pl.num_programs(1) - 1)
    def _():
        y_ref[...] = acc_sc[...]


def _moe_ffn(block_expert, n_valid, xs, wg, wu, wd, tm, tf):
    p = xs.shape[0]
    rows = pl.BlockSpec((tm, D_MODEL), lambda i, f, be, nv: (i, 0))
    gs = pltpu.PrefetchScalarGridSpec(
        num_scalar_prefetch=2, grid=(p // tm, D_FF // tf),
        in_specs=[rows, pl.BlockSpec((None, D_MODEL, tf), lambda i, f, be, nv: (be[i], 0, f)),
                  pl.BlockSpec((None, D_MODEL, tf), lambda i, f, be, nv: (be[i], 0, f)),
                  pl.BlockSpec((None, tf, D_MODEL), lambda i, f, be, nv: (be[i], f, 0))],
        out_specs=rows, scratch_shapes=[pltpu.VMEM((tm, D_MODEL), BF16), pltpu.VMEM((tm, D_MODEL), F32)])
    return pl.pallas_call(
        _moe_ffn_kernel, grid_spec=gs, out_shape=jax.ShapeDtypeStruct((p, D_MODEL), F32),
        compiler_params=_cparams("parallel", "arbitrary"), name="moe_ffn")(block_expert, n_valid, xs, wg, wu, wd)


def _combine_kernel(x_ref, y0_ref, y1_ref, gate_ref, g_ref, b_ref, o_ref, *, alpha):
    gt = gate_ref[...]
    ff = gt[:, 0:1] * y0_ref[...] + gt[:, 1:2] * y1_ref[...]
    o_ref[...] = _layer_norm(alpha * x_ref[...] + ff, g_ref[...], b_ref[...])


def _combine(x, y0, y1, gates, g, b, alpha, tm):
    n = x.shape[0]
    rows = pl.BlockSpec((tm, D_MODEL), lambda i: (i, 0))
    vec = pl.BlockSpec((1, D_MODEL), lambda i: (0, 0))
    return pl.pallas_call(
        functools.partial(_combine_kernel, alpha=alpha), grid=(n // tm,),
        in_specs=[rows, rows, rows, pl.BlockSpec((tm, LANES), lambda i: (i, 0)), vec, vec], out_specs=rows,
        out_shape=jax.ShapeDtypeStruct((n, D_MODEL), F32), compiler_params=_cparams("parallel"),
        name="moe_combine")(x, y0, y1, gates, g, b)


def _moe_layer(x, router, wg, wu, wd, g, b, alpha, tiles):
    n = x.shape[0]
    tmb = tiles["moe_rows"]
    idx, gates, counts = _route(x, router, tiles["route_rows"])
    counts = counts[0, :N_EXPERTS]
    padded = ((counts + tmb - 1) // tmb) * tmb
    pad_end = jnp.cumsum(padded)
    pad_start = pad_end - padded
    slots = jnp.stack([pad_start[idx[:, 0]] + idx[:, 2], pad_start[idx[:, 1]] + idx[:, 3]], axis=1)
    slots_flat = slots.reshape(-1).astype(I32)
    n_slots = 2 * n + N_EXPERTS * tmb
    n_blocks = n_slots // tmb
    block_expert = jnp.clip(jnp.searchsorted(pad_end, jnp.arange(n_blocks, dtype=I32) * tmb, side="right"),
                            0, N_EXPERTS - 1).astype(I32)
    n_valid = (pad_end[-1:] // tmb).astype(I32)
    xs = _dispatch(slots_flat, x, n_slots, tiles["dma_rows"])
    y = _moe_ffn(block_expert, n_valid, xs, wg, wu, wd, tmb, tiles["ffn_cols"])
    y0, y1 = _return_rows(slots_flat, y, n, tiles["dma_rows"])
    return _combine(x, y0, y1, gates, g, b, alpha, tiles["row_tile"])


def _tiles(n, t):
    return dict(
        proj_rows=min(256, t), prefix_rows=min(256, t), attn_q=min(256, t), row_tile=min(512, n),
        ffn_cols=512, route_rows=min(256, n), moe_rows=min(512, n), dma_rows=min(256, n))


def kernel(x, positions, w_in, b_forget, w_branch_a, w_branch_b, w_out, ln_mix_g, ln_mix_b, ln_ffn_g, ln_ffn_b,
           ffn_w_gate, ffn_w_up, ffn_w_down, moe_router, moe_w_gate, moe_w_up, moe_w_down):
    B, T, D = x.shape
    assert D == D_MODEL and T % CHUNK == 0
    n = B * T
    depth = w_in.shape[0]
    top_k = min(TOPK_MAX, T // 4)
    alpha = (2.0 * depth) ** 0.25
    tiles = _tiles(n, T)
    cos_t, sin_t = _rope_tables(positions)
    xf = x.reshape(n, D)
    for layer in range(depth):
        weights = _prep_mix_weights(w_in[layer], b_forget[layer])
        (dq, dk, iq, fq, fk, ga, gb, ik2, dvt, fvt, iwt, lf) = _in_projection(
            xf.reshape(B, T, D), cos_t, sin_t, weights, tiles["proj_rows"])
        aux = _forget_aux(lf, tiles["prefix_rows"])
        o_a = _dsa_attention(iq, ik2, iwt, dq, dk, dvt, tiles["attn_q"], top_k)
        o_b = _fox_attention(fq, fk, aux, fvt, tiles["attn_q"])
        x1, x1b = _merge_project(
            o_a.reshape(n, HEADS_W), o_b.reshape(n, HEADS_W), ga.reshape(n, D), gb.reshape(n, D), xf,
            w_branch_a[layer].astype(BF16), w_branch_b[layer].astype(BF16), w_out[layer].astype(BF16),
            ln_mix_g[layer][None, :], ln_mix_b[layer][None, :], alpha, tiles["row_tile"])
        j = layer // 2
        g, b = ln_ffn_g[layer][None, :], ln_ffn_b[layer][None, :]
        if layer % 2 == 0:
            xf = _dense_ffn(x1b, x1, ffn_w_gate[j].astype(BF16), ffn_w_up[j].astype(BF16),
                            ffn_w_down[j].astype(BF16), g, b, alpha, tiles["row_tile"], tiles["ffn_cols"])
        else:
            xf = _moe_layer(x1, moe_router[j], moe_w_gate[j].astype(BF16), moe_w_up[j].astype(BF16),
                            moe_w_down[j].astype(BF16), g, b, alpha, tiles)
    return xf.reshape(B, T, D)
```

```python
import functools

import jax
import jax.numpy as jnp
from jax import lax
from jax.experimental import pallas as pl
from jax.experimental.pallas import tpu as pltpu

F32, BF16, I32 = jnp.float32, jnp.bfloat16, jnp.int32

D_MODEL = 1024
HEAD_DIM = 64
N_HEADS = 8
HEADS_W = N_HEADS * HEAD_DIM
N_PAIRS = N_HEADS // 2
CHUNK = 64
TOPK_MAX = 256
ROPE_THETA = 10000.0
D_FF = 3584
N_EXPERTS = 8
LN_EPS = 1e-5
DEPTH = 2
LANES = 128
SUBLANES = 8
INT_MIN = -(2**31)
NEG = -1e30
VMEM_LIMIT = 56 * 1024 * 1024
NT_DIMS = (((1,), (1,)), ((), ()))


def _cparams(*sem):
    return pltpu.CompilerParams(dimension_semantics=sem, vmem_limit_bytes=VMEM_LIMIT)


def _split3(v):
    hi = v.astype(BF16)
    r1 = v - hi.astype(F32)
    mid = r1.astype(BF16)
    lo = (r1 - mid.astype(F32)).astype(BF16)
    return hi, mid, lo


def _layer_norm(y, g, b):
    mu = jnp.mean(y, axis=-1, keepdims=True)
    yc = y - mu
    var = jnp.mean(yc * yc, axis=-1, keepdims=True)
    return yc * lax.rsqrt(var + LN_EPS) * g + b


def _rope_kernel(pos_ref, freq_ref, sign_ref, cos_ref, sin_ref):
    ang = pos_ref[...].astype(F32) * freq_ref[...]
    cos_ref[...] = jnp.cos(ang)
    sin_ref[...] = jnp.sin(ang) * sign_ref[...]


def _rope_tables(positions):
    B, T = positions.shape
    n = B * T
    inv_freq = ROPE_THETA ** (-jnp.arange(0, HEAD_DIM, 2, dtype=F32) / HEAD_DIM)
    freq = jnp.tile(inv_freq, LANES // (HEAD_DIM // 2))[None, :]
    half = HEAD_DIM // 2
    sign = jnp.tile(jnp.concatenate([-jnp.ones((half,), F32), jnp.ones((half,), F32)]), LANES // HEAD_DIM)[None, :]
    pos_b = jnp.broadcast_to(positions.reshape(n, 1), (n, LANES))
    tm = min(n, 1024)
    row = pl.BlockSpec((tm, LANES), lambda i: (i, 0))
    const = pl.BlockSpec((1, LANES), lambda i: (0, 0))
    cos, sin = pl.pallas_call(
        _rope_kernel, grid=(n // tm,), in_specs=[row, const, const], out_specs=[row, row],
        out_shape=[jax.ShapeDtypeStruct((n, LANES), F32)] * 2, compiler_params=_cparams("parallel"),
        name="rope_tables")(pos_b, freq, sign)
    return cos.reshape(B, T, LANES), sin.reshape(B, T, LANES)


LOG2E = 1.4426950408889634
_Q_SCALE = HEAD_DIM ** -0.5 * LOG2E
_IW_SCALE = (N_HEADS ** -0.5) * (HEAD_DIM ** -0.5)
_OFF_DQ, _OFF_DK, _OFF_IQ, _OFF_FQ, _OFF_FK, _OFF_GA, _OFF_GB = 0, 512, 1024, 1536, 2048, 2560, 3584
_W_MAIN = 4608
_AUX_USED = 6


def _inproj_kernel(x_ref, cos_ref, sin_ref, wm_ref, wik_ref, wvt_ref, wit_ref, wfl_ref, bfl_ref,
                   dq_ref, dk_ref, iq_ref, fq_ref, fk_ref, ga_ref, gb_ref, ik2_ref, dvt_ref, fvt_ref,
                   iwt_ref, lf_ref):
    xb = x_ref[...].astype(BF16)
    cos = cos_ref[...]
    sin = sin_ref[...]
    lane = lax.broadcasted_iota(I32, cos.shape, 1)
    first_half = (lane & (HEAD_DIM - 1)) < HEAD_DIM // 2

    def rot(y):
        partner = jnp.where(first_half, pltpu.roll(y, LANES - HEAD_DIM // 2, 1), pltpu.roll(y, HEAD_DIM // 2, 1))
        return y * cos + partner * sin

    def proj(off, width):
        return jnp.dot(xb, wm_ref[:, off:off + width], preferred_element_type=F32)

    def rot_group(off, scale, out_ref):
        y = proj(off, HEADS_W)
        for j in range(N_PAIRS):
            r = rot(y[:, LANES * j:LANES * (j + 1)])
            if scale != 1.0:
                r = r * scale
            out_ref[:, LANES * j:LANES * (j + 1)] = r.astype(BF16)

    rot_group(_OFF_DQ, _Q_SCALE, dq_ref)
    rot_group(_OFF_DK, 1.0, dk_ref)
    rot_group(_OFF_IQ, 1.0, iq_ref)
    fq_ref[...] = (proj(_OFF_FQ, HEADS_W) * _Q_SCALE).astype(BF16)
    fk_ref[...] = proj(_OFF_FK, HEADS_W).astype(BF16)
    ga_ref[...] = jax.nn.sigmoid(proj(_OFF_GA, D_MODEL)).astype(BF16)
    gb_ref[...] = jax.nn.sigmoid(proj(_OFF_GB, D_MODEL)).astype(BF16)
    ik2_ref[...] = rot(jnp.dot(xb, wik_ref[...], preferred_element_type=F32)).astype(BF16)
    vt = lax.dot_general(wvt_ref[...], xb, NT_DIMS, preferred_element_type=F32)
    dvt_ref[...] = vt[:HEADS_W].astype(BF16)
    fvt_ref[...] = vt[HEADS_W:].astype(BF16)
    it = lax.dot_general(wit_ref[...], xb, NT_DIMS, preferred_element_type=F32)
    iwt_ref[...] = it[:N_HEADS] * _IW_SCALE
    z = jnp.dot(xb, wfl_ref[...], preferred_element_type=F32) + bfl_ref[...]
    lf = jnp.minimum(z, 0.0) - jnp.log1p(jnp.exp(-jnp.abs(z)))
    lane_w = lax.broadcasted_iota(I32, lf.shape, 1) & (LANES - 1)
    lf_ref[...] = jnp.where(lane_w < _AUX_USED, lf, 0.0)


def _prep_mix_weights(w_in, b_forget):
    sizes = (HEADS_W, HEADS_W, HEADS_W, HEADS_W, HEAD_DIM, N_HEADS, HEADS_W, HEADS_W, HEADS_W, N_HEADS, D_MODEL, D_MODEL)
    offs = [0]
    for s in sizes:
        offs.append(offs[-1] + s)
    dq, dk, dv, iq, ik, iw, fq, fk, fv, fl, ga, gb = [w_in[:, offs[i]:offs[i + 1]] for i in range(12)]
    w_main = jnp.concatenate([dq, dk, iq, fq, fk, ga, gb], axis=1).astype(BF16)
    w_ik2 = jnp.concatenate([ik, ik], axis=1).astype(BF16)
    w_vt = jnp.concatenate([dv, fv], axis=1).T.astype(BF16)
    w_it = jnp.concatenate([iw.T, jnp.zeros((16 - N_HEADS, D_MODEL), F32)], axis=0).astype(BF16)
    src = []
    for j in range(N_PAIRS):
        src += [2 * j] * 3 + [2 * j + 1] * 3 + [-1] * (LANES - _AUX_USED)
    src = jnp.array(src, I32)
    used = src >= 0
    w_fl = jnp.where(used[None, :], fl[:, jnp.maximum(src, 0)], 0.0).astype(BF16)
    b_fl = jnp.where(used, b_forget[jnp.maximum(src, 0)], 0.0)[None, :].astype(F32)
    return w_main, w_ik2, w_vt, w_it, w_fl, b_fl


def _in_projection(x, cos_t, sin_t, weights, tm):
    B, T, _ = x.shape
    w_main, w_ik2, w_vt, w_it, w_fl, b_fl = weights
    nt = T // tm

    def rows(width):
        return pl.BlockSpec((None, tm, width), lambda b, i: (b, i, 0))

    def cols(height):
        return pl.BlockSpec((None, height, tm), lambda b, i: (b, 0, i))

    def const(shape):
        return pl.BlockSpec(shape, lambda b, i: (0, 0), pipeline_mode=pl.Buffered(1))

    bt = lambda w, dt: jax.ShapeDtypeStruct((B, T, w), dt)
    tb = lambda h, dt: jax.ShapeDtypeStruct((B, h, T), dt)
    return pl.pallas_call(
        _inproj_kernel, grid=(B, nt),
        in_specs=[rows(D_MODEL), rows(LANES), rows(LANES), const(w_main.shape), const(w_ik2.shape),
                  const(w_vt.shape), const(w_it.shape), const(w_fl.shape), const(b_fl.shape)],
        out_specs=[rows(HEADS_W)] * 5 + [rows(D_MODEL)] * 2 + [rows(LANES), cols(HEADS_W), cols(HEADS_W),
                                                                cols(N_HEADS), rows(HEADS_W)],
        out_shape=[bt(HEADS_W, BF16)] * 5 + [bt(D_MODEL, BF16)] * 2 + [bt(LANES, BF16), tb(HEADS_W, BF16),
                                                                      tb(HEADS_W, BF16), tb(N_HEADS, F32),
                                                                      bt(HEADS_W, F32)],
        compiler_params=_cparams("parallel", "parallel"), name="in_projection",
    )(x, cos_t, sin_t, w_main, w_ik2, w_vt, w_it, w_fl, b_fl)


def _cumsum_kernel(lf_ref, aux_ref, carry_ref):
    @pl.when(pl.program_id(1) == 0)
    def _():
        carry_ref[...] = jnp.zeros_like(carry_ref)

    lf = lf_ref[...]
    tb = lf.shape[0]
    r = lax.broadcasted_iota(I32, (tb, tb), 0)
    c = lax.broadcasted_iota(I32, (tb, tb), 1)
    tri = jnp.where(c <= r, 1.0, 0.0).astype(BF16)
    hi, mid, lo = _split3(lf)
    cs = (jnp.dot(tri, lo, preferred_element_type=F32) + jnp.dot(tri, mid, preferred_element_type=F32)
          + jnp.dot(tri, hi, preferred_element_type=F32) + carry_ref[...])
    carry_ref[...] = cs[tb - 1:tb, :]
    nh, nm, nl = _split3(-LOG2E * cs)
    lane_w = lax.broadcasted_iota(I32, cs.shape, 1) & (LANES - 1)
    k = jnp.where(lane_w >= 3, lane_w - 3, lane_w)
    piece = jnp.where(k == 0, nh.astype(F32), jnp.where(k == 1, nm.astype(F32), nl.astype(F32)))
    aux_ref[...] = jnp.where(lane_w < _AUX_USED, piece, 0.0).astype(BF16)


def _forget_aux(lf, tb):
    B, T, W = lf.shape
    spec = pl.BlockSpec((None, tb, W), lambda b, i: (b, i, 0))
    return pl.pallas_call(
        _cumsum_kernel, grid=(B, T // tb), in_specs=[spec], out_specs=spec,
        out_shape=jax.ShapeDtypeStruct((B, T, W), BF16), scratch_shapes=[pltpu.VMEM((1, W), F32)],
        compiler_params=_cparams("parallel", "arbitrary"), name="forget_prefix")(lf)


def _stack_pair(qb):
    qf = qb.astype(F32)
    lane = lax.broadcasted_iota(I32, qf.shape, 1)
    lo = lane < HEAD_DIM
    return jnp.concatenate([jnp.where(lo, qf, 0.0), jnp.where(lo, 0.0, qf)], axis=0)


def _softmax_step(s, vb, j, m_sc, l_sc, acc_sc):
    m_old = m_sc[j]
    m_new = jnp.maximum(m_old, jnp.max(s, axis=0, keepdims=True))
    alpha = jnp.exp2(m_old - m_new)
    p = jnp.exp2(s - m_new)
    l_sc[j] = alpha * l_sc[j] + jnp.sum(p, axis=0, keepdims=True)
    acc_sc[j] = alpha * acc_sc[j] + jnp.dot(vb, p.astype(BF16), preferred_element_type=F32)
    m_sc[j] = m_new


def _init_softmax(m_sc, l_sc, acc_sc):
    m_sc[...] = jnp.full(m_sc.shape, -jnp.inf, F32)
    l_sc[...] = jnp.zeros(l_sc.shape, F32)
    acc_sc[...] = jnp.zeros(acc_sc.shape, F32)


def _finish_pair(o_ref, j, tq, l_sc, acc_sc):
    out_t = acc_sc[j] / l_sc[j]
    blk_t = jnp.concatenate([out_t[:HEAD_DIM, :tq], out_t[HEAD_DIM:, tq:]], axis=0)
    o_ref[:, LANES * j:LANES * (j + 1)] = blk_t.T.astype(BF16)


def _softmax_scratch(tq, q_width):
    return [pltpu.VMEM((N_PAIRS, q_width, 2 * tq), BF16), pltpu.VMEM((2, N_PAIRS, tq, 2 * tq), F32),
            pltpu.VMEM((N_PAIRS, 1, 2 * tq), F32), pltpu.VMEM((N_PAIRS, 1, 2 * tq), F32),
            pltpu.VMEM((N_PAIRS, LANES, 2 * tq), F32)]


def _pipelined_attention(n_steps, scores, update):
    scores(0, 0)

    def body(kv, carry):
        slot = kv & 1
        scores(kv + 1, 1 - slot)
        update(kv, slot, False)
        return carry

    lax.fori_loop(0, n_steps - 1, body, 0)
    update(n_steps - 1, (n_steps - 1) & 1, True)


def _fox_kernel(q_ref, k_ref, aux_ref, vt_ref, o_ref, qt_sc, s_sc, m_sc, l_sc, acc_sc, *, tq):
    qi = pl.program_id(1)
    q0 = qi * tq
    row2 = lax.broadcasted_iota(I32, (2 * tq, LANES), 0)
    lane2 = lax.broadcasted_iota(I32, (2 * tq, LANES), 1)
    sel = jnp.where(row2 < tq, jnp.where(lane2 < 3, 1.0, 0.0),
                    jnp.where(lane2 < 3, 0.0, jnp.where(lane2 < _AUX_USED, 1.0, 0.0)))
    for j in range(N_PAIRS):
        qa = jnp.concatenate([_stack_pair(q_ref[:, LANES * j:LANES * (j + 1)]), sel], axis=1)
        qt_sc[j] = qa.T.astype(BF16)
    _init_softmax(m_sc, l_sc, acc_sc)

    def scores(kv, slot):
        ks = pl.multiple_of(kv * tq, tq)
        for j in range(N_PAIRS):
            cs = slice(LANES * j, LANES * (j + 1))
            kb = jnp.concatenate([k_ref[pl.ds(ks, tq), cs], aux_ref[pl.ds(ks, tq), cs]], axis=1)
            s_sc[slot, j] = jnp.dot(kb, qt_sc[j], preferred_element_type=F32)

    def update(kv, slot, last):
        ks = pl.multiple_of(kv * tq, tq)
        if last:
            krow = ks + lax.broadcasted_iota(I32, (tq, 2 * tq), 0)
            qcol = q0 + (lax.broadcasted_iota(I32, (tq, 2 * tq), 1) & (tq - 1))
            keep = krow <= qcol
        for j in range(N_PAIRS):
            s = s_sc[slot, j]
            if last:
                s = jnp.where(keep, s, NEG)
            _softmax_step(s, vt_ref[LANES * j:LANES * (j + 1), pl.ds(ks, tq)], j, m_sc, l_sc, acc_sc)

    _pipelined_attention(qi + 1, scores, update)
    for j in range(N_PAIRS):
        _finish_pair(o_ref, j, tq, l_sc, acc_sc)


def _fox_attention(fq, fk, aux, fvt, tq):
    B, T, W = fq.shape
    qspec = pl.BlockSpec((None, tq, W), lambda b, i: (b, i, 0))
    full = pl.BlockSpec((None, T, W), lambda b, i: (b, 0, 0))
    fullt = pl.BlockSpec((None, W, T), lambda b, i: (b, 0, 0))
    return pl.pallas_call(
        functools.partial(_fox_kernel, tq=tq), grid=(B, T // tq),
        in_specs=[qspec, full, full, fullt], out_specs=qspec,
        out_shape=jax.ShapeDtypeStruct((B, T, W), BF16), scratch_shapes=_softmax_scratch(tq, 2 * LANES),
        compiler_params=_cparams("parallel", "arbitrary"), name="fox_attention")(fq, fk, aux, fvt)


_COUNT_CHAINS = 4


def _dsa_kernel(iq_ref, ik2_ref, iwt_ref, q_ref, k_ref, vt_ref, o_ref, key_sc, bias_sc, qt_sc, s_sc, m_sc, l_sc,
                acc_sc, *, tq, top_k):
    qi = pl.program_id(1)
    q0 = qi * tq
    col = lax.broadcasted_iota(I32, (1, tq), 1)
    limit = (lax.shift_right_logical(q0 + col, CHUNK.bit_length() - 1) + 1) * CHUNK
    iw = iwt_ref[...]

    for j in range(N_PAIRS):
        qt_sc[j] = _stack_pair(iq_ref[:, LANES * j:LANES * (j + 1)]).T.astype(BF16)

    def score_tile(kv, masked):
        ks = pl.multiple_of(kv * tq, tq)
        kb = ik2_ref[pl.ds(ks, tq), :]
        sc = jnp.zeros((tq, tq), F32)
        for j in range(N_PAIRS):
            s2 = jnp.dot(kb, qt_sc[j], preferred_element_type=F32)
            sc = sc + jnp.maximum(s2[:, :tq], 0.0) * iw[2 * j:2 * j + 1, :]
            sc = sc + jnp.maximum(s2[:, tq:], 0.0) * iw[2 * j + 1:2 * j + 2, :]
        bits = pltpu.bitcast(sc, I32)
        key = bits ^ ((bits >> 31) & jnp.int32(0x7FFFFFFF))
        key = jnp.where(bits == jnp.int32(INT_MIN), 0, key)
        if masked:
            krow = ks + lax.broadcasted_iota(I32, key.shape, 0)
            key = jnp.where(krow < limit, key, jnp.int32(INT_MIN))
        key_sc[pl.ds(ks, tq), :] = key

    def score_body(kv, carry):
        score_tile(kv, False)
        return carry

    lax.fori_loop(0, qi, score_body, 0)
    score_tile(qi, True)

    def count(pred):
        def body(g, accs):
            slab = key_sc[pl.ds(pl.multiple_of(g * tq, tq), tq), :]
            accs = list(accs)
            for r in range(tq // SUBLANES):
                blk = slab[SUBLANES * r:SUBLANES * (r + 1), :]
                accs[r % _COUNT_CHAINS] = accs[r % _COUNT_CHAINS] + jnp.where(pred(blk), 1, 0)
            return tuple(accs)

        zero = jnp.zeros((SUBLANES, tq), I32)
        accs = lax.fori_loop(0, qi + 1, body, (zero,) * _COUNT_CHAINS)
        return jnp.sum(sum(accs[1:], accs[0]), axis=0, keepdims=True)

    def search_body(i, v):
        cand = v + lax.shift_left(jnp.int32(1), 31 - i)
        cand8 = jnp.broadcast_to(cand, (SUBLANES, tq))
        cnt = count(lambda blk: blk >= cand8)
        return jnp.where(cnt >= top_k, cand, v)

    thr = lax.fori_loop(0, 32, search_body, jnp.full((1, tq), INT_MIN, I32))
    thr8 = jnp.broadcast_to(thr, (SUBLANES, tq))
    need = (top_k - count(lambda blk: blk > thr8)).astype(F32)

    r = lax.broadcasted_iota(I32, (tq, tq), 0)
    c = lax.broadcasted_iota(I32, (tq, tq), 1)
    tri = jnp.where(c < r, 1.0, 0.0).astype(BF16)

    def bias_tile(kv, seen, masked):
        ks = pl.multiple_of(kv * tq, tq)
        key = key_sc[pl.ds(ks, tq), :]
        eqf = jnp.where(key == thr, 1.0, 0.0)
        before = jnp.dot(tri, eqf.astype(BF16), preferred_element_type=F32) + seen
        b = jnp.where(key > thr, 0.0, jnp.where(key == thr, jnp.where(before < need, 0.0, NEG), NEG))
        if masked:
            krow = ks + lax.broadcasted_iota(I32, key.shape, 0)
            b = jnp.where(krow < limit, b, NEG)
        bias_sc[pl.ds(ks, tq), :] = b
        return seen + jnp.sum(eqf, axis=0, keepdims=True)

    seen = lax.fori_loop(0, qi, lambda kv, s: bias_tile(kv, s, False), jnp.zeros((1, tq), F32))
    bias_tile(qi, seen, True)

    for j in range(N_PAIRS):
        qt_sc[j] = _stack_pair(q_ref[:, LANES * j:LANES * (j + 1)]).T.astype(BF16)
    _init_softmax(m_sc, l_sc, acc_sc)

    def scores(kv, slot):
        ks = pl.multiple_of(kv * tq, tq)
        for j in range(N_PAIRS):
            s_sc[slot, j] = jnp.dot(k_ref[pl.ds(ks, tq), LANES * j:LANES * (j + 1)], qt_sc[j],
                                    preferred_element_type=F32)

    def update(kv, slot, last):
        del last
        ks = pl.multiple_of(kv * tq, tq)
        b = bias_sc[pl.ds(ks, tq), :]
        b2 = jnp.concatenate([b, b], axis=1)
        for j in range(N_PAIRS):
            _softmax_step(s_sc[slot, j] + b2, vt_ref[LANES * j:LANES * (j + 1), pl.ds(ks, tq)], j, m_sc, l_sc, acc_sc)

    _pipelined_attention(qi + 1, scores, update)
    for j in range(N_PAIRS):
        _finish_pair(o_ref, j, tq, l_sc, acc_sc)


def _dsa_attention(iq, ik2, iwt, dq, dk, dvt, tq, top_k):
    B, T, W = dq.shape
    qspec = pl.BlockSpec((None, tq, W), lambda b, i: (b, i, 0))
    full = pl.BlockSpec((None, T, W), lambda b, i: (b, 0, 0))
    fullt = pl.BlockSpec((None, W, T), lambda b, i: (b, 0, 0))
    return pl.pallas_call(
        functools.partial(_dsa_kernel, tq=tq, top_k=top_k), grid=(B, T // tq),
        in_specs=[qspec, pl.BlockSpec((None, T, LANES), lambda b, i: (b, 0, 0)),
                  pl.BlockSpec((None, N_HEADS, tq), lambda b, i: (b, 0, i)), qspec, full, fullt],
        out_specs=qspec, out_shape=jax.ShapeDtypeStruct((B, T, W), BF16),
        scratch_shapes=[pltpu.VMEM((T, tq), I32), pltpu.VMEM((T, tq), F32)] + _softmax_scratch(tq, LANES),
        compiler_params=_cparams("parallel", "arbitrary"), name="dsa_attention")(iq, ik2, iwt, dq, dk, dvt)


TOKEN_ROWS = D_MODEL // LANES
assert TOKEN_ROWS == SUBLANES


def _store_token_tiles(ref, y):
    tm = y.shape[0]
    for s in range(TOKEN_ROWS):
        ref[pl.ds(s, tm, stride=TOKEN_ROWS), :] = y[:, LANES * s:LANES * (s + 1)]


def _load_token_tiles(ref, tm):
    return jnp.concatenate([ref[pl.ds(s, tm, stride=TOKEN_ROWS), :] for s in range(TOKEN_ROWS)], axis=1)


def _merge_kernel(oa_ref, ob_ref, ga_ref, gb_ref, x_ref, wa_ref, wb_ref, wo_ref, g_ref, b_ref, o_ref, o2_ref,
                  *, alpha, token_tiles):
    a = jnp.dot(oa_ref[...], wa_ref[...], preferred_element_type=F32)
    b = jnp.dot(ob_ref[...], wb_ref[...], preferred_element_type=F32)
    merged = ga_ref[...].astype(F32) * a + gb_ref[...].astype(F32) * b
    mix = jnp.dot(merged.astype(BF16), wo_ref[...], preferred_element_type=F32)
    y = _layer_norm(alpha * x_ref[...] + mix, g_ref[...], b_ref[...])
    o_ref[...] = y
    if token_tiles:
        _store_token_tiles(o2_ref, y)
    else:
        o2_ref[...] = y.astype(BF16)


def _merge_project(oa, ob, ga, gb, x, wa, wb, wo, g, b, alpha, tm, token_tiles):
    n = x.shape[0]

    def rows(w):
        return pl.BlockSpec((tm, w), lambda i: (i, 0))

    def const(shape):
        return pl.BlockSpec(shape, lambda i: (0, 0), pipeline_mode=pl.Buffered(1))

    if token_tiles:
        out2_spec = pl.BlockSpec((tm * TOKEN_ROWS, LANES), lambda i: (i, 0))
        out2_shape = jax.ShapeDtypeStruct((n * TOKEN_ROWS, LANES), F32)
    else:
        out2_spec, out2_shape = rows(D_MODEL), jax.ShapeDtypeStruct((n, D_MODEL), BF16)
    return pl.pallas_call(
        functools.partial(_merge_kernel, alpha=alpha, token_tiles=token_tiles), grid=(n // tm,),
        in_specs=[rows(HEADS_W), rows(HEADS_W), rows(D_MODEL), rows(D_MODEL), rows(D_MODEL), const(wa.shape),
                  const(wb.shape), const(wo.shape), const(g.shape), const(b.shape)],
        out_specs=[rows(D_MODEL), out2_spec], out_shape=[jax.ShapeDtypeStruct((n, D_MODEL), F32), out2_shape],
        compiler_params=_cparams("parallel"), name="merge_project")(oa, ob, ga, gb, x, wa, wb, wo, g, b)


def _swiglu_partial(xb, wg, wu, wd):
    h = jnp.dot(xb, wg, preferred_element_type=F32)
    u = jnp.dot(xb, wu, preferred_element_type=F32)
    hid = (h * jax.nn.sigmoid(h)) * u
    return jnp.dot(hid.astype(BF16), wd, preferred_element_type=F32)


def _ffn_kernel(xb_ref, x_ref, wg_ref, wu_ref, wd_ref, g_ref, b_ref, o_ref, acc_sc, *, alpha):
    f = pl.program_id(1)

    @pl.when(f == 0)
    def _():
        acc_sc[...] = jnp.zeros_like(acc_sc)

    acc_sc[...] += _swiglu_partial(xb_ref[...], wg_ref[...], wu_ref[...], wd_ref[...])

    @pl.when(f == pl.num_programs(1) - 1)
    def _():
        o_ref[...] = _layer_norm(alpha * x_ref[...] + acc_sc[...], g_ref[...], b_ref[...])


def _dense_ffn(xb, x, wg, wu, wd, g, b, alpha, tm, tf):
    n = x.shape[0]
    rows = pl.BlockSpec((tm, D_MODEL), lambda i, f: (i, 0))
    vec = pl.BlockSpec((1, D_MODEL), lambda i, f: (0, 0))
    return pl.pallas_call(
        functools.partial(_ffn_kernel, alpha=alpha), grid=(n // tm, D_FF // tf),
        in_specs=[rows, rows, pl.BlockSpec((D_MODEL, tf), lambda i, f: (0, f)),
                  pl.BlockSpec((D_MODEL, tf), lambda i, f: (0, f)), pl.BlockSpec((tf, D_MODEL), lambda i, f: (f, 0)),
                  vec, vec],
        out_specs=rows, out_shape=jax.ShapeDtypeStruct((n, D_MODEL), F32),
        scratch_shapes=[pltpu.VMEM((tm, D_MODEL), F32)],
        compiler_params=_cparams("parallel", "arbitrary"), name="dense_ffn")(xb, x, wg, wu, wd, g, b)


def _router_kernel(x_ref, r_ref, idx_ref, gate_ref, cnt_ref, seen_sc):
    i = pl.program_id(0)

    @pl.when(i == 0)
    def _():
        seen_sc[...] = jnp.zeros_like(seen_sc)

    xs = _split3(x_ref[...])
    logits = jnp.zeros((x_ref.shape[0], LANES), F32)
    for a, b in ((2, 0), (1, 1), (0, 2), (1, 0), (0, 1), (0, 0)):
        logits = logits + jnp.dot(xs[a], r_ref[b], preferred_element_type=F32)
    lane = lax.broadcasted_iota(I32, logits.shape, 1)
    lg = jnp.where(lane < N_EXPERTS, logits, -jnp.inf)
    v1 = jnp.max(lg, axis=-1, keepdims=True)
    i1 = jnp.min(jnp.where(lg == v1, lane, LANES), axis=-1, keepdims=True)
    lg2 = jnp.where(lane == i1, -jnp.inf, lg)
    v2 = jnp.max(lg2, axis=-1, keepdims=True)
    i2 = jnp.min(jnp.where(lg2 == v2, lane, LANES), axis=-1, keepdims=True)
    e = jnp.exp(v2 - v1)
    g1 = 1.0 / (1.0 + e)
    g2 = e / (1.0 + e)
    hit1 = lane == i1
    hit2 = lane == i2
    cnt = jnp.where(hit1, 1.0, jnp.where(hit2, 1.0, 0.0))
    tm = cnt.shape[0]
    r = lax.broadcasted_iota(I32, (tm, tm), 0)
    c = lax.broadcasted_iota(I32, (tm, tm), 1)
    tri = jnp.where(c < r, 1.0, 0.0).astype(BF16)
    before = jnp.dot(tri, cnt.astype(BF16), preferred_element_type=F32) + seen_sc[...]
    rank1 = jnp.sum(jnp.where(hit1, before, 0.0), axis=-1, keepdims=True).astype(I32)
    rank2 = jnp.sum(jnp.where(hit2, before, 0.0), axis=-1, keepdims=True).astype(I32)
    seen = seen_sc[...] + jnp.sum(cnt, axis=0, keepdims=True)
    seen_sc[...] = seen
    idx_ref[...] = jnp.where(lane == 0, i1, jnp.where(lane == 1, i2, jnp.where(lane == 2, rank1,
                                                                                  jnp.where(lane == 3, rank2, 0))))
    gate_ref[...] = jnp.where(lane == 0, g1, jnp.where(lane == 1, g2, 0.0))
    cnt_ref[...] = jnp.broadcast_to(seen, cnt_ref.shape).astype(I32)


def _route(x, router, tm):
    n = x.shape[0]
    r_pad = jnp.concatenate([router, jnp.zeros((D_MODEL, LANES - N_EXPERTS), F32)], axis=1)
    r3 = jnp.stack(_split3(r_pad))
    rows = pl.BlockSpec((tm, LANES), lambda i: (i, 0))
    return pl.pallas_call(
        _router_kernel, grid=(n // tm,),
        in_specs=[pl.BlockSpec((tm, D_MODEL), lambda i: (i, 0)), pl.BlockSpec(r3.shape, lambda i: (0, 0, 0))],
        out_specs=[rows, rows, pl.BlockSpec((SUBLANES, LANES), lambda i: (0, 0))],
        out_shape=[jax.ShapeDtypeStruct((n, LANES), I32), jax.ShapeDtypeStruct((n, LANES), F32),
                   jax.ShapeDtypeStruct((SUBLANES, LANES), I32)],
        scratch_shapes=[pltpu.VMEM((1, LANES), F32)],
        compiler_params=_cparams("arbitrary"), name="moe_router")(x, r3)


def _tile_rows(idx):
    return pl.ds(pl.multiple_of(idx * TOKEN_ROWS, TOKEN_ROWS), TOKEN_ROWS)


def _dispatch_kernel(slot_ref, x_ref, init_ref, xs_ref, sem, *, tm):
    del init_ref
    base = pl.program_id(0) * tm

    def start(r, carry):
        src = x_ref.at[_tile_rows(r), :]
        for j in range(2):
            pltpu.make_async_copy(src, xs_ref.at[_tile_rows(slot_ref[2 * (base + r) + j]), :], sem).start()
        return carry

    lax.fori_loop(0, tm, start, 0, unroll=8)
    for _ in range(2):
        pltpu.make_async_copy(x_ref, xs_ref.at[pl.ds(0, tm * TOKEN_ROWS), :], sem).wait()


def _dispatch(slots_flat, xt, n_slots, tm):
    n = xt.shape[0] // TOKEN_ROWS
    any_spec = pl.BlockSpec(memory_space=pl.ANY)
    gs = pltpu.PrefetchScalarGridSpec(
        num_scalar_prefetch=1, grid=(n // tm,),
        in_specs=[pl.BlockSpec((tm * TOKEN_ROWS, LANES), lambda i, slots: (i, 0)), any_spec],
        out_specs=any_spec, scratch_shapes=[pltpu.SemaphoreType.DMA(())])
    return pl.pallas_call(
        functools.partial(_dispatch_kernel, tm=tm), grid_spec=gs,
        out_shape=jax.ShapeDtypeStruct((n_slots * TOKEN_ROWS, LANES), F32), input_output_aliases={2: 0},
        compiler_params=_cparams("arbitrary"), name="moe_dispatch",
    )(slots_flat, xt, jnp.zeros((n_slots * TOKEN_ROWS, LANES), F32))


def _moe_ffn_kernel(be_ref, nv_ref, xs_ref, wg_ref, wu_ref, wd_ref, y_ref, xb_sc, acc_sc):
    del be_ref
    i = pl.program_id(0)
    f = pl.program_id(1)
    live = i < nv_ref[0]
    tm = xb_sc.shape[0]

    @pl.when(f == 0)
    def _():
        acc_sc[...] = jnp.zeros_like(acc_sc)
        xb_sc[...] = _load_token_tiles(xs_ref, tm).astype(BF16)

    @pl.when(live)
    def _():
        acc_sc[...] += _swiglu_partial(xb_sc[...], wg_ref[...], wu_ref[...], wd_ref[...])

    @pl.when(f == pl.num_programs(1) - 1)
    def _():
        _store_token_tiles(y_ref, acc_sc[...])


def _moe_ffn(block_expert, n_valid, xs, wg, wu, wd, tm, tf):
    p = xs.shape[0] // TOKEN_ROWS
    tiles = pl.BlockSpec((tm * TOKEN_ROWS, LANES), lambda i, f, be, nv: (i, 0))
    gs = pltpu.PrefetchScalarGridSpec(
        num_scalar_prefetch=2, grid=(p // tm, D_FF // tf),
        in_specs=[tiles, pl.BlockSpec((None, D_MODEL, tf), lambda i, f, be, nv: (be[i], 0, f)),
                  pl.BlockSpec((None, D_MODEL, tf), lambda i, f, be, nv: (be[i], 0, f)),
                  pl.BlockSpec((None, tf, D_MODEL), lambda i, f, be, nv: (be[i], f, 0))],
        out_specs=tiles, scratch_shapes=[pltpu.VMEM((tm, D_MODEL), BF16), pltpu.VMEM((tm, D_MODEL), F32)])
    return pl.pallas_call(
        _moe_ffn_kernel, grid_spec=gs, out_shape=jax.ShapeDtypeStruct(xs.shape, F32),
        compiler_params=_cparams("parallel", "arbitrary"), name="moe_ffn")(block_expert, n_valid, xs, wg, wu, wd)


def _combine_kernel(slot_ref, x_ref, gate_ref, g_ref, b_ref, y_ref, o_ref, y0_sc, y1_sc, sem, *, alpha, tm):
    base = pl.program_id(0) * tm

    def start(r, carry):
        for j, buf in enumerate((y0_sc, y1_sc)):
            src = y_ref.at[_tile_rows(slot_ref[2 * (base + r) + j]), :]
            pltpu.make_async_copy(src, buf.at[_tile_rows(r), :], sem).start()
        return carry

    lax.fori_loop(0, tm, start, 0, unroll=8)
    for buf in (y0_sc, y1_sc):
        pltpu.make_async_copy(y_ref.at[pl.ds(0, tm * TOKEN_ROWS), :], buf, sem).wait()
    gt = gate_ref[...]
    ff = gt[:, 0:1] * _load_token_tiles(y0_sc, tm) + gt[:, 1:2] * _load_token_tiles(y1_sc, tm)
    o_ref[...] = _layer_norm(alpha * x_ref[...] + ff, g_ref[...], b_ref[...])


def _combine(slots_flat, x, gates, g, b, y, alpha, tm):
    n = x.shape[0]
    rows = pl.BlockSpec((tm, D_MODEL), lambda i, slots: (i, 0))
    vec = pl.BlockSpec((1, D_MODEL), lambda i, slots: (0, 0))
    tile_buf = pltpu.VMEM((tm * TOKEN_ROWS, LANES), F32)
    gs = pltpu.PrefetchScalarGridSpec(
        num_scalar_prefetch=1, grid=(n // tm,),
        in_specs=[rows, pl.BlockSpec((tm, LANES), lambda i, slots: (i, 0)), vec, vec,
                  pl.BlockSpec(memory_space=pl.ANY)],
        out_specs=rows, scratch_shapes=[tile_buf, tile_buf, pltpu.SemaphoreType.DMA(())])
    return pl.pallas_call(
        functools.partial(_combine_kernel, alpha=alpha, tm=tm), grid_spec=gs,
        out_shape=jax.ShapeDtypeStruct((n, D_MODEL), F32), compiler_params=_cparams("arbitrary"),
        name="moe_combine")(slots_flat, x, gates, g, b, y)


def _moe_layer(x, xt, router, wg, wu, wd, g, b, alpha, tiles):
    n = x.shape[0]
    tmb = tiles["moe_rows"]
    idx, gates, counts = _route(x, router, tiles["route_rows"])
    counts = counts[0, :N_EXPERTS]
    padded = ((counts + tmb - 1) // tmb) * tmb
    pad_end = jnp.cumsum(padded)
    pad_start = pad_end - padded
    slots = jnp.stack([pad_start[idx[:, 0]] + idx[:, 2], pad_start[idx[:, 1]] + idx[:, 3]], axis=1)
    slots_flat = slots.reshape(-1).astype(I32)
    n_slots = 2 * n + N_EXPERTS * tmb
    n_blocks = n_slots // tmb
    block_expert = jnp.clip(jnp.searchsorted(pad_end, jnp.arange(n_blocks, dtype=I32) * tmb, side="right"),
                            0, N_EXPERTS - 1).astype(I32)
    n_valid = (pad_end[-1:] // tmb).astype(I32)
    xs = _dispatch(slots_flat, xt, n_slots, tiles["dma_rows"])
    y = _moe_ffn(block_expert, n_valid, xs, wg, wu, wd, tmb, tiles["ffn_cols"])
    return _combine(slots_flat, x, gates, g, b, y, alpha, tiles["dma_rows"])


def _tiles(n, t):
    return dict(
        proj_rows=min(256, t), prefix_rows=min(256, t), attn_q=min(256, t), row_tile=min(512, n),
        ffn_cols=512, route_rows=min(256, n), moe_rows=min(512, n), dma_rows=min(256, n))


def kernel(x, positions, w_in, b_forget, w_branch_a, w_branch_b, w_out, ln_mix_g, ln_mix_b, ln_ffn_g, ln_ffn_b,
           ffn_w_gate, ffn_w_up, ffn_w_down, moe_router, moe_w_gate, moe_w_up, moe_w_down):
    B, T, D = x.shape
    assert D == D_MODEL and T % CHUNK == 0
    n = B * T
    depth = w_in.shape[0]
    top_k = min(TOPK_MAX, T // 4)
    alpha = (2.0 * depth) ** 0.25
    tiles = _tiles(n, T)
    cos_t, sin_t = _rope_tables(positions)
    xf = x.reshape(n, D)
    for layer in range(depth):
        weights = _prep_mix_weights(w_in[layer], b_forget[layer])
        (dq, dk, iq, fq, fk, ga, gb, ik2, dvt, fvt, iwt, lf) = _in_projection(
            xf.reshape(B, T, D), cos_t, sin_t, weights, tiles["proj_rows"])
        aux = _forget_aux(lf, tiles["prefix_rows"])
        o_a = _dsa_attention(iq, ik2, iwt, dq, dk, dvt, tiles["attn_q"], top_k)
        o_b = _fox_attention(fq, fk, aux, fvt, tiles["attn_q"])
        dense = layer % 2 == 0
        x1, x1_alt = _merge_project(
            o_a.reshape(n, HEADS_W), o_b.reshape(n, HEADS_W), ga.reshape(n, D), gb.reshape(n, D), xf,
            w_branch_a[layer].astype(BF16), w_branch_b[layer].astype(BF16), w_out[layer].astype(BF16),
            ln_mix_g[layer][None, :], ln_mix_b[layer][None, :], alpha, tiles["row_tile"], token_tiles=not dense)
        j = layer // 2
        g, b = ln_ffn_g[layer][None, :], ln_ffn_b[layer][None, :]
        if dense:
            xf = _dense_ffn(x1_alt, x1, ffn_w_gate[j].astype(BF16), ffn_w_up[j].astype(BF16),
                            ffn_w_down[j].astype(BF16), g, b, alpha, tiles["row_tile"], tiles["ffn_cols"])
        else:
            xf = _moe_layer(x1, x1_alt, moe_router[j], moe_w_gate[j].astype(BF16), moe_w_up[j].astype(BF16),
                            moe_w_down[j].astype(BF16), g, b, alpha, tiles)
    return xf.reshape(B, T, D)
```

```python
import functools

import jax
import jax.numpy as jnp
from jax import lax
from jax.experimental import pallas as pl
from jax.experimental.pallas import tpu as pltpu

F32, BF16, I32, I16 = jnp.float32, jnp.bfloat16, jnp.int32, jnp.int16

D_MODEL = 1024
HEAD_DIM = 64
N_HEADS = 8
HEADS_W = N_HEADS * HEAD_DIM
N_PAIRS = N_HEADS // 2
CHUNK = 64
TOPK_MAX = 256
ROPE_THETA = 10000.0
D_FF = 3584
N_EXPERTS = 8
LN_EPS = 1e-5
DEPTH = 2
LANES = 128
SUBLANES = 8
PACKED_ROWS = 2 * SUBLANES
INT_MIN = -(2**31)
I16_MIN = -(2**15)
NEG = -1e30
VMEM_LIMIT = 56 * 1024 * 1024
NT_DIMS = (((1,), (1,)), ((), ()))


def _cparams(*sem):
    return pltpu.CompilerParams(dimension_semantics=sem, vmem_limit_bytes=VMEM_LIMIT)


def _split3(v):
    hi = v.astype(BF16)
    r1 = v - hi.astype(F32)
    mid = r1.astype(BF16)
    lo = (r1 - mid.astype(F32)).astype(BF16)
    return hi, mid, lo


def _layer_norm(y, g, b):
    mu = jnp.mean(y, axis=-1, keepdims=True)
    yc = y - mu
    var = jnp.mean(yc * yc, axis=-1, keepdims=True)
    return yc * lax.rsqrt(var + LN_EPS) * g + b


def _rope_kernel(pos_ref, freq_ref, sign_ref, cos_ref, sin_ref):
    ang = pos_ref[...].astype(F32) * freq_ref[...]
    cos_ref[...] = jnp.cos(ang)
    sin_ref[...] = jnp.sin(ang) * sign_ref[...]


def _rope_tables(positions):
    B, T = positions.shape
    n = B * T
    inv_freq = ROPE_THETA ** (-jnp.arange(0, HEAD_DIM, 2, dtype=F32) / HEAD_DIM)
    freq = jnp.tile(inv_freq, LANES // (HEAD_DIM // 2))[None, :]
    half = HEAD_DIM // 2
    sign = jnp.tile(jnp.concatenate([-jnp.ones((half,), F32), jnp.ones((half,), F32)]), LANES // HEAD_DIM)[None, :]
    pos_b = jnp.broadcast_to(positions.reshape(n, 1), (n, LANES))
    tm = min(n, 1024)
    row = pl.BlockSpec((tm, LANES), lambda i: (i, 0))
    const = pl.BlockSpec((1, LANES), lambda i: (0, 0))
    cos, sin = pl.pallas_call(
        _rope_kernel, grid=(n // tm,), in_specs=[row, const, const], out_specs=[row, row],
        out_shape=[jax.ShapeDtypeStruct((n, LANES), F32)] * 2, compiler_params=_cparams("parallel"),
        name="rope_tables")(pos_b, freq, sign)
    return cos.reshape(B, T, LANES), sin.reshape(B, T, LANES)


LOG2E = 1.4426950408889634
_Q_SCALE = HEAD_DIM ** -0.5 * LOG2E
_IW_SCALE = (N_HEADS ** -0.5) * (HEAD_DIM ** -0.5)
_OFF_DQ, _OFF_DK, _OFF_IQ, _OFF_FQ, _OFF_FK, _OFF_GA, _OFF_GB = 0, 512, 1024, 1536, 2048, 2560, 3584
_W_MAIN = 4608
_AUX_USED = 6


def _inproj_kernel(x_ref, cos_ref, sin_ref, wm_ref, wik_ref, wvt_ref, wit_ref, wfl_ref, bfl_ref,
                   dq_ref, dk_ref, iq_ref, fq_ref, fk_ref, ga_ref, gb_ref, ik2_ref, dvt_ref, fvt_ref,
                   iwt_ref, lf_ref):
    xb = x_ref[...].astype(BF16)
    cos = cos_ref[...]
    sin = sin_ref[...]
    lane = lax.broadcasted_iota(I32, cos.shape, 1)
    first_half = (lane & (HEAD_DIM - 1)) < HEAD_DIM // 2

    def rot(y):
        partner = jnp.where(first_half, pltpu.roll(y, LANES - HEAD_DIM // 2, 1), pltpu.roll(y, HEAD_DIM // 2, 1))
        return y * cos + partner * sin

    def proj(off, width):
        return jnp.dot(xb, wm_ref[:, off:off + width], preferred_element_type=F32)

    def rot_group(off, scale, out_ref):
        y = proj(off, HEADS_W)
        for j in range(N_PAIRS):
            r = rot(y[:, LANES * j:LANES * (j + 1)])
            if scale != 1.0:
                r = r * scale
            out_ref[:, LANES * j:LANES * (j + 1)] = r.astype(BF16)

    rot_group(_OFF_DQ, _Q_SCALE, dq_ref)
    rot_group(_OFF_DK, 1.0, dk_ref)
    rot_group(_OFF_IQ, 1.0, iq_ref)
    fq_ref[...] = (proj(_OFF_FQ, HEADS_W) * _Q_SCALE).astype(BF16)
    fk_ref[...] = proj(_OFF_FK, HEADS_W).astype(BF16)
    ga_ref[...] = jax.nn.sigmoid(proj(_OFF_GA, D_MODEL)).astype(BF16)
    gb_ref[...] = jax.nn.sigmoid(proj(_OFF_GB, D_MODEL)).astype(BF16)
    ik2_ref[...] = rot(jnp.dot(xb, wik_ref[...], preferred_element_type=F32)).astype(BF16)
    vt = lax.dot_general(wvt_ref[...], xb, NT_DIMS, preferred_element_type=F32)
    dvt_ref[...] = vt[:HEADS_W].astype(BF16)
    fvt_ref[...] = vt[HEADS_W:].astype(BF16)
    it = lax.dot_general(wit_ref[...], xb, NT_DIMS, preferred_element_type=F32)
    iwt_ref[...] = it[:N_HEADS] * _IW_SCALE
    z = jnp.dot(xb, wfl_ref[...], preferred_element_type=F32) + bfl_ref[...]
    lf = jnp.minimum(z, 0.0) - jnp.log1p(jnp.exp(-jnp.abs(z)))
    lane_w = lax.broadcasted_iota(I32, lf.shape, 1) & (LANES - 1)
    lf_ref[...] = jnp.where(lane_w < _AUX_USED, lf, 0.0)


def _prep_mix_weights(w_in, b_forget):
    sizes = (HEADS_W, HEADS_W, HEADS_W, HEADS_W, HEAD_DIM, N_HEADS, HEADS_W, HEADS_W, HEADS_W, N_HEADS, D_MODEL, D_MODEL)
    offs = [0]
    for s in sizes:
        offs.append(offs[-1] + s)
    dq, dk, dv, iq, ik, iw, fq, fk, fv, fl, ga, gb = [w_in[:, offs[i]:offs[i + 1]] for i in range(12)]
    w_main = jnp.concatenate([dq, dk, iq, fq, fk, ga, gb], axis=1).astype(BF16)
    w_ik2 = jnp.concatenate([ik, ik], axis=1).astype(BF16)
    w_vt = jnp.concatenate([dv, fv], axis=1).T.astype(BF16)
    w_it = jnp.concatenate([iw.T, jnp.zeros((16 - N_HEADS, D_MODEL), F32)], axis=0).astype(BF16)
    src = []
    for j in range(N_PAIRS):
        src += [2 * j] * 3 + [2 * j + 1] * 3 + [-1] * (LANES - _AUX_USED)
    src = jnp.array(src, I32)
    used = src >= 0
    w_fl = jnp.where(used[None, :], fl[:, jnp.maximum(src, 0)], 0.0).astype(BF16)
    b_fl = jnp.where(used, b_forget[jnp.maximum(src, 0)], 0.0)[None, :].astype(F32)
    return w_main, w_ik2, w_vt, w_it, w_fl, b_fl


def _in_projection(x, cos_t, sin_t, weights, tm):
    B, T, _ = x.shape
    w_main, w_ik2, w_vt, w_it, w_fl, b_fl = weights
    nt = T // tm

    def rows(width):
        return pl.BlockSpec((None, tm, width), lambda b, i: (b, i, 0))

    def cols(height):
        return pl.BlockSpec((None, height, tm), lambda b, i: (b, 0, i))

    def const(shape):
        return pl.BlockSpec(shape, lambda b, i: (0, 0), pipeline_mode=pl.Buffered(1))

    bt = lambda w, dt: jax.ShapeDtypeStruct((B, T, w), dt)
    tb = lambda h, dt: jax.ShapeDtypeStruct((B, h, T), dt)
    return pl.pallas_call(
        _inproj_kernel, grid=(B, nt),
        in_specs=[rows(D_MODEL), rows(LANES), rows(LANES), const(w_main.shape), const(w_ik2.shape),
                  const(w_vt.shape), const(w_it.shape), const(w_fl.shape), const(b_fl.shape)],
        out_specs=[rows(HEADS_W)] * 5 + [rows(D_MODEL)] * 2 + [rows(LANES), cols(HEADS_W), cols(HEADS_W),
                                                                cols(N_HEADS), rows(HEADS_W)],
        out_shape=[bt(HEADS_W, BF16)] * 5 + [bt(D_MODEL, BF16)] * 2 + [bt(LANES, BF16), tb(HEADS_W, BF16),
                                                                      tb(HEADS_W, BF16), tb(N_HEADS, F32),
                                                                      bt(HEADS_W, F32)],
        compiler_params=_cparams("parallel", "parallel"), name="in_projection",
    )(x, cos_t, sin_t, w_main, w_ik2, w_vt, w_it, w_fl, b_fl)


def _cumsum_kernel(lf_ref, aux_ref, carry_ref):
    @pl.when(pl.program_id(1) == 0)
    def _():
        carry_ref[...] = jnp.zeros_like(carry_ref)

    lf = lf_ref[...]
    tb = lf.shape[0]
    r = lax.broadcasted_iota(I32, (tb, tb), 0)
    c = lax.broadcasted_iota(I32, (tb, tb), 1)
    tri = jnp.where(c <= r, 1.0, 0.0).astype(BF16)
    hi, mid, lo = _split3(lf)
    cs = (jnp.dot(tri, lo, preferred_element_type=F32) + jnp.dot(tri, mid, preferred_element_type=F32)
          + jnp.dot(tri, hi, preferred_element_type=F32) + carry_ref[...])
    carry_ref[...] = cs[tb - 1:tb, :]
    nh, nm, nl = _split3(-LOG2E * cs)
    lane_w = lax.broadcasted_iota(I32, cs.shape, 1) & (LANES - 1)
    k = jnp.where(lane_w >= 3, lane_w - 3, lane_w)
    piece = jnp.where(k == 0, nh.astype(F32), jnp.where(k == 1, nm.astype(F32), nl.astype(F32)))
    aux_ref[...] = jnp.where(lane_w < _AUX_USED, piece, 0.0).astype(BF16)


def _forget_aux(lf, tb):
    B, T, W = lf.shape
    spec = pl.BlockSpec((None, tb, W), lambda b, i: (b, i, 0))
    return pl.pallas_call(
        _cumsum_kernel, grid=(B, T // tb), in_specs=[spec], out_specs=spec,
        out_shape=jax.ShapeDtypeStruct((B, T, W), BF16), scratch_shapes=[pltpu.VMEM((1, W), F32)],
        compiler_params=_cparams("parallel", "arbitrary"), name="forget_prefix")(lf)


def _stack_pair(qb):
    qf = qb.astype(F32)
    lane = lax.broadcasted_iota(I32, qf.shape, 1)
    lo = lane < HEAD_DIM
    return jnp.concatenate([jnp.where(lo, qf, 0.0), jnp.where(lo, 0.0, qf)], axis=0)


def _softmax_step(s, vb, j, m_sc, l_sc, acc_sc):
    m_old = m_sc[j]
    m_new = jnp.maximum(m_old, jnp.max(s, axis=0, keepdims=True))
    alpha = jnp.exp2(m_old - m_new)
    p = jnp.exp2(s - m_new)
    l_sc[j] = alpha * l_sc[j] + jnp.sum(p, axis=0, keepdims=True)
    acc_sc[j] = alpha * acc_sc[j] + jnp.dot(vb, p.astype(BF16), preferred_element_type=F32)
    m_sc[j] = m_new


def _init_softmax(m_sc, l_sc, acc_sc):
    m_sc[...] = jnp.full(m_sc.shape, -jnp.inf, F32)
    l_sc[...] = jnp.zeros(l_sc.shape, F32)
    acc_sc[...] = jnp.zeros(acc_sc.shape, F32)


def _finish_pair(o_ref, j, tq, l_sc, acc_sc):
    out_t = acc_sc[j] / l_sc[j]
    blk_t = jnp.concatenate([out_t[:HEAD_DIM, :tq], out_t[HEAD_DIM:, tq:]], axis=0)
    o_ref[:, LANES * j:LANES * (j + 1)] = blk_t.T.astype(BF16)


def _softmax_scratch(tq, q_width):
    return [pltpu.VMEM((N_PAIRS, q_width, 2 * tq), BF16), pltpu.VMEM((2, N_PAIRS, tq, 2 * tq), F32),
            pltpu.VMEM((N_PAIRS, 1, 2 * tq), F32), pltpu.VMEM((N_PAIRS, 1, 2 * tq), F32),
            pltpu.VMEM((N_PAIRS, LANES, 2 * tq), F32)]


def _pipelined_attention(n_steps, scores, update):
    scores(0, 0)

    def body(kv, carry):
        slot = kv & 1
        scores(kv + 1, 1 - slot)
        update(kv, slot, False)
        return carry

    lax.fori_loop(0, n_steps - 1, body, 0)
    update(n_steps - 1, (n_steps - 1) & 1, True)


def _fox_kernel(q_ref, k_ref, aux_ref, vt_ref, o_ref, qt_sc, s_sc, m_sc, l_sc, acc_sc, *, tq):
    qi = pl.program_id(1)
    q0 = qi * tq
    row2 = lax.broadcasted_iota(I32, (2 * tq, LANES), 0)
    lane2 = lax.broadcasted_iota(I32, (2 * tq, LANES), 1)
    sel = jnp.where(row2 < tq, jnp.where(lane2 < 3, 1.0, 0.0),
                    jnp.where(lane2 < 3, 0.0, jnp.where(lane2 < _AUX_USED, 1.0, 0.0)))
    for j in range(N_PAIRS):
        qa = jnp.concatenate([_stack_pair(q_ref[:, LANES * j:LANES * (j + 1)]), sel], axis=1)
        qt_sc[j] = qa.T.astype(BF16)
    _init_softmax(m_sc, l_sc, acc_sc)

    def scores(kv, slot):
        ks = pl.multiple_of(kv * tq, tq)
        for j in range(N_PAIRS):
            cs = slice(LANES * j, LANES * (j + 1))
            kb = jnp.concatenate([k_ref[pl.ds(ks, tq), cs], aux_ref[pl.ds(ks, tq), cs]], axis=1)
            s_sc[slot, j] = jnp.dot(kb, qt_sc[j], preferred_element_type=F32)

    def update(kv, slot, last):
        ks = pl.multiple_of(kv * tq, tq)
        if last:
            krow = ks + lax.broadcasted_iota(I32, (tq, 2 * tq), 0)
            qcol = q0 + (lax.broadcasted_iota(I32, (tq, 2 * tq), 1) & (tq - 1))
            keep = krow <= qcol
        for j in range(N_PAIRS):
            s = s_sc[slot, j]
            if last:
                s = jnp.where(keep, s, NEG)
            _softmax_step(s, vt_ref[LANES * j:LANES * (j + 1), pl.ds(ks, tq)], j, m_sc, l_sc, acc_sc)

    _pipelined_attention(qi + 1, scores, update)
    for j in range(N_PAIRS):
        _finish_pair(o_ref, j, tq, l_sc, acc_sc)


def _fox_attention(fq, fk, aux, fvt, tq):
    B, T, W = fq.shape
    qspec = pl.BlockSpec((None, tq, W), lambda b, i: (b, i, 0))
    full = pl.BlockSpec((None, T, W), lambda b, i: (b, 0, 0))
    fullt = pl.BlockSpec((None, W, T), lambda b, i: (b, 0, 0))
    return pl.pallas_call(
        functools.partial(_fox_kernel, tq=tq), grid=(B, T // tq),
        in_specs=[qspec, full, full, fullt], out_specs=qspec,
        out_shape=jax.ShapeDtypeStruct((B, T, W), BF16), scratch_shapes=_softmax_scratch(tq, 2 * LANES),
        compiler_params=_cparams("parallel", "arbitrary"), name="fox_attention")(fq, fk, aux, fvt)


_COUNT_CHAINS = 4


def _dsa_kernel(iq_ref, ik2_ref, iwt_ref, q_ref, k_ref, vt_ref, o_ref, key_sc, hi_sc, lo_sc, bias_sc, qt_sc, s_sc,
                m_sc, l_sc, acc_sc, *, tq, top_k):
    qi = pl.program_id(1)
    q0 = qi * tq
    col = lax.broadcasted_iota(I32, (1, tq), 1)
    limit = (lax.shift_right_logical(q0 + col, CHUNK.bit_length() - 1) + 1) * CHUNK
    iw = iwt_ref[...]

    for j in range(N_PAIRS):
        qt_sc[j] = _stack_pair(iq_ref[:, LANES * j:LANES * (j + 1)]).T.astype(BF16)

    def score_tile(kv, masked):
        ks = pl.multiple_of(kv * tq, tq)
        kb = ik2_ref[pl.ds(ks, tq), :]
        sc = jnp.zeros((tq, tq), F32)
        for j in range(N_PAIRS):
            s2 = jnp.dot(kb, qt_sc[j], preferred_element_type=F32)
            sc = sc + jnp.maximum(s2[:, :tq], 0.0) * iw[2 * j:2 * j + 1, :]
            sc = sc + jnp.maximum(s2[:, tq:], 0.0) * iw[2 * j + 1:2 * j + 2, :]
        bits = pltpu.bitcast(sc, I32)
        key = bits ^ ((bits >> 31) & jnp.int32(0x7FFFFFFF))
        key = jnp.where(bits == jnp.int32(INT_MIN), 0, key)
        if masked:
            krow = ks + lax.broadcasted_iota(I32, key.shape, 0)
            key = jnp.where(krow < limit, key, jnp.int32(INT_MIN))
        key_sc[pl.ds(ks, tq), :] = key
        hi_sc[pl.ds(ks, tq), :] = (key >> 16).astype(I16)
        lo_sc[pl.ds(ks, tq), :] = ((key & 0xFFFF) + I16_MIN).astype(I16)

    def score_body(kv, carry):
        score_tile(kv, False)
        return carry

    lax.fori_loop(0, qi, score_body, 0)
    score_tile(qi, True)

    def count(plane_sc, pred):
        def body(g, accs):
            slab = plane_sc[pl.ds(pl.multiple_of(g * tq, tq), tq), :]
            accs = list(accs)
            for r in range(tq // PACKED_ROWS):
                blk = slab[PACKED_ROWS * r:PACKED_ROWS * (r + 1), :]
                hit = jnp.where(pred(blk), jnp.int16(1), jnp.int16(0))
                accs[r % _COUNT_CHAINS] = accs[r % _COUNT_CHAINS] + hit
            return tuple(accs)

        zero = jnp.zeros((PACKED_ROWS, tq), I16)
        accs = lax.fori_loop(0, qi + 1, body, (zero,) * _COUNT_CHAINS)
        return jnp.sum(sum(a.astype(I32) for a in accs), axis=0, keepdims=True)

    def rows16(v):
        return jnp.broadcast_to(v.astype(I16), (PACKED_ROWS, tq))

    def search(plane_sc, above):
        def body(i, v):
            cand = v + lax.shift_left(jnp.int32(1), 15 - i)
            cand16 = rows16(cand)
            cnt = above + count(plane_sc, lambda blk: blk >= cand16)
            return jnp.where(cnt >= top_k, cand, v)

        return lax.fori_loop(0, 16, body, jnp.full((1, tq), I16_MIN, I32))

    t_hi = search(hi_sc, 0)
    t_hi16 = rows16(t_hi)
    above_hi = count(hi_sc, lambda blk: blk > t_hi16)

    def mask_low(g, carry):
        rows = pl.ds(pl.multiple_of(g * tq, tq), tq)
        lo_sc[rows, :] = jnp.where(hi_sc[rows, :] == t_hi.astype(I16), lo_sc[rows, :], jnp.int16(I16_MIN))
        return carry

    lax.fori_loop(0, qi + 1, mask_low, 0)
    t_lo = search(lo_sc, above_hi)
    t_lo16 = rows16(t_lo)
    thr = t_hi * 65536 + (t_lo - I16_MIN)
    need = (top_k - above_hi - count(lo_sc, lambda blk: blk > t_lo16)).astype(F32)

    r = lax.broadcasted_iota(I32, (tq, tq), 0)
    c = lax.broadcasted_iota(I32, (tq, tq), 1)
    tri = jnp.where(c < r, 1.0, 0.0).astype(BF16)

    def bias_tile(kv, seen, masked):
        ks = pl.multiple_of(kv * tq, tq)
        key = key_sc[pl.ds(ks, tq), :]
        eqf = jnp.where(key == thr, 1.0, 0.0)
        before = jnp.dot(tri, eqf.astype(BF16), preferred_element_type=F32) + seen
        b = jnp.where(key > thr, 0.0, jnp.where(key == thr, jnp.where(before < need, 0.0, NEG), NEG))
        if masked:
            krow = ks + lax.broadcasted_iota(I32, key.shape, 0)
            b = jnp.where(krow < limit, b, NEG)
        bias_sc[pl.ds(ks, tq), :] = b
        return seen + jnp.sum(eqf, axis=0, keepdims=True)

    seen = lax.fori_loop(0, qi, lambda kv, s: bias_tile(kv, s, False), jnp.zeros((1, tq), F32))
    bias_tile(qi, seen, True)

    for j in range(N_PAIRS):
        qt_sc[j] = _stack_pair(q_ref[:, LANES * j:LANES * (j + 1)]).T.astype(BF16)
    _init_softmax(m_sc, l_sc, acc_sc)

    def scores(kv, slot):
        ks = pl.multiple_of(kv * tq, tq)
        for j in range(N_PAIRS):
            s_sc[slot, j] = jnp.dot(k_ref[pl.ds(ks, tq), LANES * j:LANES * (j + 1)], qt_sc[j],
                                    preferred_element_type=F32)

    def update(kv, slot, last):
        del last
        ks = pl.multiple_of(kv * tq, tq)
        b = bias_sc[pl.ds(ks, tq), :]
        b2 = jnp.concatenate([b, b], axis=1)
        for j in range(N_PAIRS):
            _softmax_step(s_sc[slot, j] + b2, vt_ref[LANES * j:LANES * (j + 1), pl.ds(ks, tq)], j, m_sc, l_sc, acc_sc)

    _pipelined_attention(qi + 1, scores, update)
    for j in range(N_PAIRS):
        _finish_pair(o_ref, j, tq, l_sc, acc_sc)


def _dsa_attention(iq, ik2, iwt, dq, dk, dvt, tq, top_k):
    B, T, W = dq.shape
    qspec = pl.BlockSpec((None, tq, W), lambda b, i: (b, i, 0))
    full = pl.BlockSpec((None, T, W), lambda b, i: (b, 0, 0))
    fullt = pl.BlockSpec((None, W, T), lambda b, i: (b, 0, 0))
    return pl.pallas_call(
        functools.partial(_dsa_kernel, tq=tq, top_k=top_k), grid=(B, T // tq),
        in_specs=[qspec, pl.BlockSpec((None, T, LANES), lambda b, i: (b, 0, 0)),
                  pl.BlockSpec((None, N_HEADS, tq), lambda b, i: (b, 0, i)), qspec, full, fullt],
        out_specs=qspec, out_shape=jax.ShapeDtypeStruct((B, T, W), BF16),
        scratch_shapes=[pltpu.VMEM((T, tq), I32), pltpu.VMEM((T, tq), I16), pltpu.VMEM((T, tq), I16),
                        pltpu.VMEM((T, tq), F32)] + _softmax_scratch(tq, LANES),
        compiler_params=_cparams("parallel", "arbitrary"), name="dsa_attention")(iq, ik2, iwt, dq, dk, dvt)


TOKEN_ROWS = D_MODEL // LANES
assert TOKEN_ROWS == SUBLANES


def _store_token_tiles(ref, y):
    tm = y.shape[0]
    for s in range(TOKEN_ROWS):
        ref[pl.ds(s, tm, stride=TOKEN_ROWS), :] = y[:, LANES * s:LANES * (s + 1)]


def _load_token_tiles(ref, tm):
    return jnp.concatenate([ref[pl.ds(s, tm, stride=TOKEN_ROWS), :] for s in range(TOKEN_ROWS)], axis=1)


def _merge_kernel(oa_ref, ob_ref, ga_ref, gb_ref, x_ref, wa_ref, wb_ref, wo_ref, g_ref, b_ref, o_ref, o2_ref,
                  *, alpha, token_tiles):
    a = jnp.dot(oa_ref[...], wa_ref[...], preferred_element_type=F32)
    b = jnp.dot(ob_ref[...], wb_ref[...], preferred_element_type=F32)
    merged = ga_ref[...].astype(F32) * a + gb_ref[...].astype(F32) * b
    mix = jnp.dot(merged.astype(BF16), wo_ref[...], preferred_element_type=F32)
    y = _layer_norm(alpha * x_ref[...] + mix, g_ref[...], b_ref[...])
    o_ref[...] = y
    if token_tiles:
        _store_token_tiles(o2_ref, y)
    else:
        o2_ref[...] = y.astype(BF16)


def _merge_project(oa, ob, ga, gb, x, wa, wb, wo, g, b, alpha, tm, token_tiles):
    n = x.shape[0]

    def rows(w):
        return pl.BlockSpec((tm, w), lambda i: (i, 0))

    def const(shape):
        return pl.BlockSpec(shape, lambda i: (0, 0), pipeline_mode=pl.Buffered(1))

    if token_tiles:
        out2_spec = pl.BlockSpec((tm * TOKEN_ROWS, LANES), lambda i: (i, 0))
        out2_shape = jax.ShapeDtypeStruct((n * TOKEN_ROWS, LANES), F32)
    else:
        out2_spec, out2_shape = rows(D_MODEL), jax.ShapeDtypeStruct((n, D_MODEL), BF16)
    return pl.pallas_call(
        functools.partial(_merge_kernel, alpha=alpha, token_tiles=token_tiles), grid=(n // tm,),
        in_specs=[rows(HEADS_W), rows(HEADS_W), rows(D_MODEL), rows(D_MODEL), rows(D_MODEL), const(wa.shape),
                  const(wb.shape), const(wo.shape), const(g.shape), const(b.shape)],
        out_specs=[rows(D_MODEL), out2_spec], out_shape=[jax.ShapeDtypeStruct((n, D_MODEL), F32), out2_shape],
        compiler_params=_cparams("parallel"), name="merge_project")(oa, ob, ga, gb, x, wa, wb, wo, g, b)


def _swiglu_partial(xb, wg, wu, wd):
    h = jnp.dot(xb, wg, preferred_element_type=F32)
    u = jnp.dot(xb, wu, preferred_element_type=F32)
    hid = (h * jax.nn.sigmoid(h)) * u
    return jnp.dot(hid.astype(BF16), wd, preferred_element_type=F32)


def _swiglu(xb, wg_ref, wu_ref, wd_ref, tf):
    acc = None
    for f in range(D_FF // tf):
        part = _swiglu_partial(xb, wg_ref[:, f * tf:(f + 1) * tf], wu_ref[:, f * tf:(f + 1) * tf],
                               wd_ref[f * tf:(f + 1) * tf, :])
        acc = part if acc is None else acc + part
    return acc


def _ffn_kernel(xb_ref, x_ref, wg_ref, wu_ref, wd_ref, g_ref, b_ref, o_ref, *, alpha, tf):
    ff = _swiglu(xb_ref[...], wg_ref, wu_ref, wd_ref, tf)
    o_ref[...] = _layer_norm(alpha * x_ref[...] + ff, g_ref[...], b_ref[...])


def _dense_ffn(xb, x, wg, wu, wd, g, b, alpha, tm, tf):
    n = x.shape[0]
    rows = pl.BlockSpec((tm, D_MODEL), lambda i: (i, 0))

    def const(shape):
        return pl.BlockSpec(shape, lambda i: (0, 0), pipeline_mode=pl.Buffered(1))

    return pl.pallas_call(
        functools.partial(_ffn_kernel, alpha=alpha, tf=tf), grid=(n // tm,),
        in_specs=[rows, rows, const(wg.shape), const(wu.shape), const(wd.shape), const(g.shape), const(b.shape)],
        out_specs=rows, out_shape=jax.ShapeDtypeStruct((n, D_MODEL), F32),
        compiler_params=_cparams("parallel"), name="dense_ffn")(xb, x, wg, wu, wd, g, b)


def _router_kernel(x_ref, r_ref, idx_ref, gate_ref, cnt_ref, seen_sc):
    i = pl.program_id(0)

    @pl.when(i == 0)
    def _():
        seen_sc[...] = jnp.zeros_like(seen_sc)

    xs = _split3(x_ref[...])
    logits = jnp.zeros((x_ref.shape[0], LANES), F32)
    for a, b in ((2, 0), (1, 1), (0, 2), (1, 0), (0, 1), (0, 0)):
        logits = logits + jnp.dot(xs[a], r_ref[b], preferred_element_type=F32)
    lane = lax.broadcasted_iota(I32, logits.shape, 1)
    lg = jnp.where(lane < N_EXPERTS, logits, -jnp.inf)
    v1 = jnp.max(lg, axis=-1, keepdims=True)
    i1 = jnp.min(jnp.where(lg == v1, lane, LANES), axis=-1, keepdims=True)
    lg2 = jnp.where(lane == i1, -jnp.inf, lg)
    v2 = jnp.max(lg2, axis=-1, keepdims=True)
    i2 = jnp.min(jnp.where(lg2 == v2, lane, LANES), axis=-1, keepdims=True)
    e = jnp.exp(v2 - v1)
    g1 = 1.0 / (1.0 + e)
    g2 = e / (1.0 + e)
    hit1 = lane == i1
    hit2 = lane == i2
    cnt = jnp.where(hit1, 1.0, jnp.where(hit2, 1.0, 0.0))
    tm = cnt.shape[0]
    r = lax.broadcasted_iota(I32, (tm, tm), 0)
    c = lax.broadcasted_iota(I32, (tm, tm), 1)
    tri = jnp.where(c < r, 1.0, 0.0).astype(BF16)
    before = jnp.dot(tri, cnt.astype(BF16), preferred_element_type=F32) + seen_sc[...]
    rank1 = jnp.sum(jnp.where(hit1, before, 0.0), axis=-1, keepdims=True).astype(I32)
    rank2 = jnp.sum(jnp.where(hit2, before, 0.0), axis=-1, keepdims=True).astype(I32)
    seen = seen_sc[...] + jnp.sum(cnt, axis=0, keepdims=True)
    seen_sc[...] = seen
    idx_ref[...] = jnp.where(lane == 0, i1, jnp.where(lane == 1, i2, jnp.where(lane == 2, rank1,
                                                                                  jnp.where(lane == 3, rank2, 0))))
    gate_ref[...] = jnp.where(lane == 0, g1, jnp.where(lane == 1, g2, 0.0))
    cnt_ref[...] = jnp.broadcast_to(seen, cnt_ref.shape).astype(I32)


def _route(x, router, tm):
    n = x.shape[0]
    r_pad = jnp.concatenate([router, jnp.zeros((D_MODEL, LANES - N_EXPERTS), F32)], axis=1)
    r3 = jnp.stack(_split3(r_pad))
    rows = pl.BlockSpec((tm, LANES), lambda i: (i, 0))
    return pl.pallas_call(
        _router_kernel, grid=(n // tm,),
        in_specs=[pl.BlockSpec((tm, D_MODEL), lambda i: (i, 0)), pl.BlockSpec(r3.shape, lambda i: (0, 0, 0))],
        out_specs=[rows, rows, pl.BlockSpec((SUBLANES, LANES), lambda i: (0, 0))],
        out_shape=[jax.ShapeDtypeStruct((n, LANES), I32), jax.ShapeDtypeStruct((n, LANES), F32),
                   jax.ShapeDtypeStruct((SUBLANES, LANES), I32)],
        scratch_shapes=[pltpu.VMEM((1, LANES), F32)],
        compiler_params=_cparams("arbitrary"), name="moe_router")(x, r3)


def _tile_rows(idx):
    return pl.ds(pl.multiple_of(idx * TOKEN_ROWS, TOKEN_ROWS), TOKEN_ROWS)


def _dispatch_kernel(slot_ref, x_ref, init_ref, xs_ref, sem, *, tm):
    del init_ref
    base = pl.program_id(0) * tm

    def start(r, carry):
        src = x_ref.at[_tile_rows(r), :]
        for j in range(2):
            pltpu.make_async_copy(src, xs_ref.at[_tile_rows(slot_ref[2 * (base + r) + j]), :], sem).start()
        return carry

    lax.fori_loop(0, tm, start, 0, unroll=8)
    for _ in range(2):
        pltpu.make_async_copy(x_ref, xs_ref.at[pl.ds(0, tm * TOKEN_ROWS), :], sem).wait()


def _dispatch(slots_flat, xt, n_slots, tm):
    n = xt.shape[0] // TOKEN_ROWS
    any_spec = pl.BlockSpec(memory_space=pl.ANY)
    gs = pltpu.PrefetchScalarGridSpec(
        num_scalar_prefetch=1, grid=(n // tm,),
        in_specs=[pl.BlockSpec((tm * TOKEN_ROWS, LANES), lambda i, slots: (i, 0)), any_spec],
        out_specs=any_spec, scratch_shapes=[pltpu.SemaphoreType.DMA(())])
    return pl.pallas_call(
        functools.partial(_dispatch_kernel, tm=tm), grid_spec=gs,
        out_shape=jax.ShapeDtypeStruct((n_slots * TOKEN_ROWS, LANES), F32), input_output_aliases={2: 0},
        compiler_params=_cparams("arbitrary"), name="moe_dispatch",
    )(slots_flat, xt, jnp.zeros((n_slots * TOKEN_ROWS, LANES), F32))


def _moe_ffn_kernel(be_ref, nv_ref, xs_ref, wg_ref, wu_ref, wd_ref, y_ref, *, tm, tf):
    del be_ref
    live = pl.program_id(0) < nv_ref[0]

    @pl.when(live)
    def _():
        xb = _load_token_tiles(xs_ref, tm).astype(BF16)
        _store_token_tiles(y_ref, _swiglu(xb, wg_ref, wu_ref, wd_ref, tf))

    @pl.when(jnp.logical_not(live))
    def _():
        y_ref[...] = jnp.zeros_like(y_ref)


def _moe_ffn(block_expert, n_valid, xs, wg, wu, wd, tm, tf):
    p = xs.shape[0] // TOKEN_ROWS
    tiles = pl.BlockSpec((tm * TOKEN_ROWS, LANES), lambda i, be, nv: (i, 0))

    def expert(shape):
        return pl.BlockSpec((None,) + shape[1:], lambda i, be, nv: (be[i], 0, 0), pipeline_mode=pl.Buffered(1))

    gs = pltpu.PrefetchScalarGridSpec(
        num_scalar_prefetch=2, grid=(p // tm,),
        in_specs=[tiles, expert(wg.shape), expert(wu.shape), expert(wd.shape)], out_specs=tiles)
    return pl.pallas_call(
        functools.partial(_moe_ffn_kernel, tm=tm, tf=tf), grid_spec=gs, out_shape=jax.ShapeDtypeStruct(xs.shape, F32),
        compiler_params=_cparams("arbitrary"), name="moe_ffn")(block_expert, n_valid, xs, wg, wu, wd)


def _combine_kernel(slot_ref, x_ref, gate_ref, g_ref, b_ref, y_ref, o_ref, ybuf, sem, *, alpha, tm):
    i = pl.program_id(0)

    def gather(step, par):
        base = step * tm

        def start(r, carry):
            for j in range(2):
                src = y_ref.at[_tile_rows(slot_ref[2 * (base + r) + j]), :]
                pltpu.make_async_copy(src, ybuf.at[par, j, _tile_rows(r), :], sem.at[par]).start()
            return carry

        lax.fori_loop(0, tm, start, 0, unroll=8)

    @pl.when(i == 0)
    def _():
        gather(0, 0)

    @pl.when(i + 1 < pl.num_programs(0))
    def _():
        gather(i + 1, (i + 1) & 1)

    par = i & 1
    for j in range(2):
        pltpu.make_async_copy(y_ref.at[pl.ds(0, tm * TOKEN_ROWS), :], ybuf.at[par, j], sem.at[par]).wait()
    gt = gate_ref[...]
    ff = gt[:, 0:1] * _load_token_tiles(ybuf.at[par, 0], tm) + gt[:, 1:2] * _load_token_tiles(ybuf.at[par, 1], tm)
    o_ref[...] = _layer_norm(alpha * x_ref[...] + ff, g_ref[...], b_ref[...])


def _combine(slots_flat, x, gates, g, b, y, alpha, tm):
    n = x.shape[0]
    rows = pl.BlockSpec((tm, D_MODEL), lambda i, slots: (i, 0))
    vec = pl.BlockSpec((1, D_MODEL), lambda i, slots: (0, 0))
    gs = pltpu.PrefetchScalarGridSpec(
        num_scalar_prefetch=1, grid=(n // tm,),
        in_specs=[rows, pl.BlockSpec((tm, LANES), lambda i, slots: (i, 0)), vec, vec,
                  pl.BlockSpec(memory_space=pl.ANY)],
        out_specs=rows, scratch_shapes=[pltpu.VMEM((2, 2, tm * TOKEN_ROWS, LANES), F32),
                                        pltpu.SemaphoreType.DMA((2,))])
    return pl.pallas_call(
        functools.partial(_combine_kernel, alpha=alpha, tm=tm), grid_spec=gs,
        out_shape=jax.ShapeDtypeStruct((n, D_MODEL), F32), compiler_params=_cparams("arbitrary"),
        name="moe_combine")(slots_flat, x, gates, g, b, y)


def _moe_layer(x, xt, router, wg, wu, wd, g, b, alpha, tiles):
    n = x.shape[0]
    tmb = tiles["moe_rows"]
    idx, gates, counts = _route(x, router, tiles["route_rows"])
    counts = counts[0, :N_EXPERTS]
    padded = ((counts + tmb - 1) // tmb) * tmb
    pad_end = jnp.cumsum(padded)
    pad_start = pad_end - padded
    slots = jnp.stack([pad_start[idx[:, 0]] + idx[:, 2], pad_start[idx[:, 1]] + idx[:, 3]], axis=1)
    slots_flat = slots.reshape(-1).astype(I32)
    n_slots = 2 * n + N_EXPERTS * tmb
    n_blocks = n_slots // tmb
    block_expert = jnp.clip(jnp.searchsorted(pad_end, jnp.arange(n_blocks, dtype=I32) * tmb, side="right"),
                            0, N_EXPERTS - 1).astype(I32)
    n_valid = (pad_end[-1:] // tmb).astype(I32)
    xs = _dispatch(slots_flat, xt, n_slots, tiles["dma_rows"])
    y = _moe_ffn(block_expert, n_valid, xs, wg, wu, wd, tmb, tiles["ffn_cols"])
    return _combine(slots_flat, x, gates, g, b, y, alpha, tiles["dma_rows"])


def _tiles(n, t):
    return dict(
        proj_rows=min(256, t), prefix_rows=min(256, t), attn_q=min(256, t), row_tile=min(512, n),
        ffn_cols=512, route_rows=min(256, n), moe_rows=min(512, n), dma_rows=min(256, n))


def kernel(x, positions, w_in, b_forget, w_branch_a, w_branch_b, w_out, ln_mix_g, ln_mix_b, ln_ffn_g, ln_ffn_b,
           ffn_w_gate, ffn_w_up, ffn_w_down, moe_router, moe_w_gate, moe_w_up, moe_w_down):
    B, T, D = x.shape
    assert D == D_MODEL and T % CHUNK == 0
    n = B * T
    depth = w_in.shape[0]
    top_k = min(TOPK_MAX, T // 4)
    alpha = (2.0 * depth) ** 0.25
    tiles = _tiles(n, T)
    cos_t, sin_t = _rope_tables(positions)
    xf = x.reshape(n, D)
    for layer in range(depth):
        weights = _prep_mix_weights(w_in[layer], b_forget[layer])
        (dq, dk, iq, fq, fk, ga, gb, ik2, dvt, fvt, iwt, lf) = _in_projection(
            xf.reshape(B, T, D), cos_t, sin_t, weights, tiles["proj_rows"])
        aux = _forget_aux(lf, tiles["prefix_rows"])
        o_a = _dsa_attention(iq, ik2, iwt, dq, dk, dvt, tiles["attn_q"], top_k)
        o_b = _fox_attention(fq, fk, aux, fvt, tiles["attn_q"])
        dense = layer % 2 == 0
        x1, x1_alt = _merge_project(
            o_a.reshape(n, HEADS_W), o_b.reshape(n, HEADS_W), ga.reshape(n, D), gb.reshape(n, D), xf,
            w_branch_a[layer].astype(BF16), w_branch_b[layer].astype(BF16), w_out[layer].astype(BF16),
            ln_mix_g[layer][None, :], ln_mix_b[layer][None, :], alpha, tiles["row_tile"], token_tiles=not dense)
        j = layer // 2
        g, b = ln_ffn_g[layer][None, :], ln_ffn_b[layer][None, :]
        if dense:
            xf = _dense_ffn(x1_alt, x1, ffn_w_gate[j].astype(BF16), ffn_w_up[j].astype(BF16),
                            ffn_w_down[j].astype(BF16), g, b, alpha, tiles["row_tile"], tiles["ffn_cols"])
        else:
            xf = _moe_layer(x1, x1_alt, moe_router[j], moe_w_gate[j].astype(BF16), moe_w_up[j].astype(BF16),
                            moe_w_down[j].astype(BF16), g, b, alpha, tiles)
    return xf.reshape(B, T, D)
```

```python
import functools

import jax
import jax.numpy as jnp
from jax import lax
from jax.experimental import pallas as pl
from jax.experimental.pallas import tpu as pltpu

F32, BF16, I32, I16 = jnp.float32, jnp.bfloat16, jnp.int32, jnp.int16

D_MODEL = 1024
HEAD_DIM = 64
N_HEADS = 8
HEADS_W = N_HEADS * HEAD_DIM
N_PAIRS = N_HEADS // 2
CHUNK = 64
TOPK_MAX = 256
ROPE_THETA = 10000.0
D_FF = 3584
N_EXPERTS = 8
LN_EPS = 1e-5
DEPTH = 2
LANES = 128
SUBLANES = 8
PACKED_ROWS = 2 * SUBLANES
INT_MIN = -(2**31)
I16_MIN = -(2**15)
NEG = -1e30
VMEM_LIMIT = 56 * 1024 * 1024
NT_DIMS = (((1,), (1,)), ((), ()))


def _cparams(*sem):
    return pltpu.CompilerParams(dimension_semantics=sem, vmem_limit_bytes=VMEM_LIMIT)


def _split3(v):
    hi = v.astype(BF16)
    r1 = v - hi.astype(F32)
    mid = r1.astype(BF16)
    lo = (r1 - mid.astype(F32)).astype(BF16)
    return hi, mid, lo


def _layer_norm(y, g, b):
    mu = jnp.mean(y, axis=-1, keepdims=True)
    yc = y - mu
    var = jnp.mean(yc * yc, axis=-1, keepdims=True)
    return yc * lax.rsqrt(var + LN_EPS) * g + b


def _rope_kernel(pos_ref, freq_ref, sign_ref, cos_ref, sin_ref):
    ang = pos_ref[...].astype(F32) * freq_ref[...]
    cos_ref[...] = jnp.cos(ang)
    sin_ref[...] = jnp.sin(ang) * sign_ref[...]


def _rope_tables(positions):
    B, T = positions.shape
    n = B * T
    inv_freq = ROPE_THETA ** (-jnp.arange(0, HEAD_DIM, 2, dtype=F32) / HEAD_DIM)
    freq = jnp.tile(inv_freq, LANES // (HEAD_DIM // 2))[None, :]
    half = HEAD_DIM // 2
    sign = jnp.tile(jnp.concatenate([-jnp.ones((half,), F32), jnp.ones((half,), F32)]), LANES // HEAD_DIM)[None, :]
    pos_b = jnp.broadcast_to(positions.reshape(n, 1), (n, LANES))
    tm = min(n, 1024)
    row = pl.BlockSpec((tm, LANES), lambda i: (i, 0))
    const = pl.BlockSpec((1, LANES), lambda i: (0, 0))
    cos, sin = pl.pallas_call(
        _rope_kernel, grid=(n // tm,), in_specs=[row, const, const], out_specs=[row, row],
        out_shape=[jax.ShapeDtypeStruct((n, LANES), F32)] * 2, compiler_params=_cparams("parallel"),
        name="rope_tables")(pos_b, freq, sign)
    return cos.reshape(B, T, LANES), sin.reshape(B, T, LANES)


LOG2E = 1.4426950408889634
_Q_SCALE = HEAD_DIM ** -0.5 * LOG2E
_IW_SCALE = (N_HEADS ** -0.5) * (HEAD_DIM ** -0.5)
_OFF_DQ, _OFF_DK, _OFF_IQ, _OFF_FQ, _OFF_FK, _OFF_GA, _OFF_GB = 0, 512, 1024, 1536, 2048, 2560, 3584
_W_MAIN = 4608
_AUX_USED = 6


def _inproj_kernel(x_ref, cos_ref, sin_ref, wm_ref, wik_ref, wvt_ref, wit_ref, wfl_ref, bfl_ref,
                   dq_ref, dk_ref, iq_ref, fq_ref, fk_ref, ga_ref, gb_ref, ik2_ref, dvt_ref, fvt_ref,
                   iwt_ref, lf_ref):
    xb = x_ref[...].astype(BF16)
    cos = cos_ref[...]
    sin = sin_ref[...]
    lane = lax.broadcasted_iota(I32, cos.shape, 1)
    first_half = (lane & (HEAD_DIM - 1)) < HEAD_DIM // 2

    def rot(y):
        partner = jnp.where(first_half, pltpu.roll(y, LANES - HEAD_DIM // 2, 1), pltpu.roll(y, HEAD_DIM // 2, 1))
        return y * cos + partner * sin

    def proj(off, width):
        return jnp.dot(xb, wm_ref[:, off:off + width], preferred_element_type=F32)

    def rot_group(off, scale, out_ref):
        y = proj(off, HEADS_W)
        for j in range(N_PAIRS):
            r = rot(y[:, LANES * j:LANES * (j + 1)])
            if scale != 1.0:
                r = r * scale
            out_ref[:, LANES * j:LANES * (j + 1)] = r.astype(BF16)

    rot_group(_OFF_DQ, _Q_SCALE, dq_ref)
    rot_group(_OFF_DK, 1.0, dk_ref)
    rot_group(_OFF_IQ, 1.0, iq_ref)
    fq_ref[...] = (proj(_OFF_FQ, HEADS_W) * _Q_SCALE).astype(BF16)
    fk_ref[...] = proj(_OFF_FK, HEADS_W).astype(BF16)
    ga_ref[...] = jax.nn.sigmoid(proj(_OFF_GA, D_MODEL)).astype(BF16)
    gb_ref[...] = jax.nn.sigmoid(proj(_OFF_GB, D_MODEL)).astype(BF16)
    ik2_ref[...] = rot(jnp.dot(xb, wik_ref[...], preferred_element_type=F32)).astype(BF16)
    vt = lax.dot_general(wvt_ref[...], xb, NT_DIMS, preferred_element_type=F32)
    dvt_ref[...] = vt[:HEADS_W].astype(BF16)
    fvt_ref[...] = vt[HEADS_W:].astype(BF16)
    it = lax.dot_general(wit_ref[...], xb, NT_DIMS, preferred_element_type=F32)
    iwt_ref[...] = it[:N_HEADS] * _IW_SCALE
    z = jnp.dot(xb, wfl_ref[...], preferred_element_type=F32) + bfl_ref[...]
    lf = jnp.minimum(z, 0.0) - jnp.log1p(jnp.exp(-jnp.abs(z)))
    lane_w = lax.broadcasted_iota(I32, lf.shape, 1) & (LANES - 1)
    lf_ref[...] = jnp.where(lane_w < _AUX_USED, lf, 0.0)


def _prep_mix_weights(w_in, b_forget):
    sizes = (HEADS_W, HEADS_W, HEADS_W, HEADS_W, HEAD_DIM, N_HEADS, HEADS_W, HEADS_W, HEADS_W, N_HEADS, D_MODEL, D_MODEL)
    offs = [0]
    for s in sizes:
        offs.append(offs[-1] + s)
    dq, dk, dv, iq, ik, iw, fq, fk, fv, fl, ga, gb = [w_in[:, offs[i]:offs[i + 1]] for i in range(12)]
    w_main = jnp.concatenate([dq, dk, iq, fq, fk, ga, gb], axis=1).astype(BF16)
    w_ik2 = jnp.concatenate([ik, ik], axis=1).astype(BF16)
    w_vt = jnp.concatenate([dv, fv], axis=1).T.astype(BF16)
    w_it = jnp.concatenate([iw.T, jnp.zeros((16 - N_HEADS, D_MODEL), F32)], axis=0).astype(BF16)
    src = []
    for j in range(N_PAIRS):
        src += [2 * j] * 3 + [2 * j + 1] * 3 + [-1] * (LANES - _AUX_USED)
    src = jnp.array(src, I32)
    used = src >= 0
    w_fl = jnp.where(used[None, :], fl[:, jnp.maximum(src, 0)], 0.0).astype(BF16)
    b_fl = jnp.where(used, b_forget[jnp.maximum(src, 0)], 0.0)[None, :].astype(F32)
    return w_main, w_ik2, w_vt, w_it, w_fl, b_fl


def _in_projection(x, cos_t, sin_t, weights, tm):
    B, T, _ = x.shape
    w_main, w_ik2, w_vt, w_it, w_fl, b_fl = weights
    nt = T // tm

    def rows(width):
        return pl.BlockSpec((None, tm, width), lambda b, i: (b, i, 0))

    def cols(height):
        return pl.BlockSpec((None, height, tm), lambda b, i: (b, 0, i))

    def const(shape):
        return pl.BlockSpec(shape, lambda b, i: (0, 0), pipeline_mode=pl.Buffered(1))

    bt = lambda w, dt: jax.ShapeDtypeStruct((B, T, w), dt)
    tb = lambda h, dt: jax.ShapeDtypeStruct((B, h, T), dt)
    return pl.pallas_call(
        _inproj_kernel, grid=(B, nt),
        in_specs=[rows(D_MODEL), rows(LANES), rows(LANES), const(w_main.shape), const(w_ik2.shape),
                  const(w_vt.shape), const(w_it.shape), const(w_fl.shape), const(b_fl.shape)],
        out_specs=[rows(HEADS_W)] * 5 + [rows(D_MODEL)] * 2 + [rows(LANES), cols(HEADS_W), cols(HEADS_W),
                                                                cols(N_HEADS), rows(HEADS_W)],
        out_shape=[bt(HEADS_W, BF16)] * 5 + [bt(D_MODEL, BF16)] * 2 + [bt(LANES, BF16), tb(HEADS_W, BF16),
                                                                      tb(HEADS_W, BF16), tb(N_HEADS, F32),
                                                                      bt(HEADS_W, F32)],
        compiler_params=_cparams("parallel", "parallel"), name="in_projection",
    )(x, cos_t, sin_t, w_main, w_ik2, w_vt, w_it, w_fl, b_fl)


def _cumsum_kernel(lf_ref, aux_ref, carry_ref):
    @pl.when(pl.program_id(1) == 0)
    def _():
        carry_ref[...] = jnp.zeros_like(carry_ref)

    lf = lf_ref[...]
    tb = lf.shape[0]
    r = lax.broadcasted_iota(I32, (tb, tb), 0)
    c = lax.broadcasted_iota(I32, (tb, tb), 1)
    tri = jnp.where(c <= r, 1.0, 0.0).astype(BF16)
    hi, mid, lo = _split3(lf)
    cs = (jnp.dot(tri, lo, preferred_element_type=F32) + jnp.dot(tri, mid, preferred_element_type=F32)
          + jnp.dot(tri, hi, preferred_element_type=F32) + carry_ref[...])
    carry_ref[...] = cs[tb - 1:tb, :]
    nh, nm, nl = _split3(-LOG2E * cs)
    lane_w = lax.broadcasted_iota(I32, cs.shape, 1) & (LANES - 1)
    k = jnp.where(lane_w >= 3, lane_w - 3, lane_w)
    piece = jnp.where(k == 0, nh.astype(F32), jnp.where(k == 1, nm.astype(F32), nl.astype(F32)))
    aux_ref[...] = jnp.where(lane_w < _AUX_USED, piece, 0.0).astype(BF16)


def _forget_aux(lf, tb):
    B, T, W = lf.shape
    spec = pl.BlockSpec((None, tb, W), lambda b, i: (b, i, 0))
    return pl.pallas_call(
        _cumsum_kernel, grid=(B, T // tb), in_specs=[spec], out_specs=spec,
        out_shape=jax.ShapeDtypeStruct((B, T, W), BF16), scratch_shapes=[pltpu.VMEM((1, W), F32)],
        compiler_params=_cparams("parallel", "arbitrary"), name="forget_prefix")(lf)


def _stack_pair(qb):
    qf = qb.astype(F32)
    lane = lax.broadcasted_iota(I32, qf.shape, 1)
    lo = lane < HEAD_DIM
    return jnp.concatenate([jnp.where(lo, qf, 0.0), jnp.where(lo, 0.0, qf)], axis=0)


def _softmax_step(s, vb, j, m_sc, l_sc, acc_sc):
    m_old = m_sc[j]
    m_new = jnp.maximum(m_old, jnp.max(s, axis=0, keepdims=True))
    alpha = jnp.exp2(m_old - m_new)
    p = jnp.exp2(s - m_new)
    l_sc[j] = alpha * l_sc[j] + jnp.sum(p, axis=0, keepdims=True)
    acc_sc[j] = alpha * acc_sc[j] + jnp.dot(vb, p.astype(BF16), preferred_element_type=F32)
    m_sc[j] = m_new


def _init_softmax(m_sc, l_sc, acc_sc):
    m_sc[...] = jnp.full(m_sc.shape, -jnp.inf, F32)
    l_sc[...] = jnp.zeros(l_sc.shape, F32)
    acc_sc[...] = jnp.zeros(acc_sc.shape, F32)


def _finish_pair(o_ref, j, tq, l_sc, acc_sc):
    out_t = acc_sc[j] / l_sc[j]
    blk_t = jnp.concatenate([out_t[:HEAD_DIM, :tq], out_t[HEAD_DIM:, tq:]], axis=0)
    o_ref[:, LANES * j:LANES * (j + 1)] = blk_t.T.astype(BF16)


def _softmax_scratch(tq, q_width):
    return [pltpu.VMEM((N_PAIRS, q_width, 2 * tq), BF16), pltpu.VMEM((2, N_PAIRS, tq, 2 * tq), F32),
            pltpu.VMEM((N_PAIRS, 1, 2 * tq), F32), pltpu.VMEM((N_PAIRS, 1, 2 * tq), F32),
            pltpu.VMEM((N_PAIRS, LANES, 2 * tq), F32)]


_STEPS_PER_ITER = 2


def _unrolled_loop(n, step, carry, unroll=_STEPS_PER_ITER):
    def main(h, c):
        for u in range(unroll):
            c = step(unroll * h + u, c)
        return c

    n_main = n // unroll
    carry = lax.fori_loop(0, n_main, main, carry)
    return lax.fori_loop(n_main * unroll, n, step, carry)


def _pipelined_attention(n_steps, scores, update):
    scores(0, 0)
    n_mid = n_steps - 1

    def body(h, carry):
        kv = _STEPS_PER_ITER * h
        for u in range(_STEPS_PER_ITER):
            scores(kv + u + 1, (u + 1) & 1)
            update(kv + u, u & 1, False)
        return carry

    n_main = n_mid // _STEPS_PER_ITER
    lax.fori_loop(0, n_main, body, 0)

    def tail(kv, carry):
        scores(kv + 1, (kv + 1) & 1)
        update(kv, kv & 1, False)
        return carry

    lax.fori_loop(n_main * _STEPS_PER_ITER, n_mid, tail, 0)
    update(n_steps - 1, (n_steps - 1) & 1, True)


def _fox_kernel(q_ref, k_ref, aux_ref, vt_ref, o_ref, qt_sc, s_sc, m_sc, l_sc, acc_sc, *, tq):
    qi = pl.program_id(1)
    q0 = qi * tq
    row2 = lax.broadcasted_iota(I32, (2 * tq, LANES), 0)
    lane2 = lax.broadcasted_iota(I32, (2 * tq, LANES), 1)
    sel = jnp.where(row2 < tq, jnp.where(lane2 < 3, 1.0, 0.0),
                    jnp.where(lane2 < 3, 0.0, jnp.where(lane2 < _AUX_USED, 1.0, 0.0)))
    for j in range(N_PAIRS):
        qa = jnp.concatenate([_stack_pair(q_ref[:, LANES * j:LANES * (j + 1)]), sel], axis=1)
        qt_sc[j] = qa.T.astype(BF16)
    _init_softmax(m_sc, l_sc, acc_sc)

    def scores(kv, slot):
        ks = pl.multiple_of(kv * tq, tq)
        for j in range(N_PAIRS):
            cs = slice(LANES * j, LANES * (j + 1))
            kb = jnp.concatenate([k_ref[pl.ds(ks, tq), cs], aux_ref[pl.ds(ks, tq), cs]], axis=1)
            s_sc[slot, j] = jnp.dot(kb, qt_sc[j], preferred_element_type=F32)

    def update(kv, slot, last):
        ks = pl.multiple_of(kv * tq, tq)
        if last:
            krow = ks + lax.broadcasted_iota(I32, (tq, 2 * tq), 0)
            qcol = q0 + (lax.broadcasted_iota(I32, (tq, 2 * tq), 1) & (tq - 1))
            keep = krow <= qcol
        for j in range(N_PAIRS):
            s = s_sc[slot, j]
            if last:
                s = jnp.where(keep, s, NEG)
            _softmax_step(s, vt_ref[LANES * j:LANES * (j + 1), pl.ds(ks, tq)], j, m_sc, l_sc, acc_sc)

    _pipelined_attention(qi + 1, scores, update)
    for j in range(N_PAIRS):
        _finish_pair(o_ref, j, tq, l_sc, acc_sc)


def _fox_attention(fq, fk, aux, fvt, tq):
    B, T, W = fq.shape
    qspec = pl.BlockSpec((None, tq, W), lambda b, i: (b, i, 0))
    full = pl.BlockSpec((None, T, W), lambda b, i: (b, 0, 0))
    fullt = pl.BlockSpec((None, W, T), lambda b, i: (b, 0, 0))
    return pl.pallas_call(
        functools.partial(_fox_kernel, tq=tq), grid=(B, T // tq),
        in_specs=[qspec, full, full, fullt], out_specs=qspec,
        out_shape=jax.ShapeDtypeStruct((B, T, W), BF16), scratch_shapes=_softmax_scratch(tq, 2 * LANES),
        compiler_params=_cparams("parallel", "arbitrary"), name="fox_attention")(fq, fk, aux, fvt)


_COUNT_CHAINS = 4


def _dsa_kernel(iq_ref, ik2_ref, iwt_ref, q_ref, k_ref, vt_ref, o_ref, key_sc, hi_sc, lo_sc, bias_sc, qt_sc, s_sc,
                m_sc, l_sc, acc_sc, *, tq, top_k):
    qi = pl.program_id(1)
    q0 = qi * tq
    col = lax.broadcasted_iota(I32, (1, tq), 1)
    limit = (lax.shift_right_logical(q0 + col, CHUNK.bit_length() - 1) + 1) * CHUNK
    iw = iwt_ref[...]

    for j in range(N_PAIRS):
        qt_sc[j] = _stack_pair(iq_ref[:, LANES * j:LANES * (j + 1)]).T.astype(BF16)

    def score_tile(kv, masked):
        ks = pl.multiple_of(kv * tq, tq)
        kb = ik2_ref[pl.ds(ks, tq), :]
        sc = jnp.zeros((tq, tq), F32)
        for j in range(N_PAIRS):
            s2 = jnp.dot(kb, qt_sc[j], preferred_element_type=F32)
            sc = sc + jnp.maximum(s2[:, :tq], 0.0) * iw[2 * j:2 * j + 1, :]
            sc = sc + jnp.maximum(s2[:, tq:], 0.0) * iw[2 * j + 1:2 * j + 2, :]
        bits = pltpu.bitcast(sc, I32)
        key = bits ^ ((bits >> 31) & jnp.int32(0x7FFFFFFF))
        key = jnp.where(bits == jnp.int32(INT_MIN), 0, key)
        if masked:
            krow = ks + lax.broadcasted_iota(I32, key.shape, 0)
            key = jnp.where(krow < limit, key, jnp.int32(INT_MIN))
        key_sc[pl.ds(ks, tq), :] = key
        hi_sc[pl.ds(ks, tq), :] = (key >> 16).astype(I16)
        lo_sc[pl.ds(ks, tq), :] = ((key & 0xFFFF) + I16_MIN).astype(I16)

    def score_body(kv, carry):
        score_tile(kv, False)
        return carry

    _unrolled_loop(qi, score_body, 0)
    score_tile(qi, True)

    def count(plane_sc, pred):
        def body(g, accs):
            slab = plane_sc[pl.ds(pl.multiple_of(g * tq, tq), tq), :]
            accs = list(accs)
            for r in range(tq // PACKED_ROWS):
                blk = slab[PACKED_ROWS * r:PACKED_ROWS * (r + 1), :]
                hit = jnp.where(pred(blk), jnp.int16(1), jnp.int16(0))
                accs[r % _COUNT_CHAINS] = accs[r % _COUNT_CHAINS] + hit
            return tuple(accs)

        zero = jnp.zeros((PACKED_ROWS, tq), I16)
        accs = lax.fori_loop(0, qi + 1, body, (zero,) * _COUNT_CHAINS)
        return jnp.sum(sum(a.astype(I32) for a in accs), axis=0, keepdims=True)

    def rows16(v):
        return jnp.broadcast_to(v.astype(I16), (PACKED_ROWS, tq))

    def search(plane_sc, above):
        def body(i, v):
            cand = v + lax.shift_left(jnp.int32(1), 15 - i)
            cand16 = rows16(cand)
            cnt = above + count(plane_sc, lambda blk: blk >= cand16)
            return jnp.where(cnt >= top_k, cand, v)

        return lax.fori_loop(0, 16, body, jnp.full((1, tq), I16_MIN, I32))

    t_hi = search(hi_sc, 0)
    t_hi16 = rows16(t_hi)
    above_hi = count(hi_sc, lambda blk: blk > t_hi16)

    def mask_low(g, carry):
        rows = pl.ds(pl.multiple_of(g * tq, tq), tq)
        lo_sc[rows, :] = jnp.where(hi_sc[rows, :] == t_hi.astype(I16), lo_sc[rows, :], jnp.int16(I16_MIN))
        return carry

    lax.fori_loop(0, qi + 1, mask_low, 0)
    t_lo = search(lo_sc, above_hi)
    t_lo16 = rows16(t_lo)
    thr = t_hi * 65536 + (t_lo - I16_MIN)
    need = (top_k - above_hi - count(lo_sc, lambda blk: blk > t_lo16)).astype(F32)

    r = lax.broadcasted_iota(I32, (tq, tq), 0)
    c = lax.broadcasted_iota(I32, (tq, tq), 1)
    tri = jnp.where(c < r, 1.0, 0.0).astype(BF16)

    def bias_tile(kv, seen, masked):
        ks = pl.multiple_of(kv * tq, tq)
        key = key_sc[pl.ds(ks, tq), :]
        eqf = jnp.where(key == thr, 1.0, 0.0)
        before = jnp.dot(tri, eqf.astype(BF16), preferred_element_type=F32) + seen
        b = jnp.where(key > thr, 0.0, jnp.where(key == thr, jnp.where(before < need, 0.0, NEG), NEG))
        if masked:
            krow = ks + lax.broadcasted_iota(I32, key.shape, 0)
            b = jnp.where(krow < limit, b, NEG)
        bias_sc[pl.ds(ks, tq), :] = b
        return seen + jnp.sum(eqf, axis=0, keepdims=True)

    seen = _unrolled_loop(qi, lambda kv, s: bias_tile(kv, s, False), jnp.zeros((1, tq), F32))
    bias_tile(qi, seen, True)

    for j in range(N_PAIRS):
        qt_sc[j] = _stack_pair(q_ref[:, LANES * j:LANES * (j + 1)]).T.astype(BF16)
    _init_softmax(m_sc, l_sc, acc_sc)

    def scores(kv, slot):
        ks = pl.multiple_of(kv * tq, tq)
        for j in range(N_PAIRS):
            s_sc[slot, j] = jnp.dot(k_ref[pl.ds(ks, tq), LANES * j:LANES * (j + 1)], qt_sc[j],
                                    preferred_element_type=F32)

    def update(kv, slot, last):
        del last
        ks = pl.multiple_of(kv * tq, tq)
        b = bias_sc[pl.ds(ks, tq), :]
        b2 = jnp.concatenate([b, b], axis=1)
        for j in range(N_PAIRS):
            _softmax_step(s_sc[slot, j] + b2, vt_ref[LANES * j:LANES * (j + 1), pl.ds(ks, tq)], j, m_sc, l_sc, acc_sc)

    _pipelined_attention(qi + 1, scores, update)
    for j in range(N_PAIRS):
        _finish_pair(o_ref, j, tq, l_sc, acc_sc)


def _dsa_attention(iq, ik2, iwt, dq, dk, dvt, tq, top_k):
    B, T, W = dq.shape
    qspec = pl.BlockSpec((None, tq, W), lambda b, i: (b, i, 0))
    full = pl.BlockSpec((None, T, W), lambda b, i: (b, 0, 0))
    fullt = pl.BlockSpec((None, W, T), lambda b, i: (b, 0, 0))
    return pl.pallas_call(
        functools.partial(_dsa_kernel, tq=tq, top_k=top_k), grid=(B, T // tq),
        in_specs=[qspec, pl.BlockSpec((None, T, LANES), lambda b, i: (b, 0, 0)),
                  pl.BlockSpec((None, N_HEADS, tq), lambda b, i: (b, 0, i)), qspec, full, fullt],
        out_specs=qspec, out_shape=jax.ShapeDtypeStruct((B, T, W), BF16),
        scratch_shapes=[pltpu.VMEM((T, tq), I32), pltpu.VMEM((T, tq), I16), pltpu.VMEM((T, tq), I16),
                        pltpu.VMEM((T, tq), F32)] + _softmax_scratch(tq, LANES),
        compiler_params=_cparams("parallel", "arbitrary"), name="dsa_attention")(iq, ik2, iwt, dq, dk, dvt)


TOKEN_ROWS = D_MODEL // LANES
assert TOKEN_ROWS == SUBLANES


def _store_token_tiles(ref, y):
    tm = y.shape[0]
    for s in range(TOKEN_ROWS):
        ref[pl.ds(s, tm, stride=TOKEN_ROWS), :] = y[:, LANES * s:LANES * (s + 1)]


def _load_token_tiles(ref, tm):
    return jnp.concatenate([ref[pl.ds(s, tm, stride=TOKEN_ROWS), :] for s in range(TOKEN_ROWS)], axis=1)


def _merge_kernel(oa_ref, ob_ref, ga_ref, gb_ref, x_ref, wa_ref, wb_ref, wo_ref, g_ref, b_ref, o_ref, o2_ref,
                  *, alpha, token_tiles):
    a = jnp.dot(oa_ref[...], wa_ref[...], preferred_element_type=F32)
    b = jnp.dot(ob_ref[...], wb_ref[...], preferred_element_type=F32)
    merged = ga_ref[...].astype(F32) * a + gb_ref[...].astype(F32) * b
    mix = jnp.dot(merged.astype(BF16), wo_ref[...], preferred_element_type=F32)
    y = _layer_norm(alpha * x_ref[...] + mix, g_ref[...], b_ref[...])
    o_ref[...] = y
    if token_tiles:
        _store_token_tiles(o2_ref, y)
    else:
        o2_ref[...] = y.astype(BF16)


def _merge_project(oa, ob, ga, gb, x, wa, wb, wo, g, b, alpha, tm, token_tiles):
    n = x.shape[0]

    def rows(w):
        return pl.BlockSpec((tm, w), lambda i: (i, 0))

    def const(shape):
        return pl.BlockSpec(shape, lambda i: (0, 0), pipeline_mode=pl.Buffered(1))

    if token_tiles:
        out2_spec = pl.BlockSpec((tm * TOKEN_ROWS, LANES), lambda i: (i, 0))
        out2_shape = jax.ShapeDtypeStruct((n * TOKEN_ROWS, LANES), F32)
    else:
        out2_spec, out2_shape = rows(D_MODEL), jax.ShapeDtypeStruct((n, D_MODEL), BF16)
    return pl.pallas_call(
        functools.partial(_merge_kernel, alpha=alpha, token_tiles=token_tiles), grid=(n // tm,),
        in_specs=[rows(HEADS_W), rows(HEADS_W), rows(D_MODEL), rows(D_MODEL), rows(D_MODEL), const(wa.shape),
                  const(wb.shape), const(wo.shape), const(g.shape), const(b.shape)],
        out_specs=[rows(D_MODEL), out2_spec], out_shape=[jax.ShapeDtypeStruct((n, D_MODEL), F32), out2_shape],
        compiler_params=_cparams("parallel"), name="merge_project")(oa, ob, ga, gb, x, wa, wb, wo, g, b)


def _swiglu_partial(xb, wg, wu, wd):
    h = jnp.dot(xb, wg, preferred_element_type=F32)
    u = jnp.dot(xb, wu, preferred_element_type=F32)
    hid = (h * jax.nn.sigmoid(h)) * u
    return jnp.dot(hid.astype(BF16), wd, preferred_element_type=F32)


def _swiglu(xb, wg_ref, wu_ref, wd_ref, tf):
    acc = None
    for f in range(D_FF // tf):
        part = _swiglu_partial(xb, wg_ref[:, f * tf:(f + 1) * tf], wu_ref[:, f * tf:(f + 1) * tf],
                               wd_ref[f * tf:(f + 1) * tf, :])
        acc = part if acc is None else acc + part
    return acc


def _ffn_kernel(xb_ref, x_ref, wg_ref, wu_ref, wd_ref, g_ref, b_ref, o_ref, *, alpha, tf):
    ff = _swiglu(xb_ref[...], wg_ref, wu_ref, wd_ref, tf)
    o_ref[...] = _layer_norm(alpha * x_ref[...] + ff, g_ref[...], b_ref[...])


def _dense_ffn(xb, x, wg, wu, wd, g, b, alpha, tm, tf):
    n = x.shape[0]
    rows = pl.BlockSpec((tm, D_MODEL), lambda i: (i, 0))

    def const(shape):
        return pl.BlockSpec(shape, lambda i: (0, 0), pipeline_mode=pl.Buffered(1))

    return pl.pallas_call(
        functools.partial(_ffn_kernel, alpha=alpha, tf=tf), grid=(n // tm,),
        in_specs=[rows, rows, const(wg.shape), const(wu.shape), const(wd.shape), const(g.shape), const(b.shape)],
        out_specs=rows, out_shape=jax.ShapeDtypeStruct((n, D_MODEL), F32),
        compiler_params=_cparams("parallel"), name="dense_ffn")(xb, x, wg, wu, wd, g, b)


def _router_kernel(x_ref, r_ref, idx_ref, gate_ref, cnt_ref, seen_sc):
    i = pl.program_id(0)

    @pl.when(i == 0)
    def _():
        seen_sc[...] = jnp.zeros_like(seen_sc)

    xs = _split3(x_ref[...])
    logits = jnp.zeros((x_ref.shape[0], LANES), F32)
    for a, b in ((2, 0), (1, 1), (0, 2), (1, 0), (0, 1), (0, 0)):
        logits = logits + jnp.dot(xs[a], r_ref[b], preferred_element_type=F32)
    lane = lax.broadcasted_iota(I32, logits.shape, 1)
    lg = jnp.where(lane < N_EXPERTS, logits, -jnp.inf)
    v1 = jnp.max(lg, axis=-1, keepdims=True)
    i1 = jnp.min(jnp.where(lg == v1, lane, LANES), axis=-1, keepdims=True)
    lg2 = jnp.where(lane == i1, -jnp.inf, lg)
    v2 = jnp.max(lg2, axis=-1, keepdims=True)
    i2 = jnp.min(jnp.where(lg2 == v2, lane, LANES), axis=-1, keepdims=True)
    e = jnp.exp(v2 - v1)
    g1 = 1.0 / (1.0 + e)
    g2 = e / (1.0 + e)
    hit1 = lane == i1
    hit2 = lane == i2
    cnt = jnp.where(hit1, 1.0, jnp.where(hit2, 1.0, 0.0))
    tm = cnt.shape[0]
    r = lax.broadcasted_iota(I32, (tm, tm), 0)
    c = lax.broadcasted_iota(I32, (tm, tm), 1)
    tri = jnp.where(c < r, 1.0, 0.0).astype(BF16)
    before = jnp.dot(tri, cnt.astype(BF16), preferred_element_type=F32) + seen_sc[...]
    rank1 = jnp.sum(jnp.where(hit1, before, 0.0), axis=-1, keepdims=True).astype(I32)
    rank2 = jnp.sum(jnp.where(hit2, before, 0.0), axis=-1, keepdims=True).astype(I32)
    seen = seen_sc[...] + jnp.sum(cnt, axis=0, keepdims=True)
    seen_sc[...] = seen
    idx_ref[...] = jnp.where(lane == 0, i1, jnp.where(lane == 1, i2, jnp.where(lane == 2, rank1,
                                                                                  jnp.where(lane == 3, rank2, 0))))
    gate_ref[...] = jnp.where(lane == 0, g1, jnp.where(lane == 1, g2, 0.0))
    cnt_ref[...] = jnp.broadcast_to(seen, cnt_ref.shape).astype(I32)


def _route(x, router, tm):
    n = x.shape[0]
    r_pad = jnp.concatenate([router, jnp.zeros((D_MODEL, LANES - N_EXPERTS), F32)], axis=1)
    r3 = jnp.stack(_split3(r_pad))
    rows = pl.BlockSpec((tm, LANES), lambda i: (i, 0))
    return pl.pallas_call(
        _router_kernel, grid=(n // tm,),
        in_specs=[pl.BlockSpec((tm, D_MODEL), lambda i: (i, 0)), pl.BlockSpec(r3.shape, lambda i: (0, 0, 0))],
        out_specs=[rows, rows, pl.BlockSpec((SUBLANES, LANES), lambda i: (0, 0))],
        out_shape=[jax.ShapeDtypeStruct((n, LANES), I32), jax.ShapeDtypeStruct((n, LANES), F32),
                   jax.ShapeDtypeStruct((SUBLANES, LANES), I32)],
        scratch_shapes=[pltpu.VMEM((1, LANES), F32)],
        compiler_params=_cparams("arbitrary"), name="moe_router")(x, r3)


def _tile_rows(idx):
    return pl.ds(pl.multiple_of(idx * TOKEN_ROWS, TOKEN_ROWS), TOKEN_ROWS)


def _dispatch_kernel(slot_ref, x_ref, init_ref, xs_ref, sem, *, tm):
    del init_ref
    base = pl.program_id(0) * tm

    def start(r, carry):
        src = x_ref.at[_tile_rows(r), :]
        for j in range(2):
            pltpu.make_async_copy(src, xs_ref.at[_tile_rows(slot_ref[2 * (base + r) + j]), :], sem).start()
        return carry

    lax.fori_loop(0, tm, start, 0, unroll=8)
    for _ in range(2):
        pltpu.make_async_copy(x_ref, xs_ref.at[pl.ds(0, tm * TOKEN_ROWS), :], sem).wait()


def _dispatch(slots_flat, xt, n_slots, tm):
    n = xt.shape[0] // TOKEN_ROWS
    any_spec = pl.BlockSpec(memory_space=pl.ANY)
    gs = pltpu.PrefetchScalarGridSpec(
        num_scalar_prefetch=1, grid=(n // tm,),
        in_specs=[pl.BlockSpec((tm * TOKEN_ROWS, LANES), lambda i, slots: (i, 0)), any_spec],
        out_specs=any_spec, scratch_shapes=[pltpu.SemaphoreType.DMA(())])
    return pl.pallas_call(
        functools.partial(_dispatch_kernel, tm=tm), grid_spec=gs,
        out_shape=jax.ShapeDtypeStruct((n_slots * TOKEN_ROWS, LANES), F32), input_output_aliases={2: 0},
        compiler_params=_cparams("arbitrary"), name="moe_dispatch",
    )(slots_flat, xt, jnp.zeros((n_slots * TOKEN_ROWS, LANES), F32))


def _moe_ffn_kernel(be_ref, nv_ref, xs_ref, wg_ref, wu_ref, wd_ref, y_ref, *, tm, tf):
    del be_ref
    live = pl.program_id(0) < nv_ref[0]

    @pl.when(live)
    def _():
        xb = _load_token_tiles(xs_ref, tm).astype(BF16)
        _store_token_tiles(y_ref, _swiglu(xb, wg_ref, wu_ref, wd_ref, tf))

    @pl.when(jnp.logical_not(live))
    def _():
        y_ref[...] = jnp.zeros_like(y_ref)


def _moe_ffn(block_expert, n_valid, xs, wg, wu, wd, tm, tf):
    p = xs.shape[0] // TOKEN_ROWS
    tiles = pl.BlockSpec((tm * TOKEN_ROWS, LANES), lambda i, be, nv: (i, 0))

    def expert(shape):
        return pl.BlockSpec((None,) + shape[1:], lambda i, be, nv: (be[i], 0, 0), pipeline_mode=pl.Buffered(1))

    gs = pltpu.PrefetchScalarGridSpec(
        num_scalar_prefetch=2, grid=(p // tm,),
        in_specs=[tiles, expert(wg.shape), expert(wu.shape), expert(wd.shape)], out_specs=tiles)
    return pl.pallas_call(
        functools.partial(_moe_ffn_kernel, tm=tm, tf=tf), grid_spec=gs, out_shape=jax.ShapeDtypeStruct(xs.shape, F32),
        compiler_params=_cparams("arbitrary"), name="moe_ffn")(block_expert, n_valid, xs, wg, wu, wd)


def _combine_kernel(slot_ref, x_ref, gate_ref, g_ref, b_ref, y_ref, o_ref, ybuf, sem, *, alpha, tm):
    i = pl.program_id(0)

    def gather(step, par):
        base = step * tm

        def start(r, carry):
            for j in range(2):
                src = y_ref.at[_tile_rows(slot_ref[2 * (base + r) + j]), :]
                pltpu.make_async_copy(src, ybuf.at[par, j, _tile_rows(r), :], sem.at[par]).start()
            return carry

        lax.fori_loop(0, tm, start, 0, unroll=8)

    @pl.when(i == 0)
    def _():
        gather(0, 0)

    @pl.when(i + 1 < pl.num_programs(0))
    def _():
        gather(i + 1, (i + 1) & 1)

    par = i & 1
    for j in range(2):
        pltpu.make_async_copy(y_ref.at[pl.ds(0, tm * TOKEN_ROWS), :], ybuf.at[par, j], sem.at[par]).wait()
    gt = gate_ref[...]
    ff = gt[:, 0:1] * _load_token_tiles(ybuf.at[par, 0], tm) + gt[:, 1:2] * _load_token_tiles(ybuf.at[par, 1], tm)
    o_ref[...] = _layer_norm(alpha * x_ref[...] + ff, g_ref[...], b_ref[...])


def _combine(slots_flat, x, gates, g, b, y, alpha, tm):
    n = x.shape[0]
    rows = pl.BlockSpec((tm, D_MODEL), lambda i, slots: (i, 0))
    vec = pl.BlockSpec((1, D_MODEL), lambda i, slots: (0, 0))
    gs = pltpu.PrefetchScalarGridSpec(
        num_scalar_prefetch=1, grid=(n // tm,),
        in_specs=[rows, pl.BlockSpec((tm, LANES), lambda i, slots: (i, 0)), vec, vec,
                  pl.BlockSpec(memory_space=pl.ANY)],
        out_specs=rows, scratch_shapes=[pltpu.VMEM((2, 2, tm * TOKEN_ROWS, LANES), F32),
                                        pltpu.SemaphoreType.DMA((2,))])
    return pl.pallas_call(
        functools.partial(_combine_kernel, alpha=alpha, tm=tm), grid_spec=gs,
        out_shape=jax.ShapeDtypeStruct((n, D_MODEL), F32), compiler_params=_cparams("arbitrary"),
        name="moe_combine")(slots_flat, x, gates, g, b, y)


def _moe_layer(x, xt, router, wg, wu, wd, g, b, alpha, tiles):
    n = x.shape[0]
    tmb = tiles["moe_rows"]
    idx, gates, counts = _route(x, router, tiles["route_rows"])
    counts = counts[0, :N_EXPERTS]
    padded = ((counts + tmb - 1) // tmb) * tmb
    pad_end = jnp.cumsum(padded)
    pad_start = pad_end - padded
    slots = jnp.stack([pad_start[idx[:, 0]] + idx[:, 2], pad_start[idx[:, 1]] + idx[:, 3]], axis=1)
    slots_flat = slots.reshape(-1).astype(I32)
    n_slots = 2 * n + N_EXPERTS * tmb
    n_blocks = n_slots // tmb
    block_expert = jnp.clip(jnp.searchsorted(pad_end, jnp.arange(n_blocks, dtype=I32) * tmb, side="right"),
                            0, N_EXPERTS - 1).astype(I32)
    n_valid = (pad_end[-1:] // tmb).astype(I32)
    xs = _dispatch(slots_flat, xt, n_slots, tiles["dma_rows"])
    y = _moe_ffn(block_expert, n_valid, xs, wg, wu, wd, tmb, tiles["ffn_cols"])
    return _combine(slots_flat, x, gates, g, b, y, alpha, tiles["dma_rows"])


def _tiles(n, t):
    return dict(
        proj_rows=min(256, t), prefix_rows=min(256, t), attn_q=min(256, t), row_tile=min(512, n),
        ffn_cols=512, route_rows=min(256, n), moe_rows=min(512, n), dma_rows=min(256, n))


def kernel(x, positions, w_in, b_forget, w_branch_a, w_branch_b, w_out, ln_mix_g, ln_mix_b, ln_ffn_g, ln_ffn_b,
           ffn_w_gate, ffn_w_up, ffn_w_down, moe_router, moe_w_gate, moe_w_up, moe_w_down):
    B, T, D = x.shape
    assert D == D_MODEL and T % CHUNK == 0
    n = B * T
    depth = w_in.shape[0]
    top_k = min(TOPK_MAX, T // 4)
    alpha = (2.0 * depth) ** 0.25
    tiles = _tiles(n, T)
    cos_t, sin_t = _rope_tables(positions)
    xf = x.reshape(n, D)
    for layer in range(depth):
        weights = _prep_mix_weights(w_in[layer], b_forget[layer])
        (dq, dk, iq, fq, fk, ga, gb, ik2, dvt, fvt, iwt, lf) = _in_projection(
            xf.reshape(B, T, D), cos_t, sin_t, weights, tiles["proj_rows"])
        aux = _forget_aux(lf, tiles["prefix_rows"])
        o_a = _dsa_attention(iq, ik2, iwt, dq, dk, dvt, tiles["attn_q"], top_k)
        o_b = _fox_attention(fq, fk, aux, fvt, tiles["attn_q"])
        dense = layer % 2 == 0
        x1, x1_alt = _merge_project(
            o_a.reshape(n, HEADS_W), o_b.reshape(n, HEADS_W), ga.reshape(n, D), gb.reshape(n, D), xf,
            w_branch_a[layer].astype(BF16), w_branch_b[layer].astype(BF16), w_out[layer].astype(BF16),
            ln_mix_g[layer][None, :], ln_mix_b[layer][None, :], alpha, tiles["row_tile"], token_tiles=not dense)
        j = layer // 2
        g, b = ln_ffn_g[layer][None, :], ln_ffn_b[layer][None, :]
        if dense:
            xf = _dense_ffn(x1_alt, x1, ffn_w_gate[j].astype(BF16), ffn_w_up[j].astype(BF16),
                            ffn_w_down[j].astype(BF16), g, b, alpha, tiles["row_tile"], tiles["ffn_cols"])
        else:
            xf = _moe_layer(x1, x1_alt, moe_router[j], moe_w_gate[j].astype(BF16), moe_w_up[j].astype(BF16),
                            moe_w_down[j].astype(BF16), g, b, alpha, tiles)
    return xf.reshape(B, T, D)
```

```python
import functools

import jax
import jax.numpy as jnp
from jax import lax
from jax.experimental import pallas as pl
from jax.experimental.pallas import tpu as pltpu

F32, BF16, I32, I16 = jnp.float32, jnp.bfloat16, jnp.int32, jnp.int16

D_MODEL = 1024
HEAD_DIM = 64
N_HEADS = 8
HEADS_W = N_HEADS * HEAD_DIM
N_PAIRS = N_HEADS // 2
CHUNK = 64
TOPK_MAX = 256
ROPE_THETA = 10000.0
D_FF = 3584
N_EXPERTS = 8
LN_EPS = 1e-5
DEPTH = 2
LANES = 128
SUBLANES = 8
PACKED_ROWS = 2 * SUBLANES
INT_MIN = -(2**31)
I16_MIN = -(2**15)
NEG = -1e30
VMEM_LIMIT = 56 * 1024 * 1024
NT_DIMS = (((1,), (1,)), ((), ()))


def _cparams(*sem):
    return pltpu.CompilerParams(dimension_semantics=sem, vmem_limit_bytes=VMEM_LIMIT)


def _split3(v):
    hi = v.astype(BF16)
    r1 = v - hi.astype(F32)
    mid = r1.astype(BF16)
    lo = (r1 - mid.astype(F32)).astype(BF16)
    return hi, mid, lo


def _layer_norm(y, g, b):
    mu = jnp.mean(y, axis=-1, keepdims=True)
    yc = y - mu
    var = jnp.mean(yc * yc, axis=-1, keepdims=True)
    return yc * lax.rsqrt(var + LN_EPS) * g + b


def _rope_kernel(pos_ref, freq_ref, sign_ref, cos_ref, sin_ref):
    ang = pos_ref[...].astype(F32) * freq_ref[...]
    cos_ref[...] = jnp.cos(ang)
    sin_ref[...] = jnp.sin(ang) * sign_ref[...]


def _rope_tables(positions):
    B, T = positions.shape
    n = B * T
    inv_freq = ROPE_THETA ** (-jnp.arange(0, HEAD_DIM, 2, dtype=F32) / HEAD_DIM)
    freq = jnp.tile(inv_freq, LANES // (HEAD_DIM // 2))[None, :]
    half = HEAD_DIM // 2
    sign = jnp.tile(jnp.concatenate([-jnp.ones((half,), F32), jnp.ones((half,), F32)]), LANES // HEAD_DIM)[None, :]
    pos_b = jnp.broadcast_to(positions.reshape(n, 1), (n, LANES))
    tm = min(n, 1024)
    row = pl.BlockSpec((tm, LANES), lambda i: (i, 0))
    const = pl.BlockSpec((1, LANES), lambda i: (0, 0))
    cos, sin = pl.pallas_call(
        _rope_kernel, grid=(n // tm,), in_specs=[row, const, const], out_specs=[row, row],
        out_shape=[jax.ShapeDtypeStruct((n, LANES), F32)] * 2, compiler_params=_cparams("parallel"),
        name="rope_tables")(pos_b, freq, sign)
    return cos.reshape(B, T, LANES), sin.reshape(B, T, LANES)


LOG2E = 1.4426950408889634
_Q_SCALE = HEAD_DIM ** -0.5 * LOG2E
_IW_SCALE = (N_HEADS ** -0.5) * (HEAD_DIM ** -0.5)
_OFF_DQ, _OFF_DK, _OFF_IQ, _OFF_FQ, _OFF_FK, _OFF_GA, _OFF_GB = 0, 512, 1024, 1536, 2048, 2560, 3584
_W_MAIN = 4608
_AUX_USED = 6


def _inproj_kernel(x_ref, cos_ref, sin_ref, wm_ref, wik_ref, wvt_ref, wit_ref, wfl_ref, bfl_ref,
                   dq_ref, dk_ref, iq_ref, fq_ref, fk_ref, ga_ref, gb_ref, ik2_ref, dvt_ref, fvt_ref,
                   iwt_ref, lf_ref):
    xb = x_ref[...].astype(BF16)
    cos = cos_ref[...]
    sin = sin_ref[...]
    lane = lax.broadcasted_iota(I32, cos.shape, 1)
    first_half = (lane & (HEAD_DIM - 1)) < HEAD_DIM // 2

    def rot(y):
        partner = jnp.where(first_half, pltpu.roll(y, LANES - HEAD_DIM // 2, 1), pltpu.roll(y, HEAD_DIM // 2, 1))
        return y * cos + partner * sin

    def proj(off, width):
        return jnp.dot(xb, wm_ref[:, off:off + width], preferred_element_type=F32)

    def rot_group(off, scale, out_ref):
        y = proj(off, HEADS_W)
        for j in range(N_PAIRS):
            r = rot(y[:, LANES * j:LANES * (j + 1)])
            if scale != 1.0:
                r = r * scale
            out_ref[:, LANES * j:LANES * (j + 1)] = r.astype(BF16)

    rot_group(_OFF_DQ, _Q_SCALE, dq_ref)
    rot_group(_OFF_DK, 1.0, dk_ref)
    rot_group(_OFF_IQ, 1.0, iq_ref)
    fq_ref[...] = (proj(_OFF_FQ, HEADS_W) * _Q_SCALE).astype(BF16)
    fk_ref[...] = proj(_OFF_FK, HEADS_W).astype(BF16)
    ga_ref[...] = jax.nn.sigmoid(proj(_OFF_GA, D_MODEL)).astype(BF16)
    gb_ref[...] = jax.nn.sigmoid(proj(_OFF_GB, D_MODEL)).astype(BF16)
    ik2_ref[...] = rot(jnp.dot(xb, wik_ref[...], preferred_element_type=F32)).astype(BF16)
    vt = lax.dot_general(wvt_ref[...], xb, NT_DIMS, preferred_element_type=F32)
    dvt_ref[...] = vt[:HEADS_W].astype(BF16)
    fvt_ref[...] = vt[HEADS_W:].astype(BF16)
    it = lax.dot_general(wit_ref[...], xb, NT_DIMS, preferred_element_type=F32)
    iwt_ref[...] = it[:N_HEADS] * _IW_SCALE
    z = jnp.dot(xb, wfl_ref[...], preferred_element_type=F32) + bfl_ref[...]
    lf = jnp.minimum(z, 0.0) - jnp.log1p(jnp.exp(-jnp.abs(z)))
    lane_w = lax.broadcasted_iota(I32, lf.shape, 1) & (LANES - 1)
    lf_ref[...] = jnp.where(lane_w < _AUX_USED, lf, 0.0)


def _prep_mix_weights(w_in, b_forget):
    sizes = (HEADS_W, HEADS_W, HEADS_W, HEADS_W, HEAD_DIM, N_HEADS, HEADS_W, HEADS_W, HEADS_W, N_HEADS, D_MODEL, D_MODEL)
    offs = [0]
    for s in sizes:
        offs.append(offs[-1] + s)
    dq, dk, dv, iq, ik, iw, fq, fk, fv, fl, ga, gb = [w_in[:, offs[i]:offs[i + 1]] for i in range(12)]
    w_main = jnp.concatenate([dq, dk, iq, fq, fk, ga, gb], axis=1).astype(BF16)
    w_ik2 = jnp.concatenate([ik, ik], axis=1).astype(BF16)
    w_vt = jnp.concatenate([dv, fv], axis=1).T.astype(BF16)
    w_it = jnp.concatenate([iw.T, jnp.zeros((16 - N_HEADS, D_MODEL), F32)], axis=0).astype(BF16)
    src = []
    for j in range(N_PAIRS):
        src += [2 * j] * 3 + [2 * j + 1] * 3 + [-1] * (LANES - _AUX_USED)
    src = jnp.array(src, I32)
    used = src >= 0
    w_fl = jnp.where(used[None, :], fl[:, jnp.maximum(src, 0)], 0.0).astype(BF16)
    b_fl = jnp.where(used, b_forget[jnp.maximum(src, 0)], 0.0)[None, :].astype(F32)
    return w_main, w_ik2, w_vt, w_it, w_fl, b_fl


def _in_projection(x, cos_t, sin_t, weights, tm):
    B, T, _ = x.shape
    w_main, w_ik2, w_vt, w_it, w_fl, b_fl = weights
    nt = T // tm

    def rows(width):
        return pl.BlockSpec((None, tm, width), lambda b, i: (b, i, 0))

    def cols(height):
        return pl.BlockSpec((None, height, tm), lambda b, i: (b, 0, i))

    def const(shape):
        return pl.BlockSpec(shape, lambda b, i: (0, 0), pipeline_mode=pl.Buffered(1))

    bt = lambda w, dt: jax.ShapeDtypeStruct((B, T, w), dt)
    tb = lambda h, dt: jax.ShapeDtypeStruct((B, h, T), dt)
    return pl.pallas_call(
        _inproj_kernel, grid=(B, nt),
        in_specs=[rows(D_MODEL), rows(LANES), rows(LANES), const(w_main.shape), const(w_ik2.shape),
                  const(w_vt.shape), const(w_it.shape), const(w_fl.shape), const(b_fl.shape)],
        out_specs=[rows(HEADS_W)] * 5 + [rows(D_MODEL)] * 2 + [rows(LANES), cols(HEADS_W), cols(HEADS_W),
                                                                cols(N_HEADS), rows(HEADS_W)],
        out_shape=[bt(HEADS_W, BF16)] * 5 + [bt(D_MODEL, BF16)] * 2 + [bt(LANES, BF16), tb(HEADS_W, BF16),
                                                                      tb(HEADS_W, BF16), tb(N_HEADS, F32),
                                                                      bt(HEADS_W, F32)],
        compiler_params=_cparams("parallel", "parallel"), name="in_projection",
    )(x, cos_t, sin_t, w_main, w_ik2, w_vt, w_it, w_fl, b_fl)


def _cumsum_kernel(lf_ref, aux_ref, carry_ref):
    @pl.when(pl.program_id(1) == 0)
    def _():
        carry_ref[...] = jnp.zeros_like(carry_ref)

    lf = lf_ref[...]
    tb = lf.shape[0]
    r = lax.broadcasted_iota(I32, (tb, tb), 0)
    c = lax.broadcasted_iota(I32, (tb, tb), 1)
    tri = jnp.where(c <= r, 1.0, 0.0).astype(BF16)
    hi, mid, lo = _split3(lf)
    cs = (jnp.dot(tri, lo, preferred_element_type=F32) + jnp.dot(tri, mid, preferred_element_type=F32)
          + jnp.dot(tri, hi, preferred_element_type=F32) + carry_ref[...])
    carry_ref[...] = cs[tb - 1:tb, :]
    nh, nm, nl = _split3(-LOG2E * cs)
    lane_w = lax.broadcasted_iota(I32, cs.shape, 1) & (LANES - 1)
    k = jnp.where(lane_w >= 3, lane_w - 3, lane_w)
    piece = jnp.where(k == 0, nh.astype(F32), jnp.where(k == 1, nm.astype(F32), nl.astype(F32)))
    aux_ref[...] = jnp.where(lane_w < _AUX_USED, piece, 0.0).astype(BF16)


def _forget_aux(lf, tb):
    B, T, W = lf.shape
    spec = pl.BlockSpec((None, tb, W), lambda b, i: (b, i, 0))
    return pl.pallas_call(
        _cumsum_kernel, grid=(B, T // tb), in_specs=[spec], out_specs=spec,
        out_shape=jax.ShapeDtypeStruct((B, T, W), BF16), scratch_shapes=[pltpu.VMEM((1, W), F32)],
        compiler_params=_cparams("parallel", "arbitrary"), name="forget_prefix")(lf)


def _stack_pair(qb):
    qf = qb.astype(F32)
    lane = lax.broadcasted_iota(I32, qf.shape, 1)
    lo = lane < HEAD_DIM
    return jnp.concatenate([jnp.where(lo, qf, 0.0), jnp.where(lo, 0.0, qf)], axis=0)


def _softmax_step(s, vb, j, m_sc, l_sc, acc_sc):
    tq = s.shape[1] // 2
    m_old = m_sc[j]
    m_new = jnp.maximum(m_old, jnp.max(s, axis=0, keepdims=True))
    alpha = jnp.exp2(m_old - m_new)
    p = jnp.exp2(s - m_new)
    l_sc[j] = alpha * l_sc[j] + jnp.sum(p, axis=0, keepdims=True)
    pb = p.astype(BF16)
    pv = jnp.concatenate([jnp.dot(vb[:HEAD_DIM], pb[:, :tq], preferred_element_type=F32),
                          jnp.dot(vb[HEAD_DIM:], pb[:, tq:], preferred_element_type=F32)], axis=1)
    acc_sc[j] = alpha * acc_sc[j] + pv
    m_sc[j] = m_new


def _init_softmax(m_sc, l_sc, acc_sc):
    m_sc[...] = jnp.full(m_sc.shape, -jnp.inf, F32)
    l_sc[...] = jnp.zeros(l_sc.shape, F32)
    acc_sc[...] = jnp.zeros(acc_sc.shape, F32)


def _finish_pair(o_ref, j, tq, l_sc, acc_sc):
    out_t = acc_sc[j] / l_sc[j]
    blk_t = jnp.concatenate([out_t[:, :tq], out_t[:, tq:]], axis=0)
    o_ref[:, LANES * j:LANES * (j + 1)] = blk_t.T.astype(BF16)


def _softmax_scratch(tq, q_width):
    return [pltpu.VMEM((N_PAIRS, q_width, 2 * tq), BF16), pltpu.VMEM((2, N_PAIRS, tq, 2 * tq), F32),
            pltpu.VMEM((N_PAIRS, 1, 2 * tq), F32), pltpu.VMEM((N_PAIRS, 1, 2 * tq), F32),
            pltpu.VMEM((N_PAIRS, HEAD_DIM, 2 * tq), F32)]


_STEPS_PER_ITER = 2


def _unrolled_loop(n, step, carry, unroll=_STEPS_PER_ITER):
    def main(h, c):
        for u in range(unroll):
            c = step(unroll * h + u, c)
        return c

    n_main = n // unroll
    carry = lax.fori_loop(0, n_main, main, carry)
    return lax.fori_loop(n_main * unroll, n, step, carry)


def _pipelined_attention(n_steps, scores, update):
    scores(0, 0)
    n_mid = n_steps - 1

    def body(h, carry):
        kv = _STEPS_PER_ITER * h
        for u in range(_STEPS_PER_ITER):
            scores(kv + u + 1, (u + 1) & 1)
            update(kv + u, u & 1, False)
        return carry

    n_main = n_mid // _STEPS_PER_ITER
    lax.fori_loop(0, n_main, body, 0)

    def tail(kv, carry):
        scores(kv + 1, (kv + 1) & 1)
        update(kv, kv & 1, False)
        return carry

    lax.fori_loop(n_main * _STEPS_PER_ITER, n_mid, tail, 0)
    update(n_steps - 1, (n_steps - 1) & 1, True)


def _fox_kernel(q_ref, k_ref, aux_ref, vt_ref, o_ref, qt_sc, s_sc, m_sc, l_sc, acc_sc, *, tq):
    qi = pl.program_id(1)
    q0 = qi * tq
    row2 = lax.broadcasted_iota(I32, (2 * tq, LANES), 0)
    lane2 = lax.broadcasted_iota(I32, (2 * tq, LANES), 1)
    sel = jnp.where(row2 < tq, jnp.where(lane2 < 3, 1.0, 0.0),
                    jnp.where(lane2 < 3, 0.0, jnp.where(lane2 < _AUX_USED, 1.0, 0.0)))
    for j in range(N_PAIRS):
        qa = jnp.concatenate([_stack_pair(q_ref[:, LANES * j:LANES * (j + 1)]), sel], axis=1)
        qt_sc[j] = qa.T.astype(BF16)
    _init_softmax(m_sc, l_sc, acc_sc)

    def scores(kv, slot):
        ks = pl.multiple_of(kv * tq, tq)
        for j in range(N_PAIRS):
            cs = slice(LANES * j, LANES * (j + 1))
            kb = jnp.concatenate([k_ref[pl.ds(ks, tq), cs], aux_ref[pl.ds(ks, tq), cs]], axis=1)
            s_sc[slot, j] = jnp.dot(kb, qt_sc[j], preferred_element_type=F32)

    def update(kv, slot, last):
        ks = pl.multiple_of(kv * tq, tq)
        if last:
            krow = ks + lax.broadcasted_iota(I32, (tq, 2 * tq), 0)
            qcol = q0 + (lax.broadcasted_iota(I32, (tq, 2 * tq), 1) & (tq - 1))
            keep = krow <= qcol
        for j in range(N_PAIRS):
            s = s_sc[slot, j]
            if last:
                s = jnp.where(keep, s, NEG)
            _softmax_step(s, vt_ref[LANES * j:LANES * (j + 1), pl.ds(ks, tq)], j, m_sc, l_sc, acc_sc)

    _pipelined_attention(qi + 1, scores, update)
    for j in range(N_PAIRS):
        _finish_pair(o_ref, j, tq, l_sc, acc_sc)


def _fox_attention(fq, fk, aux, fvt, tq):
    B, T, W = fq.shape
    qspec = pl.BlockSpec((None, tq, W), lambda b, i: (b, i, 0))
    full = pl.BlockSpec((None, T, W), lambda b, i: (b, 0, 0))
    fullt = pl.BlockSpec((None, W, T), lambda b, i: (b, 0, 0))
    return pl.pallas_call(
        functools.partial(_fox_kernel, tq=tq), grid=(B, T // tq),
        in_specs=[qspec, full, full, fullt], out_specs=qspec,
        out_shape=jax.ShapeDtypeStruct((B, T, W), BF16), scratch_shapes=_softmax_scratch(tq, 2 * LANES),
        compiler_params=_cparams("parallel", "arbitrary"), name="fox_attention")(fq, fk, aux, fvt)


_COUNT_CHAINS = 4


def _dsa_kernel(iq_ref, ik2_ref, iwt_ref, q_ref, k_ref, vt_ref, o_ref, key_sc, hi_sc, lo_sc, bias_sc, qt_sc, s_sc,
                m_sc, l_sc, acc_sc, *, tq, top_k):
    qi = pl.program_id(1)
    q0 = qi * tq
    col = lax.broadcasted_iota(I32, (1, tq), 1)
    limit = (lax.shift_right_logical(q0 + col, CHUNK.bit_length() - 1) + 1) * CHUNK
    iw = iwt_ref[...]

    for j in range(N_PAIRS):
        qt_sc[j] = _stack_pair(iq_ref[:, LANES * j:LANES * (j + 1)]).T.astype(BF16)

    def score_tile(kv, masked):
        ks = pl.multiple_of(kv * tq, tq)
        kb = ik2_ref[pl.ds(ks, tq), :]
        sc = jnp.zeros((tq, tq), F32)
        for j in range(N_PAIRS):
            s2 = jnp.dot(kb, qt_sc[j], preferred_element_type=F32)
            sc = sc + jnp.maximum(s2[:, :tq], 0.0) * iw[2 * j:2 * j + 1, :]
            sc = sc + jnp.maximum(s2[:, tq:], 0.0) * iw[2 * j + 1:2 * j + 2, :]
        bits = pltpu.bitcast(sc, I32)
        key = bits ^ ((bits >> 31) & jnp.int32(0x7FFFFFFF))
        key = jnp.where(bits == jnp.int32(INT_MIN), 0, key)
        if masked:
            krow = ks + lax.broadcasted_iota(I32, key.shape, 0)
            key = jnp.where(krow < limit, key, jnp.int32(INT_MIN))
        key_sc[pl.ds(ks, tq), :] = key
        hi_sc[pl.ds(ks, tq), :] = (key >> 16).astype(I16)
        lo_sc[pl.ds(ks, tq), :] = ((key & 0xFFFF) + I16_MIN).astype(I16)

    def score_body(kv, carry):
        score_tile(kv, False)
        return carry

    _unrolled_loop(qi, score_body, 0)
    score_tile(qi, True)

    def count(plane_sc, pred):
        def body(g, accs):
            slab = plane_sc[pl.ds(pl.multiple_of(g * tq, tq), tq), :]
            accs = list(accs)
            for r in range(tq // PACKED_ROWS):
                blk = slab[PACKED_ROWS * r:PACKED_ROWS * (r + 1), :]
                hit = jnp.where(pred(blk), jnp.int16(1), jnp.int16(0))
                accs[r % _COUNT_CHAINS] = accs[r % _COUNT_CHAINS] + hit
            return tuple(accs)

        zero = jnp.zeros((PACKED_ROWS, tq), I16)
        accs = lax.fori_loop(0, qi + 1, body, (zero,) * _COUNT_CHAINS)
        return jnp.sum(sum(a.astype(I32) for a in accs), axis=0, keepdims=True)

    def rows16(v):
        return jnp.broadcast_to(v.astype(I16), (PACKED_ROWS, tq))

    def search(plane_sc, above):
        def body(i, v):
            cand = v + lax.shift_left(jnp.int32(1), 15 - i)
            cand16 = rows16(cand)
            cnt = above + count(plane_sc, lambda blk: blk >= cand16)
            return jnp.where(cnt >= top_k, cand, v)

        return lax.fori_loop(0, 16, body, jnp.full((1, tq), I16_MIN, I32))

    t_hi = search(hi_sc, 0)
    t_hi16 = rows16(t_hi)
    above_hi = count(hi_sc, lambda blk: blk > t_hi16)

    def mask_low(g, carry):
        rows = pl.ds(pl.multiple_of(g * tq, tq), tq)
        lo_sc[rows, :] = jnp.where(hi_sc[rows, :] == t_hi.astype(I16), lo_sc[rows, :], jnp.int16(I16_MIN))
        return carry

    lax.fori_loop(0, qi + 1, mask_low, 0)
    t_lo = search(lo_sc, above_hi)
    t_lo16 = rows16(t_lo)
    thr = t_hi * 65536 + (t_lo - I16_MIN)
    n_ge = above_hi + count(lo_sc, lambda blk: blk >= t_lo16)
    tie_walk = jnp.maximum(jnp.where(n_ge > top_k, 1, 0), jnp.where(t_lo == I16_MIN, 1, 0))
    any_tie_walk = jnp.max(tie_walk) > 0

    for j in range(N_PAIRS):
        qt_sc[j] = _stack_pair(q_ref[:, LANES * j:LANES * (j + 1)]).T.astype(BF16)
    _init_softmax(m_sc, l_sc, acc_sc)

    def scores(kv, slot):
        ks = pl.multiple_of(kv * tq, tq)
        for j in range(N_PAIRS):
            s_sc[slot, j] = jnp.dot(k_ref[pl.ds(ks, tq), LANES * j:LANES * (j + 1)], qt_sc[j],
                                    preferred_element_type=F32)

    def attend(select):
        def update(kv, slot, last):
            del last
            ks = pl.multiple_of(kv * tq, tq)
            keep_selected = select(ks)
            for j in range(N_PAIRS):
                _softmax_step(keep_selected(s_sc[slot, j]), vt_ref[LANES * j:LANES * (j + 1), pl.ds(ks, tq)], j,
                              m_sc, l_sc, acc_sc)

        _pipelined_attention(qi + 1, scores, update)

    @pl.when(jnp.logical_not(any_tie_walk))
    def _():
        def select(ks):
            b = jnp.where(key_sc[pl.ds(ks, tq), :] >= thr, 0.0, NEG)
            b2 = jnp.concatenate([b, b], axis=1)
            return lambda s: s + b2

        attend(select)

    @pl.when(any_tie_walk)
    def _():
        need = (top_k - above_hi - count(lo_sc, lambda blk: blk > t_lo16)).astype(F32)
        r = lax.broadcasted_iota(I32, (tq, tq), 0)
        c = lax.broadcasted_iota(I32, (tq, tq), 1)
        tri = jnp.where(c < r, 1.0, 0.0).astype(BF16)

        def bias_tile(kv, seen, masked):
            ks = pl.multiple_of(kv * tq, tq)
            key = key_sc[pl.ds(ks, tq), :]
            eqf = jnp.where(key == thr, 1.0, 0.0)
            before = jnp.dot(tri, eqf.astype(BF16), preferred_element_type=F32) + seen
            b = jnp.where(key > thr, 0.0, jnp.where(key == thr, jnp.where(before < need, 0.0, NEG), NEG))
            if masked:
                krow = ks + lax.broadcasted_iota(I32, key.shape, 0)
                b = jnp.where(krow < limit, b, NEG)
            bias_sc[pl.ds(ks, tq), :] = b
            return seen + jnp.sum(eqf, axis=0, keepdims=True)

        seen = _unrolled_loop(qi, lambda kv, s: bias_tile(kv, s, False), jnp.zeros((1, tq), F32))
        bias_tile(qi, seen, True)

        def select(ks):
            b = bias_sc[pl.ds(ks, tq), :]
            b2 = jnp.concatenate([b, b], axis=1)
            return lambda s: s + b2

        attend(select)

    for j in range(N_PAIRS):
        _finish_pair(o_ref, j, tq, l_sc, acc_sc)


def _dsa_attention(iq, ik2, iwt, dq, dk, dvt, tq, top_k):
    B, T, W = dq.shape
    qspec = pl.BlockSpec((None, tq, W), lambda b, i: (b, i, 0))
    full = pl.BlockSpec((None, T, W), lambda b, i: (b, 0, 0))
    fullt = pl.BlockSpec((None, W, T), lambda b, i: (b, 0, 0))
    return pl.pallas_call(
        functools.partial(_dsa_kernel, tq=tq, top_k=top_k), grid=(B, T // tq),
        in_specs=[qspec, pl.BlockSpec((None, T, LANES), lambda b, i: (b, 0, 0)),
                  pl.BlockSpec((None, N_HEADS, tq), lambda b, i: (b, 0, i)), qspec, full, fullt],
        out_specs=qspec, out_shape=jax.ShapeDtypeStruct((B, T, W), BF16),
        scratch_shapes=[pltpu.VMEM((T, tq), I32), pltpu.VMEM((T, tq), I16), pltpu.VMEM((T, tq), I16),
                        pltpu.VMEM((T, tq), F32)] + _softmax_scratch(tq, LANES),
        compiler_params=_cparams("parallel", "arbitrary"), name="dsa_attention")(iq, ik2, iwt, dq, dk, dvt)


TOKEN_ROWS = D_MODEL // LANES
assert TOKEN_ROWS == SUBLANES


def _store_token_tiles(ref, y):
    tm = y.shape[0]
    for s in range(TOKEN_ROWS):
        ref[pl.ds(s, tm, stride=TOKEN_ROWS), :] = y[:, LANES * s:LANES * (s + 1)]


def _load_token_tiles(ref, tm):
    return jnp.concatenate([ref[pl.ds(s, tm, stride=TOKEN_ROWS), :] for s in range(TOKEN_ROWS)], axis=1)


def _merge_kernel(oa_ref, ob_ref, ga_ref, gb_ref, x_ref, wa_ref, wb_ref, wo_ref, g_ref, b_ref, o_ref, o2_ref,
                  *, alpha, token_tiles):
    a = jnp.dot(oa_ref[...], wa_ref[...], preferred_element_type=F32)
    b = jnp.dot(ob_ref[...], wb_ref[...], preferred_element_type=F32)
    merged = ga_ref[...].astype(F32) * a + gb_ref[...].astype(F32) * b
    mix = jnp.dot(merged.astype(BF16), wo_ref[...], preferred_element_type=F32)
    y = _layer_norm(alpha * x_ref[...] + mix, g_ref[...], b_ref[...])
    o_ref[...] = y
    if token_tiles:
        _store_token_tiles(o2_ref, y)
    else:
        o2_ref[...] = y.astype(BF16)


def _merge_project(oa, ob, ga, gb, x, wa, wb, wo, g, b, alpha, tm, token_tiles):
    n = x.shape[0]

    def rows(w):
        return pl.BlockSpec((tm, w), lambda i: (i, 0))

    def const(shape):
        return pl.BlockSpec(shape, lambda i: (0, 0), pipeline_mode=pl.Buffered(1))

    if token_tiles:
        out2_spec = pl.BlockSpec((tm * TOKEN_ROWS, LANES), lambda i: (i, 0))
        out2_shape = jax.ShapeDtypeStruct((n * TOKEN_ROWS, LANES), F32)
    else:
        out2_spec, out2_shape = rows(D_MODEL), jax.ShapeDtypeStruct((n, D_MODEL), BF16)
    return pl.pallas_call(
        functools.partial(_merge_kernel, alpha=alpha, token_tiles=token_tiles), grid=(n // tm,),
        in_specs=[rows(HEADS_W), rows(HEADS_W), rows(D_MODEL), rows(D_MODEL), rows(D_MODEL), const(wa.shape),
                  const(wb.shape), const(wo.shape), const(g.shape), const(b.shape)],
        out_specs=[rows(D_MODEL), out2_spec], out_shape=[jax.ShapeDtypeStruct((n, D_MODEL), F32), out2_shape],
        compiler_params=_cparams("parallel"), name="merge_project")(oa, ob, ga, gb, x, wa, wb, wo, g, b)


def _swiglu_partial(xb, wg, wu, wd):
    h = jnp.dot(xb, wg, preferred_element_type=F32)
    u = jnp.dot(xb, wu, preferred_element_type=F32)
    hid = (h * jax.nn.sigmoid(h)) * u
    return jnp.dot(hid.astype(BF16), wd, preferred_element_type=F32)


def _swiglu(xb, wg_ref, wu_ref, wd_ref, tf):
    acc = None
    for f in range(D_FF // tf):
        part = _swiglu_partial(xb, wg_ref[:, f * tf:(f + 1) * tf], wu_ref[:, f * tf:(f + 1) * tf],
                               wd_ref[f * tf:(f + 1) * tf, :])
        acc = part if acc is None else acc + part
    return acc


def _ffn_kernel(xb_ref, x_ref, wg_ref, wu_ref, wd_ref, g_ref, b_ref, o_ref, *, alpha, tf):
    ff = _swiglu(xb_ref[...], wg_ref, wu_ref, wd_ref, tf)
    o_ref[...] = _layer_norm(alpha * x_ref[...] + ff, g_ref[...], b_ref[...])


def _dense_ffn(xb, x, wg, wu, wd, g, b, alpha, tm, tf):
    n = x.shape[0]
    rows = pl.BlockSpec((tm, D_MODEL), lambda i: (i, 0))

    def const(shape):
        return pl.BlockSpec(shape, lambda i: (0, 0), pipeline_mode=pl.Buffered(1))

    return pl.pallas_call(
        functools.partial(_ffn_kernel, alpha=alpha, tf=tf), grid=(n // tm,),
        in_specs=[rows, rows, const(wg.shape), const(wu.shape), const(wd.shape), const(g.shape), const(b.shape)],
        out_specs=rows, out_shape=jax.ShapeDtypeStruct((n, D_MODEL), F32),
        compiler_params=_cparams("parallel"), name="dense_ffn")(xb, x, wg, wu, wd, g, b)


def _router_kernel(x_ref, r_ref, idx_ref, gate_ref, cnt_ref, seen_sc):
    i = pl.program_id(0)

    @pl.when(i == 0)
    def _():
        seen_sc[...] = jnp.zeros_like(seen_sc)

    xs = _split3(x_ref[...])
    logits = jnp.zeros((x_ref.shape[0], LANES), F32)
    for a, b in ((2, 0), (1, 1), (0, 2), (1, 0), (0, 1), (0, 0)):
        logits = logits + jnp.dot(xs[a], r_ref[b], preferred_element_type=F32)
    lane = lax.broadcasted_iota(I32, logits.shape, 1)
    lg = jnp.where(lane < N_EXPERTS, logits, -jnp.inf)
    v1 = jnp.max(lg, axis=-1, keepdims=True)
    i1 = jnp.min(jnp.where(lg == v1, lane, LANES), axis=-1, keepdims=True)
    lg2 = jnp.where(lane == i1, -jnp.inf, lg)
    v2 = jnp.max(lg2, axis=-1, keepdims=True)
    i2 = jnp.min(jnp.where(lg2 == v2, lane, LANES), axis=-1, keepdims=True)
    e = jnp.exp(v2 - v1)
    g1 = 1.0 / (1.0 + e)
    g2 = e / (1.0 + e)
    hit1 = lane == i1
    hit2 = lane == i2
    cnt = jnp.where(hit1, 1.0, jnp.where(hit2, 1.0, 0.0))
    tm = cnt.shape[0]
    r = lax.broadcasted_iota(I32, (tm, tm), 0)
    c = lax.broadcasted_iota(I32, (tm, tm), 1)
    tri = jnp.where(c < r, 1.0, 0.0).astype(BF16)
    before = jnp.dot(tri, cnt.astype(BF16), preferred_element_type=F32) + seen_sc[...]
    rank1 = jnp.sum(jnp.where(hit1, before, 0.0), axis=-1, keepdims=True).astype(I32)
    rank2 = jnp.sum(jnp.where(hit2, before, 0.0), axis=-1, keepdims=True).astype(I32)
    seen = seen_sc[...] + jnp.sum(cnt, axis=0, keepdims=True)
    seen_sc[...] = seen
    idx_ref[...] = jnp.where(lane == 0, i1, jnp.where(lane == 1, i2, jnp.where(lane == 2, rank1,
                                                                                  jnp.where(lane == 3, rank2, 0))))
    gate_ref[...] = jnp.where(lane == 0, g1, jnp.where(lane == 1, g2, 0.0))
    cnt_ref[...] = jnp.broadcast_to(seen, cnt_ref.shape).astype(I32)


def _route(x, router, tm):
    n = x.shape[0]
    r_pad = jnp.concatenate([router, jnp.zeros((D_MODEL, LANES - N_EXPERTS), F32)], axis=1)
    r3 = jnp.stack(_split3(r_pad))
    rows = pl.BlockSpec((tm, LANES), lambda i: (i, 0))
    return pl.pallas_call(
        _router_kernel, grid=(n // tm,),
        in_specs=[pl.BlockSpec((tm, D_MODEL), lambda i: (i, 0)), pl.BlockSpec(r3.shape, lambda i: (0, 0, 0))],
        out_specs=[rows, rows, pl.BlockSpec((SUBLANES, LANES), lambda i: (0, 0))],
        out_shape=[jax.ShapeDtypeStruct((n, LANES), I32), jax.ShapeDtypeStruct((n, LANES), F32),
                   jax.ShapeDtypeStruct((SUBLANES, LANES), I32)],
        scratch_shapes=[pltpu.VMEM((1, LANES), F32)],
        compiler_params=_cparams("arbitrary"), name="moe_router")(x, r3)


def _tile_rows(idx):
    return pl.ds(pl.multiple_of(idx * TOKEN_ROWS, TOKEN_ROWS), TOKEN_ROWS)


def _dispatch_kernel(slot_ref, x_ref, init_ref, xs_ref, sem, *, tm):
    del init_ref
    base = pl.program_id(0) * tm

    def start(r, carry):
        src = x_ref.at[_tile_rows(r), :]
        for j in range(2):
            pltpu.make_async_copy(src, xs_ref.at[_tile_rows(slot_ref[2 * (base + r) + j]), :], sem).start()
        return carry

    lax.fori_loop(0, tm, start, 0, unroll=8)
    for _ in range(2):
        pltpu.make_async_copy(x_ref, xs_ref.at[pl.ds(0, tm * TOKEN_ROWS), :], sem).wait()


def _dispatch(slots_flat, xt, n_slots, tm):
    n = xt.shape[0] // TOKEN_ROWS
    any_spec = pl.BlockSpec(memory_space=pl.ANY)
    gs = pltpu.PrefetchScalarGridSpec(
        num_scalar_prefetch=1, grid=(n // tm,),
        in_specs=[pl.BlockSpec((tm * TOKEN_ROWS, LANES), lambda i, slots: (i, 0)), any_spec],
        out_specs=any_spec, scratch_shapes=[pltpu.SemaphoreType.DMA(())])
    return pl.pallas_call(
        functools.partial(_dispatch_kernel, tm=tm), grid_spec=gs,
        out_shape=jax.ShapeDtypeStruct((n_slots * TOKEN_ROWS, LANES), F32), input_output_aliases={2: 0},
        compiler_params=_cparams("arbitrary"), name="moe_dispatch",
    )(slots_flat, xt, jnp.zeros((n_slots * TOKEN_ROWS, LANES), F32))


def _moe_ffn_kernel(be_ref, nv_ref, xs_ref, wg_ref, wu_ref, wd_ref, y_ref, *, tm, tf):
    del be_ref
    live = pl.program_id(0) < nv_ref[0]

    @pl.when(live)
    def _():
        xb = _load_token_tiles(xs_ref, tm).astype(BF16)
        _store_token_tiles(y_ref, _swiglu(xb, wg_ref, wu_ref, wd_ref, tf))

    @pl.when(jnp.logical_not(live))
    def _():
        y_ref[...] = jnp.zeros_like(y_ref)


def _moe_ffn(block_expert, n_valid, xs, wg, wu, wd, tm, tf):
    p = xs.shape[0] // TOKEN_ROWS
    tiles = pl.BlockSpec((tm * TOKEN_ROWS, LANES), lambda i, be, nv: (i, 0))

    def expert(shape):
        return pl.BlockSpec((None,) + shape[1:], lambda i, be, nv: (be[i], 0, 0), pipeline_mode=pl.Buffered(1))

    gs = pltpu.PrefetchScalarGridSpec(
        num_scalar_prefetch=2, grid=(p // tm,),
        in_specs=[tiles, expert(wg.shape), expert(wu.shape), expert(wd.shape)], out_specs=tiles)
    return pl.pallas_call(
        functools.partial(_moe_ffn_kernel, tm=tm, tf=tf), grid_spec=gs, out_shape=jax.ShapeDtypeStruct(xs.shape, F32),
        compiler_params=_cparams("arbitrary"), name="moe_ffn")(block_expert, n_valid, xs, wg, wu, wd)


def _combine_kernel(slot_ref, x_ref, gate_ref, g_ref, b_ref, y_ref, o_ref, ybuf, sem, *, alpha, tm):
    i = pl.program_id(0)

    def gather(step, par):
        base = step * tm

        def start(r, carry):
            for j in range(2):
                src = y_ref.at[_tile_rows(slot_ref[2 * (base + r) + j]), :]
                pltpu.make_async_copy(src, ybuf.at[par, j, _tile_rows(r), :], sem.at[par]).start()
            return carry

        lax.fori_loop(0, tm, start, 0, unroll=8)

    @pl.when(i == 0)
    def _():
        gather(0, 0)

    @pl.when(i + 1 < pl.num_programs(0))
    def _():
        gather(i + 1, (i + 1) & 1)

    par = i & 1
    for j in range(2):
        pltpu.make_async_copy(y_ref.at[pl.ds(0, tm * TOKEN_ROWS), :], ybuf.at[par, j], sem.at[par]).wait()
    gt = gate_ref[...]
    ff = gt[:, 0:1] * _load_token_tiles(ybuf.at[par, 0], tm) + gt[:, 1:2] * _load_token_tiles(ybuf.at[par, 1], tm)
    o_ref[...] = _layer_norm(alpha * x_ref[...] + ff, g_ref[...], b_ref[...])


def _combine(slots_flat, x, gates, g, b, y, alpha, tm):
    n = x.shape[0]
    rows = pl.BlockSpec((tm, D_MODEL), lambda i, slots: (i, 0))
    vec = pl.BlockSpec((1, D_MODEL), lambda i, slots: (0, 0))
    gs = pltpu.PrefetchScalarGridSpec(
        num_scalar_prefetch=1, grid=(n // tm,),
        in_specs=[rows, pl.BlockSpec((tm, LANES), lambda i, slots: (i, 0)), vec, vec,
                  pl.BlockSpec(memory_space=pl.ANY)],
        out_specs=rows, scratch_shapes=[pltpu.VMEM((2, 2, tm * TOKEN_ROWS, LANES), F32),
                                        pltpu.SemaphoreType.DMA((2,))])
    return pl.pallas_call(
        functools.partial(_combine_kernel, alpha=alpha, tm=tm), grid_spec=gs,
        out_shape=jax.ShapeDtypeStruct((n, D_MODEL), F32), compiler_params=_cparams("arbitrary"),
        name="moe_combine")(slots_flat, x, gates, g, b, y)


def _moe_layer(x, xt, router, wg, wu, wd, g, b, alpha, tiles):
    n = x.shape[0]
    tmb = tiles["moe_rows"]
    idx, gates, counts = _route(x, router, tiles["route_rows"])
    counts = counts[0, :N_EXPERTS]
    padded = ((counts + tmb - 1) // tmb) * tmb
    pad_end = jnp.cumsum(padded)
    pad_start = pad_end - padded
    slots = jnp.stack([pad_start[idx[:, 0]] + idx[:, 2], pad_start[idx[:, 1]] + idx[:, 3]], axis=1)
    slots_flat = slots.reshape(-1).astype(I32)
    n_slots = 2 * n + N_EXPERTS * tmb
    n_blocks = n_slots // tmb
    block_expert = jnp.clip(jnp.searchsorted(pad_end, jnp.arange(n_blocks, dtype=I32) * tmb, side="right"),
                            0, N_EXPERTS - 1).astype(I32)
    n_valid = (pad_end[-1:] // tmb).astype(I32)
    xs = _dispatch(slots_flat, xt, n_slots, tiles["dma_rows"])
    y = _moe_ffn(block_expert, n_valid, xs, wg, wu, wd, tmb, tiles["ffn_cols"])
    return _combine(slots_flat, x, gates, g, b, y, alpha, tiles["dma_rows"])


def _tiles(n, t):
    return dict(
        proj_rows=min(256, t), prefix_rows=min(256, t), attn_q=min(256, t), row_tile=min(512, n),
        ffn_cols=512, route_rows=min(256, n), moe_rows=min(512, n), dma_rows=min(256, n))


def kernel(x, positions, w_in, b_forget, w_branch_a, w_branch_b, w_out, ln_mix_g, ln_mix_b, ln_ffn_g, ln_ffn_b,
           ffn_w_gate, ffn_w_up, ffn_w_down, moe_router, moe_w_gate, moe_w_up, moe_w_down):
    B, T, D = x.shape
    assert D == D_MODEL and T % CHUNK == 0
    n = B * T
    depth = w_in.shape[0]
    top_k = min(TOPK_MAX, T // 4)
    alpha = (2.0 * depth) ** 0.25
    tiles = _tiles(n, T)
    cos_t, sin_t = _rope_tables(positions)
    xf = x.reshape(n, D)
    for layer in range(depth):
        weights = _prep_mix_weights(w_in[layer], b_forget[layer])
        (dq, dk, iq, fq, fk, ga, gb, ik2, dvt, fvt, iwt, lf) = _in_projection(
            xf.reshape(B, T, D), cos_t, sin_t, weights, tiles["proj_rows"])
        aux = _forget_aux(lf, tiles["prefix_rows"])
        o_a = _dsa_attention(iq, ik2, iwt, dq, dk, dvt, tiles["attn_q"], top_k)
        o_b = _fox_attention(fq, fk, aux, fvt, tiles["attn_q"])
        dense = layer % 2 == 0
        x1, x1_alt = _merge_project(
            o_a.reshape(n, HEADS_W), o_b.reshape(n, HEADS_W), ga.reshape(n, D), gb.reshape(n, D), xf,
            w_branch_a[layer].astype(BF16), w_branch_b[layer].astype(BF16), w_out[layer].astype(BF16),
            ln_mix_g[layer][None, :], ln_mix_b[layer][None, :], alpha, tiles["row_tile"], token_tiles=not dense)
        j = layer // 2
        g, b = ln_ffn_g[layer][None, :], ln_ffn_b[layer][None, :]
        if dense:
            xf = _dense_ffn(x1_alt, x1, ffn_w_gate[j].astype(BF16), ffn_w_up[j].astype(BF16),
                            ffn_w_down[j].astype(BF16), g, b, alpha, tiles["row_tile"], tiles["ffn_cols"])
        else:
            xf = _moe_layer(x1, x1_alt, moe_router[j], moe_w_gate[j].astype(BF16), moe_w_up[j].astype(BF16),
                            moe_w_down[j].astype(BF16), g, b, alpha, tiles)
    return xf.reshape(B, T, D)
```

```python
import functools

import jax
import jax.numpy as jnp
from jax import lax
from jax.experimental import pallas as pl
from jax.experimental.pallas import tpu as pltpu

F32, BF16, I32, I16 = jnp.float32, jnp.bfloat16, jnp.int32, jnp.int16

D_MODEL = 1024
HEAD_DIM = 64
N_HEADS = 8
HEADS_W = N_HEADS * HEAD_DIM
N_PAIRS = N_HEADS // 2
CHUNK = 64
TOPK_MAX = 256
ROPE_THETA = 10000.0
D_FF = 3584
N_EXPERTS = 8
LN_EPS = 1e-5
DEPTH = 2
LANES = 128
SUBLANES = 8
PACKED_ROWS = 2 * SUBLANES
INT_MIN = -(2**31)
I16_MIN = -(2**15)
NEG = -1e30
VMEM_LIMIT = 56 * 1024 * 1024
NT_DIMS = (((1,), (1,)), ((), ()))


def _cparams(*sem):
    return pltpu.CompilerParams(dimension_semantics=sem, vmem_limit_bytes=VMEM_LIMIT)


def _split3(v):
    hi = v.astype(BF16)
    r1 = v - hi.astype(F32)
    mid = r1.astype(BF16)
    lo = (r1 - mid.astype(F32)).astype(BF16)
    return hi, mid, lo


def _layer_norm(y, g, b):
    mu = jnp.mean(y, axis=-1, keepdims=True)
    yc = y - mu
    var = jnp.mean(yc * yc, axis=-1, keepdims=True)
    return yc * lax.rsqrt(var + LN_EPS) * g + b


def _rope_kernel(pos_ref, freq_ref, sign_ref, cos_ref, sin_ref):
    ang = pos_ref[...].astype(F32) * freq_ref[...]
    cos_ref[...] = jnp.cos(ang)
    sin_ref[...] = jnp.sin(ang) * sign_ref[...]


def _rope_tables(positions):
    B, T = positions.shape
    n = B * T
    inv_freq = ROPE_THETA ** (-jnp.arange(0, HEAD_DIM, 2, dtype=F32) / HEAD_DIM)
    freq = jnp.tile(inv_freq, LANES // (HEAD_DIM // 2))[None, :]
    half = HEAD_DIM // 2
    sign = jnp.tile(jnp.concatenate([-jnp.ones((half,), F32), jnp.ones((half,), F32)]), LANES // HEAD_DIM)[None, :]
    pos_b = jnp.broadcast_to(positions.reshape(n, 1), (n, LANES))
    tm = min(n, 1024)
    row = pl.BlockSpec((tm, LANES), lambda i: (i, 0))
    const = pl.BlockSpec((1, LANES), lambda i: (0, 0))
    cos, sin = pl.pallas_call(
        _rope_kernel, grid=(n // tm,), in_specs=[row, const, const], out_specs=[row, row],
        out_shape=[jax.ShapeDtypeStruct((n, LANES), F32)] * 2, compiler_params=_cparams("parallel"),
        name="rope_tables")(pos_b, freq, sign)
    return cos.reshape(B, T, LANES), sin.reshape(B, T, LANES)


LOG2E = 1.4426950408889634
_Q_SCALE = HEAD_DIM ** -0.5 * LOG2E
_IW_SCALE = (N_HEADS ** -0.5) * (HEAD_DIM ** -0.5)
_OFF_DQ, _OFF_DK, _OFF_IQ, _OFF_FQ, _OFF_FK, _OFF_GA, _OFF_GB = 0, 512, 1024, 1536, 2048, 2560, 3584
_W_MAIN = 4608
_AUX_USED = 6


def _inproj_kernel(x_ref, cos_ref, sin_ref, wm_ref, wik_ref, wvt_ref, wit_ref, wfl_ref, bfl_ref,
                   dq_ref, dk_ref, iq_ref, fq_ref, fk_ref, ga_ref, gb_ref, ik2_ref, dvt_ref, fvt_ref,
                   iwt_ref, aux_ref, carry_sc):
    @pl.when(pl.program_id(1) == 0)
    def _():
        carry_sc[...] = jnp.zeros_like(carry_sc)

    xb = x_ref[...].astype(BF16)
    cos = cos_ref[...]
    sin = sin_ref[...]
    lane = lax.broadcasted_iota(I32, cos.shape, 1)
    first_half = (lane & (HEAD_DIM - 1)) < HEAD_DIM // 2

    def rot(y):
        partner = jnp.where(first_half, pltpu.roll(y, LANES - HEAD_DIM // 2, 1), pltpu.roll(y, HEAD_DIM // 2, 1))
        return y * cos + partner * sin

    def proj(off, width):
        return jnp.dot(xb, wm_ref[:, off:off + width], preferred_element_type=F32)

    def rot_group(off, scale, out_ref):
        y = proj(off, HEADS_W)
        for j in range(N_PAIRS):
            r = rot(y[:, LANES * j:LANES * (j + 1)])
            if scale != 1.0:
                r = r * scale
            out_ref[:, LANES * j:LANES * (j + 1)] = r.astype(BF16)

    rot_group(_OFF_DQ, _Q_SCALE, dq_ref)
    rot_group(_OFF_DK, 1.0, dk_ref)
    rot_group(_OFF_IQ, 1.0, iq_ref)
    fq_ref[...] = (proj(_OFF_FQ, HEADS_W) * _Q_SCALE).astype(BF16)
    fk_ref[...] = proj(_OFF_FK, HEADS_W).astype(BF16)
    ga_ref[...] = jax.nn.sigmoid(proj(_OFF_GA, D_MODEL)).astype(BF16)
    gb_ref[...] = jax.nn.sigmoid(proj(_OFF_GB, D_MODEL)).astype(BF16)
    ik2_ref[...] = rot(jnp.dot(xb, wik_ref[...], preferred_element_type=F32)).astype(BF16)
    vt = lax.dot_general(wvt_ref[...], xb, NT_DIMS, preferred_element_type=F32)
    dvt_ref[...] = vt[:HEADS_W].astype(BF16)
    fvt_ref[...] = vt[HEADS_W:].astype(BF16)
    it = lax.dot_general(wit_ref[...], xb, NT_DIMS, preferred_element_type=F32)
    iwt_ref[...] = it[:N_HEADS] * _IW_SCALE
    z = jnp.dot(xb, wfl_ref[...], preferred_element_type=F32) + bfl_ref[...]
    lf = jnp.minimum(z, 0.0) - jnp.log1p(jnp.exp(-jnp.abs(z)))
    lane_w = lax.broadcasted_iota(I32, lf.shape, 1) & (LANES - 1)
    aux_ref[...] = _forget_aux(jnp.where(lane_w < _AUX_USED, lf, 0.0), lane_w, carry_sc)


def _forget_aux(lf, lane_w, carry_sc):
    tb = lf.shape[0]
    r = lax.broadcasted_iota(I32, (tb, tb), 0)
    c = lax.broadcasted_iota(I32, (tb, tb), 1)
    tri = jnp.where(c <= r, 1.0, 0.0).astype(BF16)
    hi, mid, lo = _split3(lf)
    cs = (jnp.dot(tri, lo, preferred_element_type=F32) + jnp.dot(tri, mid, preferred_element_type=F32)
          + jnp.dot(tri, hi, preferred_element_type=F32) + carry_sc[...])
    carry_sc[...] = cs[tb - 1:tb, :]
    nh, nm, nl = _split3(-LOG2E * cs)
    k = jnp.where(lane_w >= 3, lane_w - 3, lane_w)
    piece = jnp.where(k == 0, nh.astype(F32), jnp.where(k == 1, nm.astype(F32), nl.astype(F32)))
    return jnp.where(lane_w < _AUX_USED, piece, 0.0).astype(BF16)


def _prep_mix_weights(w_in, b_forget):
    sizes = (HEADS_W, HEADS_W, HEADS_W, HEADS_W, HEAD_DIM, N_HEADS, HEADS_W, HEADS_W, HEADS_W, N_HEADS, D_MODEL, D_MODEL)
    offs = [0]
    for s in sizes:
        offs.append(offs[-1] + s)
    dq, dk, dv, iq, ik, iw, fq, fk, fv, fl, ga, gb = [w_in[:, offs[i]:offs[i + 1]] for i in range(12)]
    w_main = jnp.concatenate([dq, dk, iq, fq, fk, ga, gb], axis=1).astype(BF16)
    w_ik2 = jnp.concatenate([ik, ik], axis=1).astype(BF16)
    w_vt = jnp.concatenate([dv, fv], axis=1).T.astype(BF16)
    w_it = jnp.concatenate([iw.T, jnp.zeros((16 - N_HEADS, D_MODEL), F32)], axis=0).astype(BF16)
    src = []
    for j in range(N_PAIRS):
        src += [2 * j] * 3 + [2 * j + 1] * 3 + [-1] * (LANES - _AUX_USED)
    src = jnp.array(src, I32)
    used = src >= 0
    w_fl = jnp.where(used[None, :], fl[:, jnp.maximum(src, 0)], 0.0).astype(BF16)
    b_fl = jnp.where(used, b_forget[jnp.maximum(src, 0)], 0.0)[None, :].astype(F32)
    return w_main, w_ik2, w_vt, w_it, w_fl, b_fl


def _in_projection(x, cos_t, sin_t, weights, tm):
    B, T, _ = x.shape
    w_main, w_ik2, w_vt, w_it, w_fl, b_fl = weights
    nt = T // tm

    def rows(width):
        return pl.BlockSpec((None, tm, width), lambda b, i: (b, i, 0))

    def cols(height):
        return pl.BlockSpec((None, height, tm), lambda b, i: (b, 0, i))

    def const(shape):
        return pl.BlockSpec(shape, lambda b, i: (0, 0), pipeline_mode=pl.Buffered(1))

    bt = lambda w, dt: jax.ShapeDtypeStruct((B, T, w), dt)
    tb = lambda h, dt: jax.ShapeDtypeStruct((B, h, T), dt)
    return pl.pallas_call(
        _inproj_kernel, grid=(B, nt),
        in_specs=[rows(D_MODEL), rows(LANES), rows(LANES), const(w_main.shape), const(w_ik2.shape),
                  const(w_vt.shape), const(w_it.shape), const(w_fl.shape), const(b_fl.shape)],
        out_specs=[rows(HEADS_W)] * 5 + [rows(D_MODEL)] * 2 + [rows(LANES), cols(HEADS_W), cols(HEADS_W),
                                                                cols(N_HEADS), rows(HEADS_W)],
        out_shape=[bt(HEADS_W, BF16)] * 5 + [bt(D_MODEL, BF16)] * 2 + [bt(LANES, BF16), tb(HEADS_W, BF16),
                                                                      tb(HEADS_W, BF16), tb(N_HEADS, F32),
                                                                      bt(HEADS_W, BF16)],
        scratch_shapes=[pltpu.VMEM((1, HEADS_W), F32)],
        compiler_params=_cparams("parallel", "arbitrary"), name="in_projection",
    )(x, cos_t, sin_t, w_main, w_ik2, w_vt, w_it, w_fl, b_fl)


def _stack_pair(qb):
    qf = qb.astype(F32)
    lane = lax.broadcasted_iota(I32, qf.shape, 1)
    lo = lane < HEAD_DIM
    return jnp.concatenate([jnp.where(lo, qf, 0.0), jnp.where(lo, 0.0, qf)], axis=0)


def _softmax_step(s, vb, j, m_sc, l_sc, acc_sc):
    tq = s.shape[1] // 2
    m_old = m_sc[j]
    m_new = jnp.maximum(m_old, jnp.max(s, axis=0, keepdims=True))
    alpha = jnp.exp2(m_old - m_new)
    p = jnp.exp2(s - m_new)
    l_sc[j] = alpha * l_sc[j] + jnp.sum(p, axis=0, keepdims=True)
    pb = p.astype(BF16)
    pv = jnp.concatenate([jnp.dot(vb[:HEAD_DIM], pb[:, :tq], preferred_element_type=F32),
                          jnp.dot(vb[HEAD_DIM:], pb[:, tq:], preferred_element_type=F32)], axis=1)
    acc_sc[j] = alpha * acc_sc[j] + pv
    m_sc[j] = m_new


def _init_softmax(m_sc, l_sc, acc_sc):
    m_sc[...] = jnp.full(m_sc.shape, -jnp.inf, F32)
    l_sc[...] = jnp.zeros(l_sc.shape, F32)
    acc_sc[...] = jnp.zeros(acc_sc.shape, F32)


def _finish_pair(o_ref, j, tq, l_sc, acc_sc):
    out_t = acc_sc[j] / l_sc[j]
    blk_t = jnp.concatenate([out_t[:, :tq], out_t[:, tq:]], axis=0)
    o_ref[:, LANES * j:LANES * (j + 1)] = blk_t.T.astype(BF16)


def _softmax_scratch(tq, q_width):
    return [pltpu.VMEM((N_PAIRS, q_width, 2 * tq), BF16), pltpu.VMEM((2, N_PAIRS, tq, 2 * tq), F32),
            pltpu.VMEM((N_PAIRS, 1, 2 * tq), F32), pltpu.VMEM((N_PAIRS, 1, 2 * tq), F32),
            pltpu.VMEM((N_PAIRS, HEAD_DIM, 2 * tq), F32)]


_STEPS_PER_ITER = 2


def _unrolled_loop(n, step, carry, unroll=_STEPS_PER_ITER):
    def main(h, c):
        for u in range(unroll):
            c = step(unroll * h + u, c)
        return c

    n_main = n // unroll
    carry = lax.fori_loop(0, n_main, main, carry)
    return lax.fori_loop(n_main * unroll, n, step, carry)


def _pipelined_attention(n_steps, scores, update):
    scores(0, 0)
    n_mid = n_steps - 1

    def body(h, carry):
        kv = _STEPS_PER_ITER * h
        for u in range(_STEPS_PER_ITER):
            scores(kv + u + 1, (u + 1) & 1)
            update(kv + u, u & 1, False)
        return carry

    n_main = n_mid // _STEPS_PER_ITER
    lax.fori_loop(0, n_main, body, 0)

    def tail(kv, carry):
        scores(kv + 1, (kv + 1) & 1)
        update(kv, kv & 1, False)
        return carry

    lax.fori_loop(n_main * _STEPS_PER_ITER, n_mid, tail, 0)
    update(n_steps - 1, (n_steps - 1) & 1, True)


def _fox_kernel(q_ref, k_ref, aux_ref, vt_ref, o_ref, qt_sc, s_sc, m_sc, l_sc, acc_sc, *, tq):
    qi = pl.program_id(1)
    q0 = qi * tq
    row2 = lax.broadcasted_iota(I32, (2 * tq, LANES), 0)
    lane2 = lax.broadcasted_iota(I32, (2 * tq, LANES), 1)
    sel = jnp.where(row2 < tq, jnp.where(lane2 < 3, 1.0, 0.0),
                    jnp.where(lane2 < 3, 0.0, jnp.where(lane2 < _AUX_USED, 1.0, 0.0)))
    for j in range(N_PAIRS):
        qa = jnp.concatenate([_stack_pair(q_ref[:, LANES * j:LANES * (j + 1)]), sel], axis=1)
        qt_sc[j] = qa.T.astype(BF16)
    _init_softmax(m_sc, l_sc, acc_sc)

    def scores(kv, slot):
        ks = pl.multiple_of(kv * tq, tq)
        for j in range(N_PAIRS):
            cs = slice(LANES * j, LANES * (j + 1))
            kb = jnp.concatenate([k_ref[pl.ds(ks, tq), cs], aux_ref[pl.ds(ks, tq), cs]], axis=1)
            s_sc[slot, j] = jnp.dot(kb, qt_sc[j], preferred_element_type=F32)

    def update(kv, slot, last):
        ks = pl.multiple_of(kv * tq, tq)
        if last:
            krow = ks + lax.broadcasted_iota(I32, (tq, 2 * tq), 0)
            qcol = q0 + (lax.broadcasted_iota(I32, (tq, 2 * tq), 1) & (tq - 1))
            keep = krow <= qcol
        for j in range(N_PAIRS):
            s = s_sc[slot, j]
            if last:
                s = jnp.where(keep, s, NEG)
            _softmax_step(s, vt_ref[LANES * j:LANES * (j + 1), pl.ds(ks, tq)], j, m_sc, l_sc, acc_sc)

    _pipelined_attention(qi + 1, scores, update)
    for j in range(N_PAIRS):
        _finish_pair(o_ref, j, tq, l_sc, acc_sc)


def _fox_attention(fq, fk, aux, fvt, tq):
    B, T, W = fq.shape
    qspec = pl.BlockSpec((None, tq, W), lambda b, i: (b, i, 0))
    full = pl.BlockSpec((None, T, W), lambda b, i: (b, 0, 0))
    fullt = pl.BlockSpec((None, W, T), lambda b, i: (b, 0, 0))
    return pl.pallas_call(
        functools.partial(_fox_kernel, tq=tq), grid=(B, T // tq),
        in_specs=[qspec, full, full, fullt], out_specs=qspec,
        out_shape=jax.ShapeDtypeStruct((B, T, W), BF16), scratch_shapes=_softmax_scratch(tq, 2 * LANES),
        compiler_params=_cparams("parallel", "arbitrary"), name="fox_attention")(fq, fk, aux, fvt)


_COUNT_CHAINS = 4


def _dsa_kernel(iq_ref, ik2_ref, iwt_ref, q_ref, k_ref, vt_ref, o_ref, key_sc, hi_sc, lo_sc, bias_sc, qt_sc, s_sc,
                m_sc, l_sc, acc_sc, *, tq, top_k):
    qi = pl.program_id(1)
    q0 = qi * tq
    col = lax.broadcasted_iota(I32, (1, tq), 1)
    limit = (lax.shift_right_logical(q0 + col, CHUNK.bit_length() - 1) + 1) * CHUNK
    iw = iwt_ref[...]

    for j in range(N_PAIRS):
        qt_sc[j] = _stack_pair(iq_ref[:, LANES * j:LANES * (j + 1)]).T.astype(BF16)

    def score_tile(kv, masked):
        ks = pl.multiple_of(kv * tq, tq)
        kb = ik2_ref[pl.ds(ks, tq), :]
        sc = jnp.zeros((tq, tq), F32)
        for j in range(N_PAIRS):
            s2 = jnp.dot(kb, qt_sc[j], preferred_element_type=F32)
            sc = sc + jnp.maximum(s2[:, :tq], 0.0) * iw[2 * j:2 * j + 1, :]
            sc = sc + jnp.maximum(s2[:, tq:], 0.0) * iw[2 * j + 1:2 * j + 2, :]
        bits = pltpu.bitcast(sc, I32)
        key = bits ^ ((bits >> 31) & jnp.int32(0x7FFFFFFF))
        key = jnp.where(bits == jnp.int32(INT_MIN), 0, key)
        if masked:
            krow = ks + lax.broadcasted_iota(I32, key.shape, 0)
            key = jnp.where(krow < limit, key, jnp.int32(INT_MIN))
        key_sc[pl.ds(ks, tq), :] = key
        hi_sc[pl.ds(ks, tq), :] = (key >> 16).astype(I16)
        lo_sc[pl.ds(ks, tq), :] = ((key & 0xFFFF) + I16_MIN).astype(I16)

    def score_body(kv, carry):
        score_tile(kv, False)
        return carry

    _unrolled_loop(qi, score_body, 0)
    score_tile(qi, True)

    def count(plane_sc, pred):
        def body(g, accs):
            slab = plane_sc[pl.ds(pl.multiple_of(g * tq, tq), tq), :]
            accs = list(accs)
            for r in range(tq // PACKED_ROWS):
                blk = slab[PACKED_ROWS * r:PACKED_ROWS * (r + 1), :]
                hit = jnp.where(pred(blk), jnp.int16(1), jnp.int16(0))
                accs[r % _COUNT_CHAINS] = accs[r % _COUNT_CHAINS] + hit
            return tuple(accs)

        zero = jnp.zeros((PACKED_ROWS, tq), I16)
        accs = lax.fori_loop(0, qi + 1, body, (zero,) * _COUNT_CHAINS)
        return jnp.sum(sum(a.astype(I32) for a in accs), axis=0, keepdims=True)

    def rows16(v):
        return jnp.broadcast_to(v.astype(I16), (PACKED_ROWS, tq))

    def search(plane_sc, above, stop_when_exact=False):
        def step(i, v):
            cand = v + lax.shift_left(jnp.int32(1), 15 - i)
            cand16 = rows16(cand)
            cnt = above + count(plane_sc, lambda blk: blk >= cand16)
            return jnp.where(cnt >= top_k, cand, v), cnt

        start = jnp.full((1, tq), I16_MIN, I32)
        if not stop_when_exact:
            return lax.fori_loop(0, 16, lambda i, v: step(i, v)[0], start)

        def body(state):
            i, v, settled, _ = state
            v, cnt = step(i, v)
            settled = jnp.maximum(settled, jnp.where(cnt == top_k, 1, 0))
            return i + 1, v, settled, tq - jnp.sum(settled)

        state = (jnp.int32(0), start, jnp.zeros((1, tq), I32), jnp.int32(tq))
        return lax.while_loop(lambda st: jnp.logical_and(st[0] < 16, st[3] > 0), body, state)[1]

    t_hi = search(hi_sc, 0)
    t_hi16 = rows16(t_hi)
    above_hi = count(hi_sc, lambda blk: blk > t_hi16)

    def mask_low(g, carry):
        rows = pl.ds(pl.multiple_of(g * tq, tq), tq)
        lo_sc[rows, :] = jnp.where(hi_sc[rows, :] == t_hi.astype(I16), lo_sc[rows, :], jnp.int16(I16_MIN))
        return carry

    lax.fori_loop(0, qi + 1, mask_low, 0)
    t_lo = search(lo_sc, above_hi, stop_when_exact=True)
    t_lo16 = rows16(t_lo)
    thr = t_hi * 65536 + (t_lo - I16_MIN)
    n_ge = above_hi + count(lo_sc, lambda blk: blk >= t_lo16)
    tie_walk = jnp.maximum(jnp.where(n_ge > top_k, 1, 0), jnp.where(t_lo == I16_MIN, 1, 0))
    any_tie_walk = jnp.max(tie_walk) > 0

    for j in range(N_PAIRS):
        qt_sc[j] = _stack_pair(q_ref[:, LANES * j:LANES * (j + 1)]).T.astype(BF16)
    _init_softmax(m_sc, l_sc, acc_sc)

    def scores(kv, slot):
        ks = pl.multiple_of(kv * tq, tq)
        for j in range(N_PAIRS):
            s_sc[slot, j] = jnp.dot(k_ref[pl.ds(ks, tq), LANES * j:LANES * (j + 1)], qt_sc[j],
                                    preferred_element_type=F32)

    def attend(select):
        def update(kv, slot, last):
            del last
            ks = pl.multiple_of(kv * tq, tq)
            keep_selected = select(ks)
            for j in range(N_PAIRS):
                _softmax_step(keep_selected(s_sc[slot, j]), vt_ref[LANES * j:LANES * (j + 1), pl.ds(ks, tq)], j,
                              m_sc, l_sc, acc_sc)

        _pipelined_attention(qi + 1, scores, update)

    @pl.when(jnp.logical_not(any_tie_walk))
    def _():
        def select(ks):
            b = jnp.where(key_sc[pl.ds(ks, tq), :] >= thr, 0.0, NEG)
            b2 = jnp.concatenate([b, b], axis=1)
            return lambda s: s + b2

        attend(select)

    @pl.when(any_tie_walk)
    def _():
        need = (top_k - above_hi - count(lo_sc, lambda blk: blk > t_lo16)).astype(F32)
        r = lax.broadcasted_iota(I32, (tq, tq), 0)
        c = lax.broadcasted_iota(I32, (tq, tq), 1)
        tri = jnp.where(c < r, 1.0, 0.0).astype(BF16)

        def bias_tile(kv, seen, masked):
            ks = pl.multiple_of(kv * tq, tq)
            key = key_sc[pl.ds(ks, tq), :]
            eqf = jnp.where(key == thr, 1.0, 0.0)
            before = jnp.dot(tri, eqf.astype(BF16), preferred_element_type=F32) + seen
            b = jnp.where(key > thr, 0.0, jnp.where(key == thr, jnp.where(before < need, 0.0, NEG), NEG))
            if masked:
                krow = ks + lax.broadcasted_iota(I32, key.shape, 0)
                b = jnp.where(krow < limit, b, NEG)
            bias_sc[pl.ds(ks, tq), :] = b
            return seen + jnp.sum(eqf, axis=0, keepdims=True)

        seen = _unrolled_loop(qi, lambda kv, s: bias_tile(kv, s, False), jnp.zeros((1, tq), F32))
        bias_tile(qi, seen, True)

        def select(ks):
            b = bias_sc[pl.ds(ks, tq), :]
            b2 = jnp.concatenate([b, b], axis=1)
            return lambda s: s + b2

        attend(select)

    for j in range(N_PAIRS):
        _finish_pair(o_ref, j, tq, l_sc, acc_sc)


def _dsa_attention(iq, ik2, iwt, dq, dk, dvt, tq, top_k):
    B, T, W = dq.shape
    qspec = pl.BlockSpec((None, tq, W), lambda b, i: (b, i, 0))
    full = pl.BlockSpec((None, T, W), lambda b, i: (b, 0, 0))
    fullt = pl.BlockSpec((None, W, T), lambda b, i: (b, 0, 0))
    return pl.pallas_call(
        functools.partial(_dsa_kernel, tq=tq, top_k=top_k), grid=(B, T // tq),
        in_specs=[qspec, pl.BlockSpec((None, T, LANES), lambda b, i: (b, 0, 0)),
                  pl.BlockSpec((None, N_HEADS, tq), lambda b, i: (b, 0, i)), qspec, full, fullt],
        out_specs=qspec, out_shape=jax.ShapeDtypeStruct((B, T, W), BF16),
        scratch_shapes=[pltpu.VMEM((T, tq), I32), pltpu.VMEM((T, tq), I16), pltpu.VMEM((T, tq), I16),
                        pltpu.VMEM((T, tq), F32)] + _softmax_scratch(tq, LANES),
        compiler_params=_cparams("parallel", "arbitrary"), name="dsa_attention")(iq, ik2, iwt, dq, dk, dvt)


TOKEN_ROWS = D_MODEL // LANES
assert TOKEN_ROWS == SUBLANES


def _store_token_tiles(ref, y):
    tm = y.shape[0]
    for s in range(TOKEN_ROWS):
        ref[pl.ds(s, tm, stride=TOKEN_ROWS), :] = y[:, LANES * s:LANES * (s + 1)]


def _load_token_tiles(ref, tm):
    return jnp.concatenate([ref[pl.ds(s, tm, stride=TOKEN_ROWS), :] for s in range(TOKEN_ROWS)], axis=1)


def _merge_kernel(oa_ref, ob_ref, ga_ref, gb_ref, x_ref, wa_ref, wb_ref, wo_ref, g_ref, b_ref, o_ref, o2_ref,
                  *, alpha, token_tiles):
    a = jnp.dot(oa_ref[...], wa_ref[...], preferred_element_type=F32)
    b = jnp.dot(ob_ref[...], wb_ref[...], preferred_element_type=F32)
    merged = ga_ref[...].astype(F32) * a + gb_ref[...].astype(F32) * b
    mix = jnp.dot(merged.astype(BF16), wo_ref[...], preferred_element_type=F32)
    y = _layer_norm(alpha * x_ref[...] + mix, g_ref[...], b_ref[...])
    o_ref[...] = y
    if token_tiles:
        _store_token_tiles(o2_ref, y)
    else:
        o2_ref[...] = y.astype(BF16)


def _merge_project(oa, ob, ga, gb, x, wa, wb, wo, g, b, alpha, tm, token_tiles):
    n = x.shape[0]

    def rows(w):
        return pl.BlockSpec((tm, w), lambda i: (i, 0))

    def const(shape):
        return pl.BlockSpec(shape, lambda i: (0, 0), pipeline_mode=pl.Buffered(1))

    if token_tiles:
        out2_spec = pl.BlockSpec((tm * TOKEN_ROWS, LANES), lambda i: (i, 0))
        out2_shape = jax.ShapeDtypeStruct((n * TOKEN_ROWS, LANES), F32)
    else:
        out2_spec, out2_shape = rows(D_MODEL), jax.ShapeDtypeStruct((n, D_MODEL), BF16)
    return pl.pallas_call(
        functools.partial(_merge_kernel, alpha=alpha, token_tiles=token_tiles), grid=(n // tm,),
        in_specs=[rows(HEADS_W), rows(HEADS_W), rows(D_MODEL), rows(D_MODEL), rows(D_MODEL), const(wa.shape),
                  const(wb.shape), const(wo.shape), const(g.shape), const(b.shape)],
        out_specs=[rows(D_MODEL), out2_spec], out_shape=[jax.ShapeDtypeStruct((n, D_MODEL), F32), out2_shape],
        compiler_params=_cparams("parallel"), name="merge_project")(oa, ob, ga, gb, x, wa, wb, wo, g, b)


def _swiglu_partial(xb, wg, wu, wd):
    h = jnp.dot(xb, wg, preferred_element_type=F32)
    u = jnp.dot(xb, wu, preferred_element_type=F32)
    hid = (h * jax.nn.sigmoid(h)) * u
    return jnp.dot(hid.astype(BF16), wd, preferred_element_type=F32)


def _swiglu(xb, wg_ref, wu_ref, wd_ref, tf):
    acc = None
    for f in range(D_FF // tf):
        part = _swiglu_partial(xb, wg_ref[:, f * tf:(f + 1) * tf], wu_ref[:, f * tf:(f + 1) * tf],
                               wd_ref[f * tf:(f + 1) * tf, :])
        acc = part if acc is None else acc + part
    return acc


def _ffn_kernel(xb_ref, x_ref, wg_ref, wu_ref, wd_ref, g_ref, b_ref, o_ref, *, alpha, tf):
    ff = _swiglu(xb_ref[...], wg_ref, wu_ref, wd_ref, tf)
    o_ref[...] = _layer_norm(alpha * x_ref[...] + ff, g_ref[...], b_ref[...])


def _dense_ffn(xb, x, wg, wu, wd, g, b, alpha, tm, tf):
    n = x.shape[0]
    rows = pl.BlockSpec((tm, D_MODEL), lambda i: (i, 0))

    def const(shape):
        return pl.BlockSpec(shape, lambda i: (0, 0), pipeline_mode=pl.Buffered(1))

    return pl.pallas_call(
        functools.partial(_ffn_kernel, alpha=alpha, tf=tf), grid=(n // tm,),
        in_specs=[rows, rows, const(wg.shape), const(wu.shape), const(wd.shape), const(g.shape), const(b.shape)],
        out_specs=rows, out_shape=jax.ShapeDtypeStruct((n, D_MODEL), F32),
        compiler_params=_cparams("parallel"), name="dense_ffn")(xb, x, wg, wu, wd, g, b)


def _router_kernel(x_ref, r_ref, idx_ref, gate_ref, cnt_ref, seen_sc):
    i = pl.program_id(0)

    @pl.when(i == 0)
    def _():
        seen_sc[...] = jnp.zeros_like(seen_sc)

    xs = _split3(x_ref[...])
    logits = jnp.zeros((x_ref.shape[0], LANES), F32)
    for a, b in ((1, 0), (0, 1), (0, 0)):
        logits = logits + jnp.dot(xs[a], r_ref[b], preferred_element_type=F32)
    lane = lax.broadcasted_iota(I32, logits.shape, 1)
    lg = jnp.where(lane < N_EXPERTS, logits, -jnp.inf)
    v1 = jnp.max(lg, axis=-1, keepdims=True)
    i1 = jnp.min(jnp.where(lg == v1, lane, LANES), axis=-1, keepdims=True)
    lg2 = jnp.where(lane == i1, -jnp.inf, lg)
    v2 = jnp.max(lg2, axis=-1, keepdims=True)
    i2 = jnp.min(jnp.where(lg2 == v2, lane, LANES), axis=-1, keepdims=True)
    e = jnp.exp(v2 - v1)
    g1 = 1.0 / (1.0 + e)
    g2 = e / (1.0 + e)
    hit1 = lane == i1
    hit2 = lane == i2
    cnt = jnp.where(hit1, 1.0, jnp.where(hit2, 1.0, 0.0))
    tm = cnt.shape[0]
    r = lax.broadcasted_iota(I32, (tm, tm), 0)
    c = lax.broadcasted_iota(I32, (tm, tm), 1)
    tri = jnp.where(c < r, 1.0, 0.0).astype(BF16)
    before = jnp.dot(tri, cnt.astype(BF16), preferred_element_type=F32) + seen_sc[...]
    rank1 = jnp.sum(jnp.where(hit1, before, 0.0), axis=-1, keepdims=True).astype(I32)
    rank2 = jnp.sum(jnp.where(hit2, before, 0.0), axis=-1, keepdims=True).astype(I32)
    seen = seen_sc[...] + jnp.sum(cnt, axis=0, keepdims=True)
    seen_sc[...] = seen
    idx_ref[...] = jnp.where(lane == 0, i1, jnp.where(lane == 1, i2, jnp.where(lane == 2, rank1,
                                                                                  jnp.where(lane == 3, rank2, 0))))
    gate_ref[...] = jnp.where(lane == 0, g1, jnp.where(lane == 1, g2, 0.0))
    cnt_ref[...] = jnp.broadcast_to(seen, cnt_ref.shape).astype(I32)


def _route(x, router, tm):
    n = x.shape[0]
    r_pad = jnp.concatenate([router, jnp.zeros((D_MODEL, LANES - N_EXPERTS), F32)], axis=1)
    r3 = jnp.stack(_split3(r_pad))
    rows = pl.BlockSpec((tm, LANES), lambda i: (i, 0))
    return pl.pallas_call(
        _router_kernel, grid=(n // tm,),
        in_specs=[pl.BlockSpec((tm, D_MODEL), lambda i: (i, 0)), pl.BlockSpec(r3.shape, lambda i: (0, 0, 0))],
        out_specs=[rows, rows, pl.BlockSpec((SUBLANES, LANES), lambda i: (0, 0))],
        out_shape=[jax.ShapeDtypeStruct((n, LANES), I32), jax.ShapeDtypeStruct((n, LANES), F32),
                   jax.ShapeDtypeStruct((SUBLANES, LANES), I32)],
        scratch_shapes=[pltpu.VMEM((1, LANES), F32)],
        compiler_params=_cparams("arbitrary"), name="moe_router")(x, r3)


def _tile_rows(idx):
    return pl.ds(pl.multiple_of(idx * TOKEN_ROWS, TOKEN_ROWS), TOKEN_ROWS)


def _dispatch_kernel(slot_ref, x_ref, init_ref, xs_ref, sem, *, tm):
    del init_ref
    base = pl.program_id(0) * tm

    def start(r, carry):
        src = x_ref.at[_tile_rows(r), :]
        for j in range(2):
            pltpu.make_async_copy(src, xs_ref.at[_tile_rows(slot_ref[2 * (base + r) + j]), :], sem).start()
        return carry

    lax.fori_loop(0, tm, start, 0, unroll=8)
    for _ in range(2):
        pltpu.make_async_copy(x_ref, xs_ref.at[pl.ds(0, tm * TOKEN_ROWS), :], sem).wait()


def _dispatch(slots_flat, xt, n_slots, tm):
    n = xt.shape[0] // TOKEN_ROWS
    any_spec = pl.BlockSpec(memory_space=pl.ANY)
    gs = pltpu.PrefetchScalarGridSpec(
        num_scalar_prefetch=1, grid=(n // tm,),
        in_specs=[pl.BlockSpec((tm * TOKEN_ROWS, LANES), lambda i, slots: (i, 0)), any_spec],
        out_specs=any_spec, scratch_shapes=[pltpu.SemaphoreType.DMA(())])
    return pl.pallas_call(
        functools.partial(_dispatch_kernel, tm=tm), grid_spec=gs,
        out_shape=jax.ShapeDtypeStruct((n_slots * TOKEN_ROWS, LANES), F32), input_output_aliases={2: 0},
        compiler_params=_cparams("arbitrary"), name="moe_dispatch",
    )(slots_flat, xt, jnp.zeros((n_slots * TOKEN_ROWS, LANES), F32))


def _moe_ffn_kernel(be_ref, nv_ref, xs_ref, wg_ref, wu_ref, wd_ref, y_ref, *, tm, tf):
    del be_ref
    live = pl.program_id(0) < nv_ref[0]

    @pl.when(live)
    def _():
        xb = _load_token_tiles(xs_ref, tm).astype(BF16)
        _store_token_tiles(y_ref, _swiglu(xb, wg_ref, wu_ref, wd_ref, tf))

    @pl.when(jnp.logical_not(live))
    def _():
        y_ref[...] = jnp.zeros_like(y_ref)


def _moe_ffn(block_expert, n_valid, xs, wg, wu, wd, tm, tf):
    p = xs.shape[0] // TOKEN_ROWS
    tiles = pl.BlockSpec((tm * TOKEN_ROWS, LANES), lambda i, be, nv: (i, 0))

    def expert(shape):
        return pl.BlockSpec((None,) + shape[1:], lambda i, be, nv: (be[i], 0, 0), pipeline_mode=pl.Buffered(1))

    gs = pltpu.PrefetchScalarGridSpec(
        num_scalar_prefetch=2, grid=(p // tm,),
        in_specs=[tiles, expert(wg.shape), expert(wu.shape), expert(wd.shape)], out_specs=tiles)
    return pl.pallas_call(
        functools.partial(_moe_ffn_kernel, tm=tm, tf=tf), grid_spec=gs, out_shape=jax.ShapeDtypeStruct(xs.shape, F32),
        compiler_params=_cparams("arbitrary"), name="moe_ffn")(block_expert, n_valid, xs, wg, wu, wd)


def _combine_kernel(slot_ref, x_ref, gate_ref, g_ref, b_ref, y_ref, o_ref, ybuf, sem, *, alpha, tm):
    i = pl.program_id(0)

    def gather(step, par):
        base = step * tm

        def start(r, carry):
            for j in range(2):
                src = y_ref.at[_tile_rows(slot_ref[2 * (base + r) + j]), :]
                pltpu.make_async_copy(src, ybuf.at[par, j, _tile_rows(r), :], sem.at[par]).start()
            return carry

        lax.fori_loop(0, tm, start, 0, unroll=8)

    @pl.when(i == 0)
    def _():
        gather(0, 0)

    @pl.when(i + 1 < pl.num_programs(0))
    def _():
        gather(i + 1, (i + 1) & 1)

    par = i & 1
    for j in range(2):
        pltpu.make_async_copy(y_ref.at[pl.ds(0, tm * TOKEN_ROWS), :], ybuf.at[par, j], sem.at[par]).wait()
    gt = gate_ref[...]
    ff = gt[:, 0:1] * _load_token_tiles(ybuf.at[par, 0], tm) + gt[:, 1:2] * _load_token_tiles(ybuf.at[par, 1], tm)
    o_ref[...] = _layer_norm(alpha * x_ref[...] + ff, g_ref[...], b_ref[...])


def _combine(slots_flat, x, gates, g, b, y, alpha, tm):
    n = x.shape[0]
    rows = pl.BlockSpec((tm, D_MODEL), lambda i, slots: (i, 0))
    vec = pl.BlockSpec((1, D_MODEL), lambda i, slots: (0, 0))
    gs = pltpu.PrefetchScalarGridSpec(
        num_scalar_prefetch=1, grid=(n // tm,),
        in_specs=[rows, pl.BlockSpec((tm, LANES), lambda i, slots: (i, 0)), vec, vec,
                  pl.BlockSpec(memory_space=pl.ANY)],
        out_specs=rows, scratch_shapes=[pltpu.VMEM((2, 2, tm * TOKEN_ROWS, LANES), F32),
                                        pltpu.SemaphoreType.DMA((2,))])
    return pl.pallas_call(
        functools.partial(_combine_kernel, alpha=alpha, tm=tm), grid_spec=gs,
        out_shape=jax.ShapeDtypeStruct((n, D_MODEL), F32), compiler_params=_cparams("arbitrary"),
        name="moe_combine")(slots_flat, x, gates, g, b, y)


def _moe_layer(x, xt, router, wg, wu, wd, g, b, alpha, tiles):
    n = x.shape[0]
    tmb = tiles["moe_rows"]
    idx, gates, counts = _route(x, router, tiles["route_rows"])
    counts = counts[0, :N_EXPERTS]
    padded = ((counts + tmb - 1) // tmb) * tmb
    pad_end = jnp.cumsum(padded)
    pad_start = pad_end - padded
    slots = jnp.stack([pad_start[idx[:, 0]] + idx[:, 2], pad_start[idx[:, 1]] + idx[:, 3]], axis=1)
    slots_flat = slots.reshape(-1).astype(I32)
    n_slots = 2 * n + N_EXPERTS * tmb
    n_blocks = n_slots // tmb
    block_expert = jnp.clip(jnp.searchsorted(pad_end, jnp.arange(n_blocks, dtype=I32) * tmb, side="right"),
                            0, N_EXPERTS - 1).astype(I32)
    n_valid = (pad_end[-1:] // tmb).astype(I32)
    xs = _dispatch(slots_flat, xt, n_slots, tiles["dma_rows"])
    y = _moe_ffn(block_expert, n_valid, xs, wg, wu, wd, tmb, tiles["ffn_cols"])
    return _combine(slots_flat, x, gates, g, b, y, alpha, tiles["dma_rows"])


def _tiles(n, t):
    return dict(
        proj_rows=min(256, t), attn_q=min(256, t), row_tile=min(512, n),
        ffn_cols=512, route_rows=min(256, n), moe_rows=min(512, n), dma_rows=min(256, n))


def kernel(x, positions, w_in, b_forget, w_branch_a, w_branch_b, w_out, ln_mix_g, ln_mix_b, ln_ffn_g, ln_ffn_b,
           ffn_w_gate, ffn_w_up, ffn_w_down, moe_router, moe_w_gate, moe_w_up, moe_w_down):
    B, T, D = x.shape
    assert D == D_MODEL and T % CHUNK == 0
    n = B * T
    depth = w_in.shape[0]
    top_k = min(TOPK_MAX, T // 4)
    alpha = (2.0 * depth) ** 0.25
    tiles = _tiles(n, T)
    cos_t, sin_t = _rope_tables(positions)
    xf = x.reshape(n, D)
    for layer in range(depth):
        weights = _prep_mix_weights(w_in[layer], b_forget[layer])
        (dq, dk, iq, fq, fk, ga, gb, ik2, dvt, fvt, iwt, aux) = _in_projection(
            xf.reshape(B, T, D), cos_t, sin_t, weights, tiles["proj_rows"])
        o_a = _dsa_attention(iq, ik2, iwt, dq, dk, dvt, tiles["attn_q"], top_k)
        o_b = _fox_attention(fq, fk, aux, fvt, tiles["attn_q"])
        dense = layer % 2 == 0
        x1, x1_alt = _merge_project(
            o_a.reshape(n, HEADS_W), o_b.reshape(n, HEADS_W), ga.reshape(n, D), gb.reshape(n, D), xf,
            w_branch_a[layer].astype(BF16), w_branch_b[layer].astype(BF16), w_out[layer].astype(BF16),
            ln_mix_g[layer][None, :], ln_mix_b[layer][None, :], alpha, tiles["row_tile"], token_tiles=not dense)
        j = layer // 2
        g, b = ln_ffn_g[layer][None, :], ln_ffn_b[layer][None, :]
        if dense:
            xf = _dense_ffn(x1_alt, x1, ffn_w_gate[j].astype(BF16), ffn_w_up[j].astype(BF16),
                            ffn_w_down[j].astype(BF16), g, b, alpha, tiles["row_tile"], tiles["ffn_cols"])
        else:
            xf = _moe_layer(x1, x1_alt, moe_router[j], moe_w_gate[j].astype(BF16), moe_w_up[j].astype(BF16),
                            moe_w_down[j].astype(BF16), g, b, alpha, tiles)
    return xf.reshape(B, T, D)
```

```python
import functools

import jax
import jax.numpy as jnp
from jax import lax
from jax.experimental import pallas as pl
from jax.experimental.pallas import tpu as pltpu

F32, BF16, I32, I16 = jnp.float32, jnp.bfloat16, jnp.int32, jnp.int16

D_MODEL = 1024
HEAD_DIM = 64
N_HEADS = 8
HEADS_W = N_HEADS * HEAD_DIM
N_PAIRS = N_HEADS // 2
CHUNK = 64
TOPK_MAX = 256
ROPE_THETA = 10000.0
D_FF = 3584
N_EXPERTS = 8
LN_EPS = 1e-5
DEPTH = 2
LANES = 128
SUBLANES = 8
PACKED_ROWS = 2 * SUBLANES
INT_MIN = -(2**31)
I16_MIN = -(2**15)
NEG = -1e30
VMEM_LIMIT = 56 * 1024 * 1024
NT_DIMS = (((1,), (1,)), ((), ()))


def _cparams(*sem):
    return pltpu.CompilerParams(dimension_semantics=sem, vmem_limit_bytes=VMEM_LIMIT)


def _split3(v):
    hi = v.astype(BF16)
    r1 = v - hi.astype(F32)
    mid = r1.astype(BF16)
    lo = (r1 - mid.astype(F32)).astype(BF16)
    return hi, mid, lo


def _layer_norm(y, g, b):
    mu = jnp.mean(y, axis=-1, keepdims=True)
    yc = y - mu
    var = jnp.mean(yc * yc, axis=-1, keepdims=True)
    return yc * lax.rsqrt(var + LN_EPS) * g + b


def _rope_kernel(pos_ref, freq_ref, sign_ref, cos_ref, sin_ref):
    ang = pos_ref[...].astype(F32) * freq_ref[...]
    cos_ref[...] = jnp.cos(ang)
    sin_ref[...] = jnp.sin(ang) * sign_ref[...]


def _rope_tables(positions):
    B, T = positions.shape
    n = B * T
    inv_freq = ROPE_THETA ** (-jnp.arange(0, HEAD_DIM, 2, dtype=F32) / HEAD_DIM)
    freq = jnp.tile(inv_freq, LANES // (HEAD_DIM // 2))[None, :]
    half = HEAD_DIM // 2
    sign = jnp.tile(jnp.concatenate([-jnp.ones((half,), F32), jnp.ones((half,), F32)]), LANES // HEAD_DIM)[None, :]
    pos_b = jnp.broadcast_to(positions.reshape(n, 1), (n, LANES))
    tm = min(n, 1024)
    row = pl.BlockSpec((tm, LANES), lambda i: (i, 0))
    const = pl.BlockSpec((1, LANES), lambda i: (0, 0))
    cos, sin = pl.pallas_call(
        _rope_kernel, grid=(n // tm,), in_specs=[row, const, const], out_specs=[row, row],
        out_shape=[jax.ShapeDtypeStruct((n, LANES), F32)] * 2, compiler_params=_cparams("parallel"),
        name="rope_tables")(pos_b, freq, sign)
    return cos.reshape(B, T, LANES), sin.reshape(B, T, LANES)


LOG2E = 1.4426950408889634
_Q_SCALE = HEAD_DIM ** -0.5 * LOG2E
_IW_SCALE = (N_HEADS ** -0.5) * (HEAD_DIM ** -0.5)
_OFF_DQ, _OFF_DK, _OFF_IQ, _OFF_FQ, _OFF_FK, _OFF_GA, _OFF_GB = 0, 512, 1024, 1536, 2048, 2560, 3584
_W_MAIN = 4608
_AUX_USED = 6


def _inproj_kernel(x_ref, cos_ref, sin_ref, wm_ref, wik_ref, wvt_ref, wit_ref, wfl_ref, bfl_ref,
                   dq_ref, dk_ref, iq_ref, fq_ref, fk_ref, ga_ref, gb_ref, ik2_ref, dvt_ref, fvt_ref,
                   iwt_ref, aux_ref, carry_sc):
    @pl.when(pl.program_id(1) == 0)
    def _():
        carry_sc[...] = jnp.zeros_like(carry_sc)

    xb = x_ref[...].astype(BF16)
    cos = cos_ref[...]
    sin = sin_ref[...]
    lane = lax.broadcasted_iota(I32, cos.shape, 1)
    first_half = (lane & (HEAD_DIM - 1)) < HEAD_DIM // 2

    def rot(y):
        partner = jnp.where(first_half, pltpu.roll(y, LANES - HEAD_DIM // 2, 1), pltpu.roll(y, HEAD_DIM // 2, 1))
        return y * cos + partner * sin

    def proj(off, width):
        return jnp.dot(xb, wm_ref[:, off:off + width], preferred_element_type=F32)

    def rot_group(off, scale, out_ref):
        y = proj(off, HEADS_W)
        for j in range(N_PAIRS):
            r = rot(y[:, LANES * j:LANES * (j + 1)])
            if scale != 1.0:
                r = r * scale
            out_ref[:, LANES * j:LANES * (j + 1)] = r.astype(BF16)

    rot_group(_OFF_DQ, _Q_SCALE, dq_ref)
    rot_group(_OFF_DK, 1.0, dk_ref)
    rot_group(_OFF_IQ, 1.0, iq_ref)
    fq_ref[...] = (proj(_OFF_FQ, HEADS_W) * _Q_SCALE).astype(BF16)
    fk_ref[...] = proj(_OFF_FK, HEADS_W).astype(BF16)
    ga_ref[...] = jax.nn.sigmoid(proj(_OFF_GA, D_MODEL)).astype(BF16)
    gb_ref[...] = jax.nn.sigmoid(proj(_OFF_GB, D_MODEL)).astype(BF16)
    ik2_ref[...] = rot(jnp.dot(xb, wik_ref[...], preferred_element_type=F32)).astype(BF16)
    vt = lax.dot_general(wvt_ref[...], xb, NT_DIMS, preferred_element_type=F32)
    dvt_ref[...] = vt[:HEADS_W].astype(BF16)
    fvt_ref[...] = vt[HEADS_W:].astype(BF16)
    it = lax.dot_general(wit_ref[...], xb, NT_DIMS, preferred_element_type=F32)
    iwt_ref[...] = it[:N_HEADS] * _IW_SCALE
    z = jnp.dot(xb, wfl_ref[...], preferred_element_type=F32) + bfl_ref[...]
    lf = jnp.minimum(z, 0.0) - jnp.log1p(jnp.exp(-jnp.abs(z)))
    lane_w = lax.broadcasted_iota(I32, lf.shape, 1) & (LANES - 1)
    aux_ref[...] = _forget_aux(jnp.where(lane_w < _AUX_USED, lf, 0.0), lane_w, carry_sc)


def _forget_aux(lf, lane_w, carry_sc):
    tb = lf.shape[0]
    r = lax.broadcasted_iota(I32, (tb, tb), 0)
    c = lax.broadcasted_iota(I32, (tb, tb), 1)
    tri = jnp.where(c <= r, 1.0, 0.0).astype(BF16)
    hi, mid, lo = _split3(lf)
    cs = (jnp.dot(tri, lo, preferred_element_type=F32) + jnp.dot(tri, mid, preferred_element_type=F32)
          + jnp.dot(tri, hi, preferred_element_type=F32) + carry_sc[...])
    carry_sc[...] = cs[tb - 1:tb, :]
    nh, nm, nl = _split3(-LOG2E * cs)
    k = jnp.where(lane_w >= 3, lane_w - 3, lane_w)
    piece = jnp.where(k == 0, nh.astype(F32), jnp.where(k == 1, nm.astype(F32), nl.astype(F32)))
    return jnp.where(lane_w < _AUX_USED, piece, 0.0).astype(BF16)


def _prep_mix_weights(w_in, b_forget):
    sizes = (HEADS_W, HEADS_W, HEADS_W, HEADS_W, HEAD_DIM, N_HEADS, HEADS_W, HEADS_W, HEADS_W, N_HEADS, D_MODEL, D_MODEL)
    offs = [0]
    for s in sizes:
        offs.append(offs[-1] + s)
    dq, dk, dv, iq, ik, iw, fq, fk, fv, fl, ga, gb = [w_in[:, offs[i]:offs[i + 1]] for i in range(12)]
    w_main = jnp.concatenate([dq, dk, iq, fq, fk, ga, gb], axis=1).astype(BF16)
    w_ik2 = jnp.concatenate([ik, ik], axis=1).astype(BF16)
    w_vt = jnp.concatenate([dv, fv], axis=1).T.astype(BF16)
    w_it = jnp.concatenate([iw.T, jnp.zeros((16 - N_HEADS, D_MODEL), F32)], axis=0).astype(BF16)
    src = []
    for j in range(N_PAIRS):
        src += [2 * j] * 3 + [2 * j + 1] * 3 + [-1] * (LANES - _AUX_USED)
    src = jnp.array(src, I32)
    used = src >= 0
    w_fl = jnp.where(used[None, :], fl[:, jnp.maximum(src, 0)], 0.0).astype(BF16)
    b_fl = jnp.where(used, b_forget[jnp.maximum(src, 0)], 0.0)[None, :].astype(F32)
    return w_main, w_ik2, w_vt, w_it, w_fl, b_fl


def _in_projection(x, cos_t, sin_t, weights, tm):
    B, T, _ = x.shape
    w_main, w_ik2, w_vt, w_it, w_fl, b_fl = weights
    nt = T // tm

    def rows(width):
        return pl.BlockSpec((None, tm, width), lambda b, i: (b, i, 0))

    def cols(height):
        return pl.BlockSpec((None, height, tm), lambda b, i: (b, 0, i))

    def const(shape):
        return pl.BlockSpec(shape, lambda b, i: (0, 0), pipeline_mode=pl.Buffered(1))

    bt = lambda w, dt: jax.ShapeDtypeStruct((B, T, w), dt)
    tb = lambda h, dt: jax.ShapeDtypeStruct((B, h, T), dt)
    return pl.pallas_call(
        _inproj_kernel, grid=(B, nt),
        in_specs=[rows(D_MODEL), rows(LANES), rows(LANES), const(w_main.shape), const(w_ik2.shape),
                  const(w_vt.shape), const(w_it.shape), const(w_fl.shape), const(b_fl.shape)],
        out_specs=[rows(HEADS_W)] * 5 + [rows(D_MODEL)] * 2 + [rows(LANES), cols(HEADS_W), cols(HEADS_W),
                                                                cols(N_HEADS), rows(HEADS_W)],
        out_shape=[bt(HEADS_W, BF16)] * 5 + [bt(D_MODEL, BF16)] * 2 + [bt(LANES, BF16), tb(HEADS_W, BF16),
                                                                      tb(HEADS_W, BF16), tb(N_HEADS, F32),
                                                                      bt(HEADS_W, BF16)],
        scratch_shapes=[pltpu.VMEM((1, HEADS_W), F32)],
        compiler_params=_cparams("parallel", "arbitrary"), name="in_projection",
    )(x, cos_t, sin_t, w_main, w_ik2, w_vt, w_it, w_fl, b_fl)


def _stack_pair(qb):
    qf = qb.astype(F32)
    lane = lax.broadcasted_iota(I32, qf.shape, 1)
    lo = lane < HEAD_DIM
    return jnp.concatenate([jnp.where(lo, qf, 0.0), jnp.where(lo, 0.0, qf)], axis=0)


def _softmax_step(s, vb, j, m_sc, l_sc, acc_sc):
    tq = s.shape[1] // 2
    m_old = m_sc[j]
    m_new = jnp.maximum(m_old, jnp.max(s, axis=0, keepdims=True))
    alpha = jnp.exp2(m_old - m_new)
    p = jnp.exp2(s - m_new)
    l_sc[j] = alpha * l_sc[j] + jnp.sum(p, axis=0, keepdims=True)
    pb = p.astype(BF16)
    pv = jnp.concatenate([jnp.dot(vb[:HEAD_DIM], pb[:, :tq], preferred_element_type=F32),
                          jnp.dot(vb[HEAD_DIM:], pb[:, tq:], preferred_element_type=F32)], axis=1)
    acc_sc[j] = alpha * acc_sc[j] + pv
    m_sc[j] = m_new


def _init_softmax(m_sc, l_sc, acc_sc):
    m_sc[...] = jnp.full(m_sc.shape, -jnp.inf, F32)
    l_sc[...] = jnp.zeros(l_sc.shape, F32)
    acc_sc[...] = jnp.zeros(acc_sc.shape, F32)


def _finish_pair(o_ref, j, tq, l_sc, acc_sc):
    out_t = acc_sc[j] / l_sc[j]
    blk_t = jnp.concatenate([out_t[:, :tq], out_t[:, tq:]], axis=0)
    o_ref[:, LANES * j:LANES * (j + 1)] = blk_t.T.astype(BF16)


def _softmax_scratch(tq, q_width):
    return [pltpu.VMEM((N_PAIRS, q_width, 2 * tq), BF16), pltpu.VMEM((2, N_PAIRS, tq, 2 * tq), F32),
            pltpu.VMEM((N_PAIRS, 1, 2 * tq), F32), pltpu.VMEM((N_PAIRS, 1, 2 * tq), F32),
            pltpu.VMEM((N_PAIRS, HEAD_DIM, 2 * tq), F32)]


_STEPS_PER_ITER = 2


def _unrolled_loop(n, step, carry, unroll=_STEPS_PER_ITER):
    def main(h, c):
        for u in range(unroll):
            c = step(unroll * h + u, c)
        return c

    n_main = n // unroll
    carry = lax.fori_loop(0, n_main, main, carry)
    return lax.fori_loop(n_main * unroll, n, step, carry)


def _pipelined_attention(n_steps, scores, update):
    scores(0, 0)
    n_mid = n_steps - 1

    def body(h, carry):
        kv = _STEPS_PER_ITER * h
        for u in range(_STEPS_PER_ITER):
            scores(kv + u + 1, (u + 1) & 1)
            update(kv + u, u & 1, False)
        return carry

    n_main = n_mid // _STEPS_PER_ITER
    lax.fori_loop(0, n_main, body, 0)

    def tail(kv, carry):
        scores(kv + 1, (kv + 1) & 1)
        update(kv, kv & 1, False)
        return carry

    lax.fori_loop(n_main * _STEPS_PER_ITER, n_mid, tail, 0)
    update(n_steps - 1, (n_steps - 1) & 1, True)


def _fox_kernel(q_ref, k_ref, aux_ref, vt_ref, o_ref, qt_sc, s_sc, m_sc, l_sc, acc_sc, *, tq):
    qi = pl.program_id(1)
    q0 = qi * tq
    row2 = lax.broadcasted_iota(I32, (2 * tq, LANES), 0)
    lane2 = lax.broadcasted_iota(I32, (2 * tq, LANES), 1)
    sel = jnp.where(row2 < tq, jnp.where(lane2 < 3, 1.0, 0.0),
                    jnp.where(lane2 < 3, 0.0, jnp.where(lane2 < _AUX_USED, 1.0, 0.0)))
    for j in range(N_PAIRS):
        qa = jnp.concatenate([_stack_pair(q_ref[:, LANES * j:LANES * (j + 1)]), sel], axis=1)
        qt_sc[j] = qa.T.astype(BF16)
    _init_softmax(m_sc, l_sc, acc_sc)

    def scores(kv, slot):
        ks = pl.multiple_of(kv * tq, tq)
        for j in range(N_PAIRS):
            cs = slice(LANES * j, LANES * (j + 1))
            kb = jnp.concatenate([k_ref[pl.ds(ks, tq), cs], aux_ref[pl.ds(ks, tq), cs]], axis=1)
            s_sc[slot, j] = jnp.dot(kb, qt_sc[j], preferred_element_type=F32)

    def update(kv, slot, last):
        ks = pl.multiple_of(kv * tq, tq)
        if last:
            krow = ks + lax.broadcasted_iota(I32, (tq, 2 * tq), 0)
            qcol = q0 + (lax.broadcasted_iota(I32, (tq, 2 * tq), 1) & (tq - 1))
            keep = krow <= qcol
        for j in range(N_PAIRS):
            s = s_sc[slot, j]
            if last:
                s = jnp.where(keep, s, NEG)
            _softmax_step(s, vt_ref[LANES * j:LANES * (j + 1), pl.ds(ks, tq)], j, m_sc, l_sc, acc_sc)

    _pipelined_attention(qi + 1, scores, update)
    for j in range(N_PAIRS):
        _finish_pair(o_ref, j, tq, l_sc, acc_sc)


def _fox_attention(fq, fk, aux, fvt, tq):
    B, T, W = fq.shape
    qspec = pl.BlockSpec((None, tq, W), lambda b, i: (b, i, 0))
    full = pl.BlockSpec((None, T, W), lambda b, i: (b, 0, 0))
    fullt = pl.BlockSpec((None, W, T), lambda b, i: (b, 0, 0))
    return pl.pallas_call(
        functools.partial(_fox_kernel, tq=tq), grid=(B, T // tq),
        in_specs=[qspec, full, full, fullt], out_specs=qspec,
        out_shape=jax.ShapeDtypeStruct((B, T, W), BF16), scratch_shapes=_softmax_scratch(tq, 2 * LANES),
        compiler_params=_cparams("parallel", "arbitrary"), name="fox_attention")(fq, fk, aux, fvt)


_COUNT_CHAINS = 4


def _dsa_kernel(iq_ref, ik2_ref, iwt_ref, q_ref, k_ref, vt_ref, o_ref, key_sc, hi_sc, lo_sc, bias_sc, qt_sc, s_sc,
                m_sc, l_sc, acc_sc, *, tq, top_k):
    qi = pl.program_id(1)
    q0 = qi * tq
    col = lax.broadcasted_iota(I32, (1, tq), 1)
    limit = (lax.shift_right_logical(q0 + col, CHUNK.bit_length() - 1) + 1) * CHUNK
    iw = iwt_ref[...]

    for j in range(N_PAIRS):
        qt_sc[j] = _stack_pair(iq_ref[:, LANES * j:LANES * (j + 1)]).T.astype(BF16)

    def score_tile(kv, masked):
        ks = pl.multiple_of(kv * tq, tq)
        kb = ik2_ref[pl.ds(ks, tq), :]
        sc = jnp.zeros((tq, tq), F32)
        for j in range(N_PAIRS):
            s2 = jnp.dot(kb, qt_sc[j], preferred_element_type=F32)
            sc = sc + jnp.maximum(s2[:, :tq], 0.0) * iw[2 * j:2 * j + 1, :]
            sc = sc + jnp.maximum(s2[:, tq:], 0.0) * iw[2 * j + 1:2 * j + 2, :]
        bits = pltpu.bitcast(sc, I32)
        key = bits ^ ((bits >> 31) & jnp.int32(0x7FFFFFFF))
        key = jnp.where(bits == jnp.int32(INT_MIN), 0, key)
        if masked:
            krow = ks + lax.broadcasted_iota(I32, key.shape, 0)
            key = jnp.where(krow < limit, key, jnp.int32(INT_MIN))
        key_sc[pl.ds(ks, tq), :] = key
        hi_sc[pl.ds(ks, tq), :] = (key >> 16).astype(I16)
        lo_sc[pl.ds(ks, tq), :] = ((key & 0xFFFF) + I16_MIN).astype(I16)

    def score_body(kv, carry):
        score_tile(kv, False)
        return carry

    _unrolled_loop(qi, score_body, 0)
    score_tile(qi, True)

    def count(plane_sc, pred):
        def body(g, accs):
            slab = plane_sc[pl.ds(pl.multiple_of(g * tq, tq), tq), :]
            accs = list(accs)
            for r in range(tq // PACKED_ROWS):
                blk = slab[PACKED_ROWS * r:PACKED_ROWS * (r + 1), :]
                hit = jnp.where(pred(blk), jnp.int16(1), jnp.int16(0))
                accs[r % _COUNT_CHAINS] = accs[r % _COUNT_CHAINS] + hit
            return tuple(accs)

        zero = jnp.zeros((PACKED_ROWS, tq), I16)
        accs = lax.fori_loop(0, qi + 1, body, (zero,) * _COUNT_CHAINS)
        return jnp.sum(sum(a.astype(I32) for a in accs), axis=0, keepdims=True)

    def rows16(v):
        return jnp.broadcast_to(v.astype(I16), (PACKED_ROWS, tq))

    def search(plane_sc, above):
        def body(i, v):
            cand = v + lax.shift_left(jnp.int32(1), 15 - i)
            cand16 = rows16(cand)
            cnt = above + count(plane_sc, lambda blk: blk >= cand16)
            return jnp.where(cnt >= top_k, cand, v)

        return lax.fori_loop(0, 16, body, jnp.full((1, tq), I16_MIN, I32))

    t_hi = search(hi_sc, 0)
    t_hi16 = rows16(t_hi)
    above_hi = count(hi_sc, lambda blk: blk > t_hi16)

    def mask_low(g, carry):
        rows = pl.ds(pl.multiple_of(g * tq, tq), tq)
        lo_sc[rows, :] = jnp.where(hi_sc[rows, :] == t_hi.astype(I16), lo_sc[rows, :], jnp.int16(I16_MIN))
        return carry

    lax.fori_loop(0, qi + 1, mask_low, 0)
    t_lo = search(lo_sc, above_hi)
    t_lo16 = rows16(t_lo)
    thr = t_hi * 65536 + (t_lo - I16_MIN)
    n_ge = above_hi + count(lo_sc, lambda blk: blk >= t_lo16)
    tie_walk = jnp.maximum(jnp.where(n_ge > top_k, 1, 0), jnp.where(t_lo == I16_MIN, 1, 0))
    any_tie_walk = jnp.max(tie_walk) > 0

    for j in range(N_PAIRS):
        qt_sc[j] = _stack_pair(q_ref[:, LANES * j:LANES * (j + 1)]).T.astype(BF16)
    _init_softmax(m_sc, l_sc, acc_sc)

    def scores(kv, slot):
        ks = pl.multiple_of(kv * tq, tq)
        for j in range(N_PAIRS):
            s_sc[slot, j] = jnp.dot(k_ref[pl.ds(ks, tq), LANES * j:LANES * (j + 1)], qt_sc[j],
                                    preferred_element_type=F32)

    def attend(select):
        def update(kv, slot, last):
            del last
            ks = pl.multiple_of(kv * tq, tq)
            keep_selected = select(ks)
            for j in range(N_PAIRS):
                _softmax_step(keep_selected(s_sc[slot, j]), vt_ref[LANES * j:LANES * (j + 1), pl.ds(ks, tq)], j,
                              m_sc, l_sc, acc_sc)

        _pipelined_attention(qi + 1, scores, update)

    @pl.when(jnp.logical_not(any_tie_walk))
    def _():
        def select(ks):
            b = jnp.where(key_sc[pl.ds(ks, tq), :] >= thr, 0.0, NEG)
            b2 = jnp.concatenate([b, b], axis=1)
            return lambda s: s + b2

        attend(select)

    @pl.when(any_tie_walk)
    def _():
        need = (top_k - above_hi - count(lo_sc, lambda blk: blk > t_lo16)).astype(F32)
        r = lax.broadcasted_iota(I32, (tq, tq), 0)
        c = lax.broadcasted_iota(I32, (tq, tq), 1)
        tri = jnp.where(c < r, 1.0, 0.0).astype(BF16)

        def bias_tile(kv, seen, masked):
            ks = pl.multiple_of(kv * tq, tq)
            key = key_sc[pl.ds(ks, tq), :]
            eqf = jnp.where(key == thr, 1.0, 0.0)
            before = jnp.dot(tri, eqf.astype(BF16), preferred_element_type=F32) + seen
            b = jnp.where(key > thr, 0.0, jnp.where(key == thr, jnp.where(before < need, 0.0, NEG), NEG))
            if masked:
                krow = ks + lax.broadcasted_iota(I32, key.shape, 0)
                b = jnp.where(krow < limit, b, NEG)
            bias_sc[pl.ds(ks, tq), :] = b
            return seen + jnp.sum(eqf, axis=0, keepdims=True)

        seen = _unrolled_loop(qi, lambda kv, s: bias_tile(kv, s, False), jnp.zeros((1, tq), F32))
        bias_tile(qi, seen, True)

        def select(ks):
            b = bias_sc[pl.ds(ks, tq), :]
            b2 = jnp.concatenate([b, b], axis=1)
            return lambda s: s + b2

        attend(select)

    for j in range(N_PAIRS):
        _finish_pair(o_ref, j, tq, l_sc, acc_sc)


def _dsa_attention(iq, ik2, iwt, dq, dk, dvt, tq, top_k):
    B, T, W = dq.shape
    qspec = pl.BlockSpec((None, tq, W), lambda b, i: (b, i, 0))
    full = pl.BlockSpec((None, T, W), lambda b, i: (b, 0, 0))
    fullt = pl.BlockSpec((None, W, T), lambda b, i: (b, 0, 0))
    return pl.pallas_call(
        functools.partial(_dsa_kernel, tq=tq, top_k=top_k), grid=(B, T // tq),
        in_specs=[qspec, pl.BlockSpec((None, T, LANES), lambda b, i: (b, 0, 0)),
                  pl.BlockSpec((None, N_HEADS, tq), lambda b, i: (b, 0, i)), qspec, full, fullt],
        out_specs=qspec, out_shape=jax.ShapeDtypeStruct((B, T, W), BF16),
        scratch_shapes=[pltpu.VMEM((T, tq), I32), pltpu.VMEM((T, tq), I16), pltpu.VMEM((T, tq), I16),
                        pltpu.VMEM((T, tq), F32)] + _softmax_scratch(tq, LANES),
        compiler_params=_cparams("parallel", "arbitrary"), name="dsa_attention")(iq, ik2, iwt, dq, dk, dvt)


TOKEN_ROWS = D_MODEL // LANES
assert TOKEN_ROWS == SUBLANES


def _store_token_tiles(ref, y):
    tm = y.shape[0]
    for s in range(TOKEN_ROWS):
        ref[pl.ds(s, tm, stride=TOKEN_ROWS), :] = y[:, LANES * s:LANES * (s + 1)]


def _load_token_tiles(ref, tm):
    return jnp.concatenate([ref[pl.ds(s, tm, stride=TOKEN_ROWS), :] for s in range(TOKEN_ROWS)], axis=1)


def _merge_kernel(oa_ref, ob_ref, ga_ref, gb_ref, x_ref, wa_ref, wb_ref, wo_ref, g_ref, b_ref, o_ref, o2_ref,
                  *, alpha, token_tiles):
    a = jnp.dot(oa_ref[...], wa_ref[...], preferred_element_type=F32)
    b = jnp.dot(ob_ref[...], wb_ref[...], preferred_element_type=F32)
    merged = ga_ref[...].astype(F32) * a + gb_ref[...].astype(F32) * b
    mix = jnp.dot(merged.astype(BF16), wo_ref[...], preferred_element_type=F32)
    y = _layer_norm(alpha * x_ref[...] + mix, g_ref[...], b_ref[...])
    o_ref[...] = y
    if token_tiles:
        _store_token_tiles(o2_ref, y)
    else:
        o2_ref[...] = y.astype(BF16)


def _merge_project(oa, ob, ga, gb, x, wa, wb, wo, g, b, alpha, tm, token_tiles):
    n = x.shape[0]

    def rows(w):
        return pl.BlockSpec((tm, w), lambda i: (i, 0))

    def const(shape):
        return pl.BlockSpec(shape, lambda i: (0, 0), pipeline_mode=pl.Buffered(1))

    if token_tiles:
        out2_spec = pl.BlockSpec((tm * TOKEN_ROWS, LANES), lambda i: (i, 0))
        out2_shape = jax.ShapeDtypeStruct((n * TOKEN_ROWS, LANES), F32)
    else:
        out2_spec, out2_shape = rows(D_MODEL), jax.ShapeDtypeStruct((n, D_MODEL), BF16)
    return pl.pallas_call(
        functools.partial(_merge_kernel, alpha=alpha, token_tiles=token_tiles), grid=(n // tm,),
        in_specs=[rows(HEADS_W), rows(HEADS_W), rows(D_MODEL), rows(D_MODEL), rows(D_MODEL), const(wa.shape),
                  const(wb.shape), const(wo.shape), const(g.shape), const(b.shape)],
        out_specs=[rows(D_MODEL), out2_spec], out_shape=[jax.ShapeDtypeStruct((n, D_MODEL), F32), out2_shape],
        compiler_params=_cparams("parallel"), name="merge_project")(oa, ob, ga, gb, x, wa, wb, wo, g, b)


def _swiglu_partial(xb, wg, wu, wd):
    h = jnp.dot(xb, wg, preferred_element_type=F32)
    u = jnp.dot(xb, wu, preferred_element_type=F32)
    hid = (h * jax.nn.sigmoid(h)) * u
    return jnp.dot(hid.astype(BF16), wd, preferred_element_type=F32)


def _swiglu(xb, wg_ref, wu_ref, wd_ref, tf):
    acc = None
    for f in range(D_FF // tf):
        part = _swiglu_partial(xb, wg_ref[:, f * tf:(f + 1) * tf], wu_ref[:, f * tf:(f + 1) * tf],
                               wd_ref[f * tf:(f + 1) * tf, :])
        acc = part if acc is None else acc + part
    return acc


def _ffn_kernel(xb_ref, x_ref, wg_ref, wu_ref, wd_ref, g_ref, b_ref, o_ref, *, alpha, tf):
    ff = _swiglu(xb_ref[...], wg_ref, wu_ref, wd_ref, tf)
    o_ref[...] = _layer_norm(alpha * x_ref[...] + ff, g_ref[...], b_ref[...])


def _dense_ffn(xb, x, wg, wu, wd, g, b, alpha, tm, tf):
    n = x.shape[0]
    rows = pl.BlockSpec((tm, D_MODEL), lambda i: (i, 0))

    def const(shape):
        return pl.BlockSpec(shape, lambda i: (0, 0), pipeline_mode=pl.Buffered(1))

    return pl.pallas_call(
        functools.partial(_ffn_kernel, alpha=alpha, tf=tf), grid=(n // tm,),
        in_specs=[rows, rows, const(wg.shape), const(wu.shape), const(wd.shape), const(g.shape), const(b.shape)],
        out_specs=rows, out_shape=jax.ShapeDtypeStruct((n, D_MODEL), F32),
        compiler_params=_cparams("parallel"), name="dense_ffn")(xb, x, wg, wu, wd, g, b)


def _router_kernel(x_ref, r_ref, idx_ref, gate_ref, cnt_ref, seen_sc):
    i = pl.program_id(0)

    @pl.when(i == 0)
    def _():
        seen_sc[...] = jnp.zeros_like(seen_sc)

    xs = _split3(x_ref[...])
    logits = jnp.zeros((x_ref.shape[0], LANES), F32)
    for a, b in ((1, 0), (0, 1), (0, 0)):
        logits = logits + jnp.dot(xs[a], r_ref[b], preferred_element_type=F32)
    lane = lax.broadcasted_iota(I32, logits.shape, 1)
    lg = jnp.where(lane < N_EXPERTS, logits, -jnp.inf)
    v1 = jnp.max(lg, axis=-1, keepdims=True)
    i1 = jnp.min(jnp.where(lg == v1, lane, LANES), axis=-1, keepdims=True)
    lg2 = jnp.where(lane == i1, -jnp.inf, lg)
    v2 = jnp.max(lg2, axis=-1, keepdims=True)
    i2 = jnp.min(jnp.where(lg2 == v2, lane, LANES), axis=-1, keepdims=True)
    e = jnp.exp(v2 - v1)
    g1 = 1.0 / (1.0 + e)
    g2 = e / (1.0 + e)
    hit1 = lane == i1
    hit2 = lane == i2
    cnt = jnp.where(hit1, 1.0, jnp.where(hit2, 1.0, 0.0))
    tm = cnt.shape[0]
    r = lax.broadcasted_iota(I32, (tm, tm), 0)
    c = lax.broadcasted_iota(I32, (tm, tm), 1)
    tri = jnp.where(c < r, 1.0, 0.0).astype(BF16)
    before = jnp.dot(tri, cnt.astype(BF16), preferred_element_type=F32) + seen_sc[...]
    rank1 = jnp.sum(jnp.where(hit1, before, 0.0), axis=-1, keepdims=True).astype(I32)
    rank2 = jnp.sum(jnp.where(hit2, before, 0.0), axis=-1, keepdims=True).astype(I32)
    seen = seen_sc[...] + jnp.sum(cnt, axis=0, keepdims=True)
    seen_sc[...] = seen
    idx_ref[...] = jnp.where(lane == 0, i1, jnp.where(lane == 1, i2, jnp.where(lane == 2, rank1,
                                                                                  jnp.where(lane == 3, rank2, 0))))
    gate_ref[...] = jnp.where(lane == 0, g1, jnp.where(lane == 1, g2, 0.0))
    cnt_ref[...] = jnp.broadcast_to(seen, cnt_ref.shape).astype(I32)


def _route(x, router, tm):
    n = x.shape[0]
    r_pad = jnp.concatenate([router, jnp.zeros((D_MODEL, LANES - N_EXPERTS), F32)], axis=1)
    r3 = jnp.stack(_split3(r_pad))
    rows = pl.BlockSpec((tm, LANES), lambda i: (i, 0))
    return pl.pallas_call(
        _router_kernel, grid=(n // tm,),
        in_specs=[pl.BlockSpec((tm, D_MODEL), lambda i: (i, 0)), pl.BlockSpec(r3.shape, lambda i: (0, 0, 0))],
        out_specs=[rows, rows, pl.BlockSpec((SUBLANES, LANES), lambda i: (0, 0))],
        out_shape=[jax.ShapeDtypeStruct((n, LANES), I32), jax.ShapeDtypeStruct((n, LANES), F32),
                   jax.ShapeDtypeStruct((SUBLANES, LANES), I32)],
        scratch_shapes=[pltpu.VMEM((1, LANES), F32)],
        compiler_params=_cparams("arbitrary"), name="moe_router")(x, r3)


def _tile_rows(idx):
    return pl.ds(pl.multiple_of(idx * TOKEN_ROWS, TOKEN_ROWS), TOKEN_ROWS)


def _dispatch_kernel(slot_ref, x_ref, init_ref, xs_ref, sem, *, tm):
    del init_ref
    base = pl.program_id(0) * tm

    def start(r, carry):
        src = x_ref.at[_tile_rows(r), :]
        for j in range(2):
            pltpu.make_async_copy(src, xs_ref.at[_tile_rows(slot_ref[2 * (base + r) + j]), :], sem).start()
        return carry

    lax.fori_loop(0, tm, start, 0, unroll=8)
    for _ in range(2):
        pltpu.make_async_copy(x_ref, xs_ref.at[pl.ds(0, tm * TOKEN_ROWS), :], sem).wait()


def _dispatch(slots_flat, xt, n_slots, tm):
    n = xt.shape[0] // TOKEN_ROWS
    any_spec = pl.BlockSpec(memory_space=pl.ANY)
    gs = pltpu.PrefetchScalarGridSpec(
        num_scalar_prefetch=1, grid=(n // tm,),
        in_specs=[pl.BlockSpec((tm * TOKEN_ROWS, LANES), lambda i, slots: (i, 0)), any_spec],
        out_specs=any_spec, scratch_shapes=[pltpu.SemaphoreType.DMA(())])
    return pl.pallas_call(
        functools.partial(_dispatch_kernel, tm=tm), grid_spec=gs,
        out_shape=jax.ShapeDtypeStruct((n_slots * TOKEN_ROWS, LANES), F32), input_output_aliases={2: 0},
        compiler_params=_cparams("arbitrary"), name="moe_dispatch",
    )(slots_flat, xt, jnp.zeros((n_slots * TOKEN_ROWS, LANES), F32))


def _moe_ffn_kernel(be_ref, nv_ref, xs_ref, wg_ref, wu_ref, wd_ref, y_ref, *, tm, tf):
    del be_ref
    live = pl.program_id(0) < nv_ref[0]

    @pl.when(live)
    def _():
        xb = _load_token_tiles(xs_ref, tm).astype(BF16)
        _store_token_tiles(y_ref, _swiglu(xb, wg_ref, wu_ref, wd_ref, tf))

    @pl.when(jnp.logical_not(live))
    def _():
        y_ref[...] = jnp.zeros_like(y_ref)


def _moe_ffn(block_expert, n_valid, xs, wg, wu, wd, tm, tf):
    p = xs.shape[0] // TOKEN_ROWS
    tiles = pl.BlockSpec((tm * TOKEN_ROWS, LANES), lambda i, be, nv: (i, 0))

    def expert(shape):
        return pl.BlockSpec((None,) + shape[1:], lambda i, be, nv: (be[i], 0, 0), pipeline_mode=pl.Buffered(1))

    gs = pltpu.PrefetchScalarGridSpec(
        num_scalar_prefetch=2, grid=(p // tm,),
        in_specs=[tiles, expert(wg.shape), expert(wu.shape), expert(wd.shape)], out_specs=tiles)
    return pl.pallas_call(
        functools.partial(_moe_ffn_kernel, tm=tm, tf=tf), grid_spec=gs, out_shape=jax.ShapeDtypeStruct(xs.shape, F32),
        compiler_params=_cparams("arbitrary"), name="moe_ffn")(block_expert, n_valid, xs, wg, wu, wd)


def _combine_kernel(slot_ref, x_ref, gate_ref, g_ref, b_ref, y_ref, o_ref, ybuf, sem, *, alpha, tm):
    i = pl.program_id(0)

    def gather(step, par):
        base = step * tm

        def start(r, carry):
            for j in range(2):
                src = y_ref.at[_tile_rows(slot_ref[2 * (base + r) + j]), :]
                pltpu.make_async_copy(src, ybuf.at[par, j, _tile_rows(r), :], sem.at[par]).start()
            return carry

        lax.fori_loop(0, tm, start, 0, unroll=8)

    @pl.when(i == 0)
    def _():
        gather(0, 0)

    @pl.when(i + 1 < pl.num_programs(0))
    def _():
        gather(i + 1, (i + 1) & 1)

    par = i & 1
    for j in range(2):
        pltpu.make_async_copy(y_ref.at[pl.ds(0, tm * TOKEN_ROWS), :], ybuf.at[par, j], sem.at[par]).wait()
    gt = gate_ref[...]
    ff = gt[:, 0:1] * _load_token_tiles(ybuf.at[par, 0], tm) + gt[:, 1:2] * _load_token_tiles(ybuf.at[par, 1], tm)
    o_ref[...] = _layer_norm(alpha * x_ref[...] + ff, g_ref[...], b_ref[...])


def _combine(slots_flat, x, gates, g, b, y, alpha, tm):
    n = x.shape[0]
    rows = pl.BlockSpec((tm, D_MODEL), lambda i, slots: (i, 0))
    vec = pl.BlockSpec((1, D_MODEL), lambda i, slots: (0, 0))
    gs = pltpu.PrefetchScalarGridSpec(
        num_scalar_prefetch=1, grid=(n // tm,),
        in_specs=[rows, pl.BlockSpec((tm, LANES), lambda i, slots: (i, 0)), vec, vec,
                  pl.BlockSpec(memory_space=pl.ANY)],
        out_specs=rows, scratch_shapes=[pltpu.VMEM((2, 2, tm * TOKEN_ROWS, LANES), F32),
                                        pltpu.SemaphoreType.DMA((2,))])
    return pl.pallas_call(
        functools.partial(_combine_kernel, alpha=alpha, tm=tm), grid_spec=gs,
        out_shape=jax.ShapeDtypeStruct((n, D_MODEL), F32), compiler_params=_cparams("arbitrary"),
        name="moe_combine")(slots_flat, x, gates, g, b, y)


def _moe_layer(x, xt, router, wg, wu, wd, g, b, alpha, tiles):
    n = x.shape[0]
    tmb = tiles["moe_rows"]
    idx, gates, counts = _route(x, router, tiles["route_rows"])
    counts = counts[0, :N_EXPERTS]
    padded = ((counts + tmb - 1) // tmb) * tmb
    pad_end = jnp.cumsum(padded)
    pad_start = pad_end - padded
    slots = jnp.stack([pad_start[idx[:, 0]] + idx[:, 2], pad_start[idx[:, 1]] + idx[:, 3]], axis=1)
    slots_flat = slots.reshape(-1).astype(I32)
    n_slots = 2 * n + N_EXPERTS * tmb
    n_blocks = n_slots // tmb
    block_expert = jnp.clip(jnp.searchsorted(pad_end, jnp.arange(n_blocks, dtype=I32) * tmb, side="right"),
                            0, N_EXPERTS - 1).astype(I32)
    n_valid = (pad_end[-1:] // tmb).astype(I32)
    xs = _dispatch(slots_flat, xt, n_slots, tiles["dma_rows"])
    y = _moe_ffn(block_expert, n_valid, xs, wg, wu, wd, tmb, tiles["ffn_cols"])
    return _combine(slots_flat, x, gates, g, b, y, alpha, tiles["dma_rows"])


def _tiles(n, t):
    return dict(
        proj_rows=min(256, t), attn_q=min(256, t), row_tile=min(512, n),
        ffn_cols=512, route_rows=min(256, n), moe_rows=min(512, n), dma_rows=min(256, n))


def kernel(x, positions, w_in, b_forget, w_branch_a, w_branch_b, w_out, ln_mix_g, ln_mix_b, ln_ffn_g, ln_ffn_b,
           ffn_w_gate, ffn_w_up, ffn_w_down, moe_router, moe_w_gate, moe_w_up, moe_w_down):
    B, T, D = x.shape
    assert D == D_MODEL and T % CHUNK == 0
    n = B * T
    depth = w_in.shape[0]
    top_k = min(TOPK_MAX, T // 4)
    alpha = (2.0 * depth) ** 0.25
    tiles = _tiles(n, T)
    cos_t, sin_t = _rope_tables(positions)
    xf = x.reshape(n, D)
    for layer in range(depth):
        weights = _prep_mix_weights(w_in[layer], b_forget[layer])
        (dq, dk, iq, fq, fk, ga, gb, ik2, dvt, fvt, iwt, aux) = _in_projection(
            xf.reshape(B, T, D), cos_t, sin_t, weights, tiles["proj_rows"])
        o_a = _dsa_attention(iq, ik2, iwt, dq, dk, dvt, tiles["attn_q"], top_k)
        o_b = _fox_attention(fq, fk, aux, fvt, tiles["attn_q"])
        dense = layer % 2 == 0
        x1, x1_alt = _merge_project(
            o_a.reshape(n, HEADS_W), o_b.reshape(n, HEADS_W), ga.reshape(n, D), gb.reshape(n, D), xf,
            w_branch_a[layer].astype(BF16), w_branch_b[layer].astype(BF16), w_out[layer].astype(BF16),
            ln_mix_g[layer][None, :], ln_mix_b[layer][None, :], alpha, tiles["row_tile"], token_tiles=not dense)
        j = layer // 2
        g, b = ln_ffn_g[layer][None, :], ln_ffn_b[layer][None, :]
        if dense:
            xf = _dense_ffn(x1_alt, x1, ffn_w_gate[j].astype(BF16), ffn_w_up[j].astype(BF16),
                            ffn_w_down[j].astype(BF16), g, b, alpha, tiles["row_tile"], tiles["ffn_cols"])
        else:
            xf = _moe_layer(x1, x1_alt, moe_router[j], moe_w_gate[j].astype(BF16), moe_w_up[j].astype(BF16),
                            moe_w_down[j].astype(BF16), g, b, alpha, tiles)
    return xf.reshape(B, T, D)
```

```python
import functools

import jax
import jax.numpy as jnp
from jax import lax
from jax.experimental import pallas as pl
from jax.experimental.pallas import tpu as pltpu

F32, BF16, I32, I16 = jnp.float32, jnp.bfloat16, jnp.int32, jnp.int16

D_MODEL = 1024
HEAD_DIM = 64
N_HEADS = 8
HEADS_W = N_HEADS * HEAD_DIM
N_PAIRS = N_HEADS // 2
CHUNK = 64
TOPK_MAX = 256
ROPE_THETA = 10000.0
D_FF = 3584
N_EXPERTS = 8
LN_EPS = 1e-5
DEPTH = 2
LANES = 128
SUBLANES = 8
PACKED_ROWS = 2 * SUBLANES
INT_MIN = -(2**31)
I16_MIN = -(2**15)
NEG = -1e30
VMEM_LIMIT = 56 * 1024 * 1024
NT_DIMS = (((1,), (1,)), ((), ()))


def _cparams(*sem):
    return pltpu.CompilerParams(dimension_semantics=sem, vmem_limit_bytes=VMEM_LIMIT)


def _split3(v):
    hi = v.astype(BF16)
    r1 = v - hi.astype(F32)
    mid = r1.astype(BF16)
    lo = (r1 - mid.astype(F32)).astype(BF16)
    return hi, mid, lo


def _layer_norm(y, g, b):
    mu = jnp.mean(y, axis=-1, keepdims=True)
    yc = y - mu
    var = jnp.mean(yc * yc, axis=-1, keepdims=True)
    return yc * lax.rsqrt(var + LN_EPS) * g + b


def _rope_kernel(pos_ref, freq_ref, sign_ref, cos_ref, sin_ref):
    ang = pos_ref[...].astype(F32) * freq_ref[...]
    cos_ref[...] = jnp.cos(ang)
    sin_ref[...] = jnp.sin(ang) * sign_ref[...]


def _rope_tables(positions):
    B, T = positions.shape
    n = B * T
    inv_freq = ROPE_THETA ** (-jnp.arange(0, HEAD_DIM, 2, dtype=F32) / HEAD_DIM)
    freq = jnp.tile(inv_freq, LANES // (HEAD_DIM // 2))[None, :]
    half = HEAD_DIM // 2
    sign = jnp.tile(jnp.concatenate([-jnp.ones((half,), F32), jnp.ones((half,), F32)]), LANES // HEAD_DIM)[None, :]
    pos_b = jnp.broadcast_to(positions.reshape(n, 1), (n, LANES))
    tm = min(n, 1024)
    row = pl.BlockSpec((tm, LANES), lambda i: (i, 0))
    const = pl.BlockSpec((1, LANES), lambda i: (0, 0))
    cos, sin = pl.pallas_call(
        _rope_kernel, grid=(n // tm,), in_specs=[row, const, const], out_specs=[row, row],
        out_shape=[jax.ShapeDtypeStruct((n, LANES), F32)] * 2, compiler_params=_cparams("parallel"),
        name="rope_tables")(pos_b, freq, sign)
    return cos.reshape(B, T, LANES), sin.reshape(B, T, LANES)


LOG2E = 1.4426950408889634
_Q_SCALE = HEAD_DIM ** -0.5 * LOG2E
_IW_SCALE = (N_HEADS ** -0.5) * (HEAD_DIM ** -0.5)
_OFF_DQ, _OFF_DK, _OFF_IQ, _OFF_FQ, _OFF_FK, _OFF_GA, _OFF_GB = 0, 512, 1024, 1536, 2048, 2560, 3584
_W_MAIN = 4608
_AUX_USED = 6


def _inproj_kernel(x_ref, cos_ref, sin_ref, wm_ref, wik_ref, wvt_ref, wit_ref, wfl_ref, bfl_ref,
                   dq_ref, dk_ref, iq_ref, fq_ref, fk_ref, ga_ref, gb_ref, ik2_ref, dvt_ref, fvt_ref,
                   iwt_ref, aux_ref, carry_sc):
    @pl.when(pl.program_id(1) == 0)
    def _():
        carry_sc[...] = jnp.zeros_like(carry_sc)

    xb = x_ref[...].astype(BF16)
    cos = cos_ref[...]
    sin = sin_ref[...]
    lane = lax.broadcasted_iota(I32, cos.shape, 1)
    first_half = (lane & (HEAD_DIM - 1)) < HEAD_DIM // 2

    def rot(y):
        partner = jnp.where(first_half, pltpu.roll(y, LANES - HEAD_DIM // 2, 1), pltpu.roll(y, HEAD_DIM // 2, 1))
        return y * cos + partner * sin

    def proj(off, width):
        return jnp.dot(xb, wm_ref[:, off:off + width], preferred_element_type=F32)

    def rot_group(off, scale, out_ref):
        y = proj(off, HEADS_W)
        for j in range(N_PAIRS):
            r = rot(y[:, LANES * j:LANES * (j + 1)])
            if scale != 1.0:
                r = r * scale
            out_ref[:, LANES * j:LANES * (j + 1)] = r.astype(BF16)

    rot_group(_OFF_DQ, _Q_SCALE, dq_ref)
    rot_group(_OFF_DK, 1.0, dk_ref)
    rot_group(_OFF_IQ, 1.0, iq_ref)
    fq_ref[...] = (proj(_OFF_FQ, HEADS_W) * _Q_SCALE).astype(BF16)
    fk_ref[...] = proj(_OFF_FK, HEADS_W).astype(BF16)
    ga_ref[...] = jax.nn.sigmoid(proj(_OFF_GA, D_MODEL)).astype(BF16)
    gb_ref[...] = jax.nn.sigmoid(proj(_OFF_GB, D_MODEL)).astype(BF16)
    ik2_ref[...] = rot(jnp.dot(xb, wik_ref[...], preferred_element_type=F32)).astype(BF16)
    vt = lax.dot_general(wvt_ref[...], xb, NT_DIMS, preferred_element_type=F32)
    dvt_ref[...] = vt[:HEADS_W].astype(BF16)
    fvt_ref[...] = vt[HEADS_W:].astype(BF16)
    it = lax.dot_general(wit_ref[...], xb, NT_DIMS, preferred_element_type=F32)
    iwt_ref[...] = it[:N_HEADS] * _IW_SCALE
    z = jnp.dot(xb, wfl_ref[...], preferred_element_type=F32) + bfl_ref[...]
    lf = jnp.minimum(z, 0.0) - jnp.log1p(jnp.exp(-jnp.abs(z)))
    lane_w = lax.broadcasted_iota(I32, lf.shape, 1) & (LANES - 1)
    aux_ref[...] = _forget_aux(jnp.where(lane_w < _AUX_USED, lf, 0.0), lane_w, carry_sc)


def _forget_aux(lf, lane_w, carry_sc):
    tb = lf.shape[0]
    r = lax.broadcasted_iota(I32, (tb, tb), 0)
    c = lax.broadcasted_iota(I32, (tb, tb), 1)
    tri = jnp.where(c <= r, 1.0, 0.0).astype(BF16)
    hi, mid, lo = _split3(lf)
    cs = (jnp.dot(tri, lo, preferred_element_type=F32) + jnp.dot(tri, mid, preferred_element_type=F32)
          + jnp.dot(tri, hi, preferred_element_type=F32) + carry_sc[...])
    carry_sc[...] = cs[tb - 1:tb, :]
    nh, nm, nl = _split3(-LOG2E * cs)
    k = jnp.where(lane_w >= 3, lane_w - 3, lane_w)
    piece = jnp.where(k == 0, nh.astype(F32), jnp.where(k == 1, nm.astype(F32), nl.astype(F32)))
    return jnp.where(lane_w < _AUX_USED, piece, 0.0).astype(BF16)


def _prep_mix_weights(w_in, b_forget):
    sizes = (HEADS_W, HEADS_W, HEADS_W, HEADS_W, HEAD_DIM, N_HEADS, HEADS_W, HEADS_W, HEADS_W, N_HEADS, D_MODEL, D_MODEL)
    offs = [0]
    for s in sizes:
        offs.append(offs[-1] + s)
    dq, dk, dv, iq, ik, iw, fq, fk, fv, fl, ga, gb = [w_in[:, offs[i]:offs[i + 1]] for i in range(12)]
    w_main = jnp.concatenate([dq, dk, iq, fq, fk, ga, gb], axis=1).astype(BF16)
    w_ik2 = jnp.concatenate([ik, ik], axis=1).astype(BF16)
    w_vt = jnp.concatenate([dv, fv], axis=1).T.astype(BF16)
    w_it = jnp.concatenate([iw.T, jnp.zeros((16 - N_HEADS, D_MODEL), F32)], axis=0).astype(BF16)
    src = []
    for j in range(N_PAIRS):
        src += [2 * j] * 3 + [2 * j + 1] * 3 + [-1] * (LANES - _AUX_USED)
    src = jnp.array(src, I32)
    used = src >= 0
    w_fl = jnp.where(used[None, :], fl[:, jnp.maximum(src, 0)], 0.0).astype(BF16)
    b_fl = jnp.where(used, b_forget[jnp.maximum(src, 0)], 0.0)[None, :].astype(F32)
    return w_main, w_ik2, w_vt, w_it, w_fl, b_fl


def _in_projection(x, cos_t, sin_t, weights, tm):
    B, T, _ = x.shape
    w_main, w_ik2, w_vt, w_it, w_fl, b_fl = weights
    nt = T // tm

    def rows(width):
        return pl.BlockSpec((None, tm, width), lambda b, i: (b, i, 0))

    def cols(height):
        return pl.BlockSpec((None, height, tm), lambda b, i: (b, 0, i))

    def const(shape):
        return pl.BlockSpec(shape, lambda b, i: (0, 0), pipeline_mode=pl.Buffered(1))

    bt = lambda w, dt: jax.ShapeDtypeStruct((B, T, w), dt)
    tb = lambda h, dt: jax.ShapeDtypeStruct((B, h, T), dt)
    return pl.pallas_call(
        _inproj_kernel, grid=(B, nt),
        in_specs=[rows(D_MODEL), rows(LANES), rows(LANES), const(w_main.shape), const(w_ik2.shape),
                  const(w_vt.shape), const(w_it.shape), const(w_fl.shape), const(b_fl.shape)],
        out_specs=[rows(HEADS_W)] * 5 + [rows(D_MODEL)] * 2 + [rows(LANES), cols(HEADS_W), cols(HEADS_W),
                                                                cols(N_HEADS), rows(HEADS_W)],
        out_shape=[bt(HEADS_W, BF16)] * 5 + [bt(D_MODEL, BF16)] * 2 + [bt(LANES, BF16), tb(HEADS_W, BF16),
                                                                      tb(HEADS_W, BF16), tb(N_HEADS, F32),
                                                                      bt(HEADS_W, BF16)],
        scratch_shapes=[pltpu.VMEM((1, HEADS_W), F32)],
        compiler_params=_cparams("parallel", "arbitrary"), name="in_projection",
    )(x, cos_t, sin_t, w_main, w_ik2, w_vt, w_it, w_fl, b_fl)


def _stack_pair(qb):
    qf = qb.astype(F32)
    lane = lax.broadcasted_iota(I32, qf.shape, 1)
    lo = lane < HEAD_DIM
    return jnp.concatenate([jnp.where(lo, qf, 0.0), jnp.where(lo, 0.0, qf)], axis=0)


def _softmax_step(s, vb, j, m_sc, l_sc, acc_sc):
    tq = s.shape[1] // 2
    m_old = m_sc[j]
    m_new = jnp.maximum(m_old, jnp.max(s, axis=0, keepdims=True))
    alpha = jnp.exp2(m_old - m_new)
    p = jnp.exp2(s - m_new)
    l_sc[j] = alpha * l_sc[j] + jnp.sum(p, axis=0, keepdims=True)
    pb = p.astype(BF16)
    pv = jnp.concatenate([jnp.dot(vb[:HEAD_DIM], pb[:, :tq], preferred_element_type=F32),
                          jnp.dot(vb[HEAD_DIM:], pb[:, tq:], preferred_element_type=F32)], axis=1)
    acc_sc[j] = alpha * acc_sc[j] + pv
    m_sc[j] = m_new


def _init_softmax(m_sc, l_sc, acc_sc):
    m_sc[...] = jnp.full(m_sc.shape, -jnp.inf, F32)
    l_sc[...] = jnp.zeros(l_sc.shape, F32)
    acc_sc[...] = jnp.zeros(acc_sc.shape, F32)


def _finish_pair(o_ref, j, tq, l_sc, acc_sc):
    out_t = acc_sc[j] / l_sc[j]
    blk_t = jnp.concatenate([out_t[:, :tq], out_t[:, tq:]], axis=0)
    o_ref[:, LANES * j:LANES * (j + 1)] = blk_t.T.astype(BF16)


def _softmax_scratch(tq, q_width):
    return [pltpu.VMEM((N_PAIRS, q_width, 2 * tq), BF16), pltpu.VMEM((2, N_PAIRS, tq, 2 * tq), F32),
            pltpu.VMEM((N_PAIRS, 1, 2 * tq), F32), pltpu.VMEM((N_PAIRS, 1, 2 * tq), F32),
            pltpu.VMEM((N_PAIRS, HEAD_DIM, 2 * tq), F32)]


def _unrolled_loop(n, step, carry, unroll=2):
    def main(h, c):
        for u in range(unroll):
            c = step(unroll * h + u, c)
        return c

    n_main = n // unroll
    carry = lax.fori_loop(0, n_main, main, carry)
    return lax.fori_loop(n_main * unroll, n, step, carry)


def _pipelined_attention(n_steps, scores, update):
    scores(0, 0)
    n_mid = n_steps - 1

    def body(h, carry):
        kv = 2 * h
        for u in range(2):
            scores(kv + u + 1, (u + 1) & 1)
            update(kv + u, u & 1, False)
        return carry

    lax.fori_loop(0, n_mid // 2, body, 0)
    odd = n_mid % 2 == 1

    @pl.when(odd)
    def _():
        scores(n_mid, 1)
        update(n_mid - 1, 0, False)
        update(n_mid, 1, True)

    @pl.when(jnp.logical_not(odd))
    def _():
        update(n_mid, 0, True)


def _fox_kernel(q_ref, k_ref, aux_ref, vt_ref, o_ref, qt_sc, s_sc, m_sc, l_sc, acc_sc, *, tq):
    qi = pl.program_id(1)
    q0 = qi * tq
    row2 = lax.broadcasted_iota(I32, (2 * tq, LANES), 0)
    lane2 = lax.broadcasted_iota(I32, (2 * tq, LANES), 1)
    sel = jnp.where(row2 < tq, jnp.where(lane2 < 3, 1.0, 0.0),
                    jnp.where(lane2 < 3, 0.0, jnp.where(lane2 < _AUX_USED, 1.0, 0.0)))
    for j in range(N_PAIRS):
        qa = jnp.concatenate([_stack_pair(q_ref[:, LANES * j:LANES * (j + 1)]), sel], axis=1)
        qt_sc[j] = qa.T.astype(BF16)
    _init_softmax(m_sc, l_sc, acc_sc)

    def scores(kv, slot):
        ks = pl.multiple_of(kv * tq, tq)
        for j in range(N_PAIRS):
            cs = slice(LANES * j, LANES * (j + 1))
            kb = jnp.concatenate([k_ref[pl.ds(ks, tq), cs], aux_ref[pl.ds(ks, tq), cs]], axis=1)
            s_sc[slot, j] = jnp.dot(kb, qt_sc[j], preferred_element_type=F32)

    def update(kv, slot, last):
        ks = pl.multiple_of(kv * tq, tq)
        if last:
            krow = ks + lax.broadcasted_iota(I32, (tq, 2 * tq), 0)
            qcol = q0 + (lax.broadcasted_iota(I32, (tq, 2 * tq), 1) & (tq - 1))
            keep = krow <= qcol
        for j in range(N_PAIRS):
            s = s_sc[slot, j]
            if last:
                s = jnp.where(keep, s, NEG)
            _softmax_step(s, vt_ref[LANES * j:LANES * (j + 1), pl.ds(ks, tq)], j, m_sc, l_sc, acc_sc)

    _pipelined_attention(qi + 1, scores, update)
    for j in range(N_PAIRS):
        _finish_pair(o_ref, j, tq, l_sc, acc_sc)


def _fox_attention(fq, fk, aux, fvt, tq):
    B, T, W = fq.shape
    qspec = pl.BlockSpec((None, tq, W), lambda b, i: (b, i, 0))
    full = pl.BlockSpec((None, T, W), lambda b, i: (b, 0, 0))
    fullt = pl.BlockSpec((None, W, T), lambda b, i: (b, 0, 0))
    return pl.pallas_call(
        functools.partial(_fox_kernel, tq=tq), grid=(B, T // tq),
        in_specs=[qspec, full, full, fullt], out_specs=qspec,
        out_shape=jax.ShapeDtypeStruct((B, T, W), BF16), scratch_shapes=_softmax_scratch(tq, 2 * LANES),
        compiler_params=_cparams("parallel", "arbitrary"), name="fox_attention")(fq, fk, aux, fvt)


_COUNT_CHAINS = 4


def _dsa_kernel(iq_ref, ik2_ref, iwt_ref, q_ref, k_ref, vt_ref, o_ref, key_sc, hi_sc, lo_sc, bias_sc, qt_sc, s_sc,
                m_sc, l_sc, acc_sc, *, tq, top_k):
    qi = pl.program_id(1)
    q0 = qi * tq
    col = lax.broadcasted_iota(I32, (1, tq), 1)
    limit = (lax.shift_right_logical(q0 + col, CHUNK.bit_length() - 1) + 1) * CHUNK
    iw = iwt_ref[...]

    for j in range(N_PAIRS):
        qt_sc[j] = _stack_pair(iq_ref[:, LANES * j:LANES * (j + 1)]).T.astype(BF16)

    def score_tile(kv, masked):
        ks = pl.multiple_of(kv * tq, tq)
        kb = ik2_ref[pl.ds(ks, tq), :]
        sc = jnp.zeros((tq, tq), F32)
        for j in range(N_PAIRS):
            s2 = jnp.dot(kb, qt_sc[j], preferred_element_type=F32)
            sc = sc + jnp.maximum(s2[:, :tq], 0.0) * iw[2 * j:2 * j + 1, :]
            sc = sc + jnp.maximum(s2[:, tq:], 0.0) * iw[2 * j + 1:2 * j + 2, :]
        bits = pltpu.bitcast(sc, I32)
        key = bits ^ ((bits >> 31) & jnp.int32(0x7FFFFFFF))
        key = jnp.where(bits == jnp.int32(INT_MIN), 0, key)
        if masked:
            krow = ks + lax.broadcasted_iota(I32, key.shape, 0)
            key = jnp.where(krow < limit, key, jnp.int32(INT_MIN))
        key_sc[pl.ds(ks, tq), :] = key
        hi_sc[pl.ds(ks, tq), :] = (key >> 16).astype(I16)
        lo_sc[pl.ds(ks, tq), :] = ((key & 0xFFFF) + I16_MIN).astype(I16)

    def score_body(kv, carry):
        score_tile(kv, False)
        return carry

    _unrolled_loop(qi, score_body, 0)
    score_tile(qi, True)

    def count(plane_sc, pred):
        def body(g, accs):
            slab = plane_sc[pl.ds(pl.multiple_of(g * tq, tq), tq), :]
            accs = list(accs)
            for r in range(tq // PACKED_ROWS):
                blk = slab[PACKED_ROWS * r:PACKED_ROWS * (r + 1), :]
                hit = jnp.where(pred(blk), jnp.int16(1), jnp.int16(0))
                accs[r % _COUNT_CHAINS] = accs[r % _COUNT_CHAINS] + hit
            return tuple(accs)

        zero = jnp.zeros((PACKED_ROWS, tq), I16)
        accs = lax.fori_loop(0, qi + 1, body, (zero,) * _COUNT_CHAINS)
        return jnp.sum(sum(a.astype(I32) for a in accs), axis=0, keepdims=True)

    def rows16(v):
        return jnp.broadcast_to(v.astype(I16), (PACKED_ROWS, tq))

    def search(plane_sc, above):
        def body(i, v):
            cand = v + lax.shift_left(jnp.int32(1), 15 - i)
            cand16 = rows16(cand)
            cnt = above + count(plane_sc, lambda blk: blk >= cand16)
            return jnp.where(cnt >= top_k, cand, v)

        return lax.fori_loop(0, 16, body, jnp.full((1, tq), I16_MIN, I32))

    t_hi = search(hi_sc, 0)
    t_hi16 = rows16(t_hi)
    above_hi = count(hi_sc, lambda blk: blk > t_hi16)

    def mask_low(g, carry):
        rows = pl.ds(pl.multiple_of(g * tq, tq), tq)
        lo_sc[rows, :] = jnp.where(hi_sc[rows, :] == t_hi.astype(I16), lo_sc[rows, :], jnp.int16(I16_MIN))
        return carry

    lax.fori_loop(0, qi + 1, mask_low, 0)
    t_lo = search(lo_sc, above_hi)
    t_lo16 = rows16(t_lo)
    thr = t_hi * 65536 + (t_lo - I16_MIN)
    n_ge = above_hi + count(lo_sc, lambda blk: blk >= t_lo16)
    tie_walk = jnp.maximum(jnp.where(n_ge > top_k, 1, 0), jnp.where(t_lo == I16_MIN, 1, 0))
    any_tie_walk = jnp.max(tie_walk) > 0

    for j in range(N_PAIRS):
        qt_sc[j] = _stack_pair(q_ref[:, LANES * j:LANES * (j + 1)]).T.astype(BF16)
    _init_softmax(m_sc, l_sc, acc_sc)

    def scores(kv, slot):
        ks = pl.multiple_of(kv * tq, tq)
        for j in range(N_PAIRS):
            s_sc[slot, j] = jnp.dot(k_ref[pl.ds(ks, tq), LANES * j:LANES * (j + 1)], qt_sc[j],
                                    preferred_element_type=F32)

    def attend(select):
        def update(kv, slot, last):
            del last
            ks = pl.multiple_of(kv * tq, tq)
            keep_selected = select(ks)
            for j in range(N_PAIRS):
                _softmax_step(keep_selected(s_sc[slot, j]), vt_ref[LANES * j:LANES * (j + 1), pl.ds(ks, tq)], j,
                              m_sc, l_sc, acc_sc)

        _pipelined_attention(qi + 1, scores, update)

    @pl.when(jnp.logical_not(any_tie_walk))
    def _():
        def select(ks):
            b = jnp.where(key_sc[pl.ds(ks, tq), :] >= thr, 0.0, NEG)
            b2 = jnp.concatenate([b, b], axis=1)
            return lambda s: s + b2

        attend(select)

    @pl.when(any_tie_walk)
    def _():
        need = (top_k - above_hi - count(lo_sc, lambda blk: blk > t_lo16)).astype(F32)
        r = lax.broadcasted_iota(I32, (tq, tq), 0)
        c = lax.broadcasted_iota(I32, (tq, tq), 1)
        tri = jnp.where(c < r, 1.0, 0.0).astype(BF16)

        def bias_tile(kv, seen, masked):
            ks = pl.multiple_of(kv * tq, tq)
            key = key_sc[pl.ds(ks, tq), :]
            eqf = jnp.where(key == thr, 1.0, 0.0)
            before = jnp.dot(tri, eqf.astype(BF16), preferred_element_type=F32) + seen
            b = jnp.where(key > thr, 0.0, jnp.where(key == thr, jnp.where(before < need, 0.0, NEG), NEG))
            if masked:
                krow = ks + lax.broadcasted_iota(I32, key.shape, 0)
                b = jnp.where(krow < limit, b, NEG)
            bias_sc[pl.ds(ks, tq), :] = b
            return seen + jnp.sum(eqf, axis=0, keepdims=True)

        seen = _unrolled_loop(qi, lambda kv, s: bias_tile(kv, s, False), jnp.zeros((1, tq), F32))
        bias_tile(qi, seen, True)

        def select(ks):
            b = bias_sc[pl.ds(ks, tq), :]
            b2 = jnp.concatenate([b, b], axis=1)
            return lambda s: s + b2

        attend(select)

    for j in range(N_PAIRS):
        _finish_pair(o_ref, j, tq, l_sc, acc_sc)


def _dsa_attention(iq, ik2, iwt, dq, dk, dvt, tq, top_k):
    B, T, W = dq.shape
    qspec = pl.BlockSpec((None, tq, W), lambda b, i: (b, i, 0))
    full = pl.BlockSpec((None, T, W), lambda b, i: (b, 0, 0))
    fullt = pl.BlockSpec((None, W, T), lambda b, i: (b, 0, 0))
    return pl.pallas_call(
        functools.partial(_dsa_kernel, tq=tq, top_k=top_k), grid=(B, T // tq),
        in_specs=[qspec, pl.BlockSpec((None, T, LANES), lambda b, i: (b, 0, 0)),
                  pl.BlockSpec((None, N_HEADS, tq), lambda b, i: (b, 0, i)), qspec, full, fullt],
        out_specs=qspec, out_shape=jax.ShapeDtypeStruct((B, T, W), BF16),
        scratch_shapes=[pltpu.VMEM((T, tq), I32), pltpu.VMEM((T, tq), I16), pltpu.VMEM((T, tq), I16),
                        pltpu.VMEM((T, tq), F32)] + _softmax_scratch(tq, LANES),
        compiler_params=_cparams("parallel", "arbitrary"), name="dsa_attention")(iq, ik2, iwt, dq, dk, dvt)


TOKEN_ROWS = D_MODEL // LANES
assert TOKEN_ROWS == SUBLANES


def _store_token_tiles(ref, y):
    tm = y.shape[0]
    for s in range(TOKEN_ROWS):
        ref[pl.ds(s, tm, stride=TOKEN_ROWS), :] = y[:, LANES * s:LANES * (s + 1)]


def _load_token_tiles(ref, tm):
    return jnp.concatenate([ref[pl.ds(s, tm, stride=TOKEN_ROWS), :] for s in range(TOKEN_ROWS)], axis=1)


def _merge_kernel(oa_ref, ob_ref, ga_ref, gb_ref, x_ref, wa_ref, wb_ref, wo_ref, g_ref, b_ref, o_ref, o2_ref,
                  *, alpha, token_tiles):
    a = jnp.dot(oa_ref[...], wa_ref[...], preferred_element_type=F32)
    b = jnp.dot(ob_ref[...], wb_ref[...], preferred_element_type=F32)
    merged = ga_ref[...].astype(F32) * a + gb_ref[...].astype(F32) * b
    mix = jnp.dot(merged.astype(BF16), wo_ref[...], preferred_element_type=F32)
    y = _layer_norm(alpha * x_ref[...] + mix, g_ref[...], b_ref[...])
    o_ref[...] = y
    if token_tiles:
        _store_token_tiles(o2_ref, y)
    else:
        o2_ref[...] = y.astype(BF16)


def _merge_project(oa, ob, ga, gb, x, wa, wb, wo, g, b, alpha, tm, token_tiles):
    n = x.shape[0]

    def rows(w):
        return pl.BlockSpec((tm, w), lambda i: (i, 0))

    def const(shape):
        return pl.BlockSpec(shape, lambda i: (0, 0), pipeline_mode=pl.Buffered(1))

    if token_tiles:
        out2_spec = pl.BlockSpec((tm * TOKEN_ROWS, LANES), lambda i: (i, 0))
        out2_shape = jax.ShapeDtypeStruct((n * TOKEN_ROWS, LANES), F32)
    else:
        out2_spec, out2_shape = rows(D_MODEL), jax.ShapeDtypeStruct((n, D_MODEL), BF16)
    return pl.pallas_call(
        functools.partial(_merge_kernel, alpha=alpha, token_tiles=token_tiles), grid=(n // tm,),
        in_specs=[rows(HEADS_W), rows(HEADS_W), rows(D_MODEL), rows(D_MODEL), rows(D_MODEL), const(wa.shape),
                  const(wb.shape), const(wo.shape), const(g.shape), const(b.shape)],
        out_specs=[rows(D_MODEL), out2_spec], out_shape=[jax.ShapeDtypeStruct((n, D_MODEL), F32), out2_shape],
        compiler_params=_cparams("parallel"), name="merge_project")(oa, ob, ga, gb, x, wa, wb, wo, g, b)


def _swiglu_partial(xb, wg, wu, wd):
    h = jnp.dot(xb, wg, preferred_element_type=F32)
    u = jnp.dot(xb, wu, preferred_element_type=F32)
    hid = (h * jax.nn.sigmoid(h)) * u
    return jnp.dot(hid.astype(BF16), wd, preferred_element_type=F32)


def _swiglu(xb, wg_ref, wu_ref, wd_ref, tf):
    acc = None
    for f in range(D_FF // tf):
        part = _swiglu_partial(xb, wg_ref[:, f * tf:(f + 1) * tf], wu_ref[:, f * tf:(f + 1) * tf],
                               wd_ref[f * tf:(f + 1) * tf, :])
        acc = part if acc is None else acc + part
    return acc


def _ffn_kernel(xb_ref, x_ref, wg_ref, wu_ref, wd_ref, g_ref, b_ref, o_ref, *, alpha, tf):
    ff = _swiglu(xb_ref[...], wg_ref, wu_ref, wd_ref, tf)
    o_ref[...] = _layer_norm(alpha * x_ref[...] + ff, g_ref[...], b_ref[...])


def _dense_ffn(xb, x, wg, wu, wd, g, b, alpha, tm, tf):
    n = x.shape[0]
    rows = pl.BlockSpec((tm, D_MODEL), lambda i: (i, 0))

    def const(shape):
        return pl.BlockSpec(shape, lambda i: (0, 0), pipeline_mode=pl.Buffered(1))

    return pl.pallas_call(
        functools.partial(_ffn_kernel, alpha=alpha, tf=tf), grid=(n // tm,),
        in_specs=[rows, rows, const(wg.shape), const(wu.shape), const(wd.shape), const(g.shape), const(b.shape)],
        out_specs=rows, out_shape=jax.ShapeDtypeStruct((n, D_MODEL), F32),
        compiler_params=_cparams("parallel"), name="dense_ffn")(xb, x, wg, wu, wd, g, b)


def _router_kernel(x_ref, r_ref, idx_ref, gate_ref, cnt_ref, seen_sc):
    i = pl.program_id(0)

    @pl.when(i == 0)
    def _():
        seen_sc[...] = jnp.zeros_like(seen_sc)

    xs = _split3(x_ref[...])
    logits = jnp.zeros((x_ref.shape[0], LANES), F32)
    for a, b in ((1, 0), (0, 1), (0, 0)):
        logits = logits + jnp.dot(xs[a], r_ref[b], preferred_element_type=F32)
    lane = lax.broadcasted_iota(I32, logits.shape, 1)
    lg = jnp.where(lane < N_EXPERTS, logits, -jnp.inf)
    v1 = jnp.max(lg, axis=-1, keepdims=True)
    i1 = jnp.min(jnp.where(lg == v1, lane, LANES), axis=-1, keepdims=True)
    lg2 = jnp.where(lane == i1, -jnp.inf, lg)
    v2 = jnp.max(lg2, axis=-1, keepdims=True)
    i2 = jnp.min(jnp.where(lg2 == v2, lane, LANES), axis=-1, keepdims=True)
    e = jnp.exp(v2 - v1)
    g1 = 1.0 / (1.0 + e)
    g2 = e / (1.0 + e)
    hit1 = lane == i1
    hit2 = lane == i2
    cnt = jnp.where(hit1, 1.0, jnp.where(hit2, 1.0, 0.0))
    tm = cnt.shape[0]
    r = lax.broadcasted_iota(I32, (tm, tm), 0)
    c = lax.broadcasted_iota(I32, (tm, tm), 1)
    tri = jnp.where(c < r, 1.0, 0.0).astype(BF16)
    before = jnp.dot(tri, cnt.astype(BF16), preferred_element_type=F32) + seen_sc[...]
    rank1 = jnp.sum(jnp.where(hit1, before, 0.0), axis=-1, keepdims=True).astype(I32)
    rank2 = jnp.sum(jnp.where(hit2, before, 0.0), axis=-1, keepdims=True).astype(I32)
    seen = seen_sc[...] + jnp.sum(cnt, axis=0, keepdims=True)
    seen_sc[...] = seen
    idx_ref[...] = jnp.where(lane == 0, i1, jnp.where(lane == 1, i2, jnp.where(lane == 2, rank1,
                                                                                  jnp.where(lane == 3, rank2, 0))))
    gate_ref[...] = jnp.where(lane == 0, g1, jnp.where(lane == 1, g2, 0.0))
    cnt_ref[...] = jnp.broadcast_to(seen, cnt_ref.shape).astype(I32)


def _route(x, router, tm):
    n = x.shape[0]
    r_pad = jnp.concatenate([router, jnp.zeros((D_MODEL, LANES - N_EXPERTS), F32)], axis=1)
    r3 = jnp.stack(_split3(r_pad))
    rows = pl.BlockSpec((tm, LANES), lambda i: (i, 0))
    return pl.pallas_call(
        _router_kernel, grid=(n // tm,),
        in_specs=[pl.BlockSpec((tm, D_MODEL), lambda i: (i, 0)), pl.BlockSpec(r3.shape, lambda i: (0, 0, 0))],
        out_specs=[rows, rows, pl.BlockSpec((SUBLANES, LANES), lambda i: (0, 0))],
        out_shape=[jax.ShapeDtypeStruct((n, LANES), I32), jax.ShapeDtypeStruct((n, LANES), F32),
                   jax.ShapeDtypeStruct((SUBLANES, LANES), I32)],
        scratch_shapes=[pltpu.VMEM((1, LANES), F32)],
        compiler_params=_cparams("arbitrary"), name="moe_router")(x, r3)


def _tile_rows(idx):
    return pl.ds(pl.multiple_of(idx * TOKEN_ROWS, TOKEN_ROWS), TOKEN_ROWS)


def _dispatch_kernel(slot_ref, x_ref, init_ref, xs_ref, sem, *, tm):
    del init_ref
    base = pl.program_id(0) * tm

    def start(r, carry):
        src = x_ref.at[_tile_rows(r), :]
        for j in range(2):
            pltpu.make_async_copy(src, xs_ref.at[_tile_rows(slot_ref[2 * (base + r) + j]), :], sem).start()
        return carry

    lax.fori_loop(0, tm, start, 0, unroll=8)
    for _ in range(2):
        pltpu.make_async_copy(x_ref, xs_ref.at[pl.ds(0, tm * TOKEN_ROWS), :], sem).wait()


def _dispatch(slots_flat, xt, n_slots, tm):
    n = xt.shape[0] // TOKEN_ROWS
    any_spec = pl.BlockSpec(memory_space=pl.ANY)
    gs = pltpu.PrefetchScalarGridSpec(
        num_scalar_prefetch=1, grid=(n // tm,),
        in_specs=[pl.BlockSpec((tm * TOKEN_ROWS, LANES), lambda i, slots: (i, 0)), any_spec],
        out_specs=any_spec, scratch_shapes=[pltpu.SemaphoreType.DMA(())])
    return pl.pallas_call(
        functools.partial(_dispatch_kernel, tm=tm), grid_spec=gs,
        out_shape=jax.ShapeDtypeStruct((n_slots * TOKEN_ROWS, LANES), F32), input_output_aliases={2: 0},
        compiler_params=_cparams("arbitrary"), name="moe_dispatch",
    )(slots_flat, xt, jnp.zeros((n_slots * TOKEN_ROWS, LANES), F32))


def _moe_ffn_kernel(be_ref, nv_ref, xs_ref, wg_ref, wu_ref, wd_ref, y_ref, *, tm, tf):
    del be_ref
    live = pl.program_id(0) < nv_ref[0]

    @pl.when(live)
    def _():
        xb = _load_token_tiles(xs_ref, tm).astype(BF16)
        _store_token_tiles(y_ref, _swiglu(xb, wg_ref, wu_ref, wd_ref, tf))

    @pl.when(jnp.logical_not(live))
    def _():
        y_ref[...] = jnp.zeros_like(y_ref)


def _moe_ffn(block_expert, n_valid, xs, wg, wu, wd, tm, tf):
    p = xs.shape[0] // TOKEN_ROWS
    tiles = pl.BlockSpec((tm * TOKEN_ROWS, LANES), lambda i, be, nv: (i, 0))

    def expert(shape):
        return pl.BlockSpec((None,) + shape[1:], lambda i, be, nv: (be[i], 0, 0), pipeline_mode=pl.Buffered(1))

    gs = pltpu.PrefetchScalarGridSpec(
        num_scalar_prefetch=2, grid=(p // tm,),
        in_specs=[tiles, expert(wg.shape), expert(wu.shape), expert(wd.shape)], out_specs=tiles)
    return pl.pallas_call(
        functools.partial(_moe_ffn_kernel, tm=tm, tf=tf), grid_spec=gs, out_shape=jax.ShapeDtypeStruct(xs.shape, F32),
        compiler_params=_cparams("arbitrary"), name="moe_ffn")(block_expert, n_valid, xs, wg, wu, wd)


def _combine_kernel(slot_ref, x_ref, gate_ref, g_ref, b_ref, y_ref, o_ref, ybuf, sem, *, alpha, tm):
    i = pl.program_id(0)

    def gather(step, par):
        base = step * tm

        def start(r, carry):
            for j in range(2):
                src = y_ref.at[_tile_rows(slot_ref[2 * (base + r) + j]), :]
                pltpu.make_async_copy(src, ybuf.at[par, j, _tile_rows(r), :], sem.at[par]).start()
            return carry

        lax.fori_loop(0, tm, start, 0, unroll=8)

    @pl.when(i == 0)
    def _():
        gather(0, 0)

    @pl.when(i + 1 < pl.num_programs(0))
    def _():
        gather(i + 1, (i + 1) & 1)

    par = i & 1
    for j in range(2):
        pltpu.make_async_copy(y_ref.at[pl.ds(0, tm * TOKEN_ROWS), :], ybuf.at[par, j], sem.at[par]).wait()
    gt = gate_ref[...]
    ff = gt[:, 0:1] * _load_token_tiles(ybuf.at[par, 0], tm) + gt[:, 1:2] * _load_token_tiles(ybuf.at[par, 1], tm)
    o_ref[...] = _layer_norm(alpha * x_ref[...] + ff, g_ref[...], b_ref[...])


def _combine(slots_flat, x, gates, g, b, y, alpha, tm):
    n = x.shape[0]
    rows = pl.BlockSpec((tm, D_MODEL), lambda i, slots: (i, 0))
    vec = pl.BlockSpec((1, D_MODEL), lambda i, slots: (0, 0))
    gs = pltpu.PrefetchScalarGridSpec(
        num_scalar_prefetch=1, grid=(n // tm,),
        in_specs=[rows, pl.BlockSpec((tm, LANES), lambda i, slots: (i, 0)), vec, vec,
                  pl.BlockSpec(memory_space=pl.ANY)],
        out_specs=rows, scratch_shapes=[pltpu.VMEM((2, 2, tm * TOKEN_ROWS, LANES), F32),
                                        pltpu.SemaphoreType.DMA((2,))])
    return pl.pallas_call(
        functools.partial(_combine_kernel, alpha=alpha, tm=tm), grid_spec=gs,
        out_shape=jax.ShapeDtypeStruct((n, D_MODEL), F32), compiler_params=_cparams("arbitrary"),
        name="moe_combine")(slots_flat, x, gates, g, b, y)


def _moe_layer(x, xt, router, wg, wu, wd, g, b, alpha, tiles):
    n = x.shape[0]
    tmb = tiles["moe_rows"]
    idx, gates, counts = _route(x, router, tiles["route_rows"])
    counts = counts[0, :N_EXPERTS]
    padded = ((counts + tmb - 1) // tmb) * tmb
    pad_end = jnp.cumsum(padded)
    pad_start = pad_end - padded
    slots = jnp.stack([pad_start[idx[:, 0]] + idx[:, 2], pad_start[idx[:, 1]] + idx[:, 3]], axis=1)
    slots_flat = slots.reshape(-1).astype(I32)
    n_slots = 2 * n + N_EXPERTS * tmb
    n_blocks = n_slots // tmb
    block_expert = jnp.clip(jnp.searchsorted(pad_end, jnp.arange(n_blocks, dtype=I32) * tmb, side="right"),
                            0, N_EXPERTS - 1).astype(I32)
    n_valid = (pad_end[-1:] // tmb).astype(I32)
    xs = _dispatch(slots_flat, xt, n_slots, tiles["dma_rows"])
    y = _moe_ffn(block_expert, n_valid, xs, wg, wu, wd, tmb, tiles["ffn_cols"])
    return _combine(slots_flat, x, gates, g, b, y, alpha, tiles["dma_rows"])


def _tiles(n, t):
    return dict(
        proj_rows=min(256, t), attn_q=min(256, t), row_tile=min(512, n),
        ffn_cols=512, route_rows=min(256, n), moe_rows=min(512, n), dma_rows=min(256, n))


def kernel(x, positions, w_in, b_forget, w_branch_a, w_branch_b, w_out, ln_mix_g, ln_mix_b, ln_ffn_g, ln_ffn_b,
           ffn_w_gate, ffn_w_up, ffn_w_down, moe_router, moe_w_gate, moe_w_up, moe_w_down):
    B, T, D = x.shape
    assert D == D_MODEL and T % CHUNK == 0
    n = B * T
    depth = w_in.shape[0]
    top_k = min(TOPK_MAX, T // 4)
    alpha = (2.0 * depth) ** 0.25
    tiles = _tiles(n, T)
    cos_t, sin_t = _rope_tables(positions)
    xf = x.reshape(n, D)
    for layer in range(depth):
        weights = _prep_mix_weights(w_in[layer], b_forget[layer])
        (dq, dk, iq, fq, fk, ga, gb, ik2, dvt, fvt, iwt, aux) = _in_projection(
            xf.reshape(B, T, D), cos_t, sin_t, weights, tiles["proj_rows"])
        o_a = _dsa_attention(iq, ik2, iwt, dq, dk, dvt, tiles["attn_q"], top_k)
        o_b = _fox_attention(fq, fk, aux, fvt, tiles["attn_q"])
        dense = layer % 2 == 0
        x1, x1_alt = _merge_project(
            o_a.reshape(n, HEADS_W), o_b.reshape(n, HEADS_W), ga.reshape(n, D), gb.reshape(n, D), xf,
            w_branch_a[layer].astype(BF16), w_branch_b[layer].astype(BF16), w_out[layer].astype(BF16),
            ln_mix_g[layer][None, :], ln_mix_b[layer][None, :], alpha, tiles["row_tile"], token_tiles=not dense)
        j = layer // 2
        g, b = ln_ffn_g[layer][None, :], ln_ffn_b[layer][None, :]
        if dense:
            xf = _dense_ffn(x1_alt, x1, ffn_w_gate[j].astype(BF16), ffn_w_up[j].astype(BF16),
                            ffn_w_down[j].astype(BF16), g, b, alpha, tiles["row_tile"], tiles["ffn_cols"])
        else:
            xf = _moe_layer(x1, x1_alt, moe_router[j], moe_w_gate[j].astype(BF16), moe_w_up[j].astype(BF16),
                            moe_w_down[j].astype(BF16), g, b, alpha, tiles)
    return xf.reshape(B, T, D)
```

```python
import functools

import jax
import jax.numpy as jnp
from jax import lax
from jax.experimental import pallas as pl
from jax.experimental.pallas import tpu as pltpu

F32, BF16, I32, I16 = jnp.float32, jnp.bfloat16, jnp.int32, jnp.int16

D_MODEL = 1024
HEAD_DIM = 64
N_HEADS = 8
HEADS_W = N_HEADS * HEAD_DIM
N_PAIRS = N_HEADS // 2
CHUNK = 64
TOPK_MAX = 256
ROPE_THETA = 10000.0
D_FF = 3584
N_EXPERTS = 8
LN_EPS = 1e-5
DEPTH = 2
LANES = 128
SUBLANES = 8
PACKED_ROWS = 2 * SUBLANES
INT_MIN = -(2**31)
I16_MIN = -(2**15)
NEG = -1e30
VMEM_LIMIT = 56 * 1024 * 1024
NT_DIMS = (((1,), (1,)), ((), ()))


def _cparams(*sem):
    return pltpu.CompilerParams(dimension_semantics=sem, vmem_limit_bytes=VMEM_LIMIT)


def _split3(v):
    hi = v.astype(BF16)
    r1 = v - hi.astype(F32)
    mid = r1.astype(BF16)
    lo = (r1 - mid.astype(F32)).astype(BF16)
    return hi, mid, lo


def _layer_norm(y, g, b):
    mu = jnp.mean(y, axis=-1, keepdims=True)
    yc = y - mu
    var = jnp.mean(yc * yc, axis=-1, keepdims=True)
    return yc * lax.rsqrt(var + LN_EPS) * g + b


def _rope_kernel(pos_ref, freq_ref, sign_ref, cos_ref, sin_ref):
    ang = pos_ref[...].astype(F32) * freq_ref[...]
    cos_ref[...] = jnp.cos(ang)
    sin_ref[...] = jnp.sin(ang) * sign_ref[...]


def _rope_tables(positions):
    B, T = positions.shape
    n = B * T
    inv_freq = ROPE_THETA ** (-jnp.arange(0, HEAD_DIM, 2, dtype=F32) / HEAD_DIM)
    freq = jnp.tile(inv_freq, LANES // (HEAD_DIM // 2))[None, :]
    half = HEAD_DIM // 2
    sign = jnp.tile(jnp.concatenate([-jnp.ones((half,), F32), jnp.ones((half,), F32)]), LANES // HEAD_DIM)[None, :]
    pos_b = jnp.broadcast_to(positions.reshape(n, 1), (n, LANES))
    tm = min(n, 1024)
    row = pl.BlockSpec((tm, LANES), lambda i: (i, 0))
    const = pl.BlockSpec((1, LANES), lambda i: (0, 0))
    cos, sin = pl.pallas_call(
        _rope_kernel, grid=(n // tm,), in_specs=[row, const, const], out_specs=[row, row],
        out_shape=[jax.ShapeDtypeStruct((n, LANES), F32)] * 2, compiler_params=_cparams("parallel"),
        name="rope_tables")(pos_b, freq, sign)
    return cos.reshape(B, T, LANES), sin.reshape(B, T, LANES)


LOG2E = 1.4426950408889634
_Q_SCALE = HEAD_DIM ** -0.5 * LOG2E
_IW_SCALE = (N_HEADS ** -0.5) * (HEAD_DIM ** -0.5)
_OFF_DQ, _OFF_DK, _OFF_IQ, _OFF_FQ, _OFF_FK, _OFF_GA, _OFF_GB = 0, 512, 1024, 1536, 2048, 2560, 3584
_W_MAIN = 4608
_AUX_USED = 6


def _inproj_kernel(x_ref, cos_ref, sin_ref, wm_ref, wik_ref, wvt_ref, wit_ref, wfl_ref, bfl_ref,
                   dq_ref, dk_ref, iq_ref, fq_ref, fk_ref, ga_ref, gb_ref, ik2_ref, dvt_ref, fvt_ref,
                   iwt_ref, aux_ref, carry_sc):
    @pl.when(pl.program_id(1) == 0)
    def _():
        carry_sc[...] = jnp.zeros_like(carry_sc)

    xb = x_ref[...].astype(BF16)
    cos = cos_ref[...]
    sin = sin_ref[...]
    lane = lax.broadcasted_iota(I32, cos.shape, 1)
    first_half = (lane & (HEAD_DIM - 1)) < HEAD_DIM // 2

    def rot(y):
        partner = jnp.where(first_half, pltpu.roll(y, LANES - HEAD_DIM // 2, 1), pltpu.roll(y, HEAD_DIM // 2, 1))
        return y * cos + partner * sin

    def proj(off, width):
        return jnp.dot(xb, wm_ref[:, off:off + width], preferred_element_type=F32)

    def rot_group(off, scale, out_ref):
        y = proj(off, HEADS_W)
        for j in range(N_PAIRS):
            r = rot(y[:, LANES * j:LANES * (j + 1)])
            if scale != 1.0:
                r = r * scale
            out_ref[:, LANES * j:LANES * (j + 1)] = r.astype(BF16)

    rot_group(_OFF_DQ, _Q_SCALE, dq_ref)
    rot_group(_OFF_DK, 1.0, dk_ref)
    rot_group(_OFF_IQ, 1.0, iq_ref)
    fq_ref[...] = (proj(_OFF_FQ, HEADS_W) * _Q_SCALE).astype(BF16)
    fk_ref[...] = proj(_OFF_FK, HEADS_W).astype(BF16)
    ga_ref[...] = jax.nn.sigmoid(proj(_OFF_GA, D_MODEL)).astype(BF16)
    gb_ref[...] = jax.nn.sigmoid(proj(_OFF_GB, D_MODEL)).astype(BF16)
    ik2_ref[...] = rot(jnp.dot(xb, wik_ref[...], preferred_element_type=F32)).astype(BF16)
    vt = lax.dot_general(wvt_ref[...], xb, NT_DIMS, preferred_element_type=F32)
    dvt_ref[...] = vt[:HEADS_W].astype(BF16)
    fvt_ref[...] = vt[HEADS_W:].astype(BF16)
    it = lax.dot_general(wit_ref[...], xb, NT_DIMS, preferred_element_type=F32)
    iwt_ref[...] = it[:N_HEADS] * _IW_SCALE
    z = jnp.dot(xb, wfl_ref[...], preferred_element_type=F32) + bfl_ref[...]
    lf = jnp.minimum(z, 0.0) - jnp.log1p(jnp.exp(-jnp.abs(z)))
    lane_w = lax.broadcasted_iota(I32, lf.shape, 1) & (LANES - 1)
    aux_ref[...] = _forget_aux(jnp.where(lane_w < _AUX_USED, lf, 0.0), lane_w, carry_sc)


def _forget_aux(lf, lane_w, carry_sc):
    tb = lf.shape[0]
    r = lax.broadcasted_iota(I32, (tb, tb), 0)
    c = lax.broadcasted_iota(I32, (tb, tb), 1)
    tri = jnp.where(c <= r, 1.0, 0.0).astype(BF16)
    hi, mid, lo = _split3(lf)
    cs = (jnp.dot(tri, lo, preferred_element_type=F32) + jnp.dot(tri, mid, preferred_element_type=F32)
          + jnp.dot(tri, hi, preferred_element_type=F32) + carry_sc[...])
    carry_sc[...] = cs[tb - 1:tb, :]
    nh, nm, nl = _split3(-LOG2E * cs)
    k = jnp.where(lane_w >= 3, lane_w - 3, lane_w)
    piece = jnp.where(k == 0, nh.astype(F32), jnp.where(k == 1, nm.astype(F32), nl.astype(F32)))
    return jnp.where(lane_w < _AUX_USED, piece, 0.0).astype(BF16)


def _prep_mix_weights(w_in, b_forget):
    sizes = (HEADS_W, HEADS_W, HEADS_W, HEADS_W, HEAD_DIM, N_HEADS, HEADS_W, HEADS_W, HEADS_W, N_HEADS, D_MODEL, D_MODEL)
    offs = [0]
    for s in sizes:
        offs.append(offs[-1] + s)
    dq, dk, dv, iq, ik, iw, fq, fk, fv, fl, ga, gb = [w_in[:, offs[i]:offs[i + 1]] for i in range(12)]
    w_main = jnp.concatenate([dq, dk, iq, fq, fk, ga, gb], axis=1).astype(BF16)
    w_ik2 = jnp.concatenate([ik, ik], axis=1).astype(BF16)
    w_vt = jnp.concatenate([dv, fv], axis=1).T.astype(BF16)
    w_it = jnp.concatenate([iw.T, jnp.zeros((16 - N_HEADS, D_MODEL), F32)], axis=0).astype(BF16)
    src = []
    for j in range(N_PAIRS):
        src += [2 * j] * 3 + [2 * j + 1] * 3 + [-1] * (LANES - _AUX_USED)
    src = jnp.array(src, I32)
    used = src >= 0
    w_fl = jnp.where(used[None, :], fl[:, jnp.maximum(src, 0)], 0.0).astype(BF16)
    b_fl = jnp.where(used, b_forget[jnp.maximum(src, 0)], 0.0)[None, :].astype(F32)
    return w_main, w_ik2, w_vt, w_it, w_fl, b_fl


def _in_projection(x, cos_t, sin_t, weights, tm):
    B, T, _ = x.shape
    w_main, w_ik2, w_vt, w_it, w_fl, b_fl = weights
    nt = T // tm

    def rows(width):
        return pl.BlockSpec((None, tm, width), lambda b, i: (b, i, 0))

    def cols(height):
        return pl.BlockSpec((None, height, tm), lambda b, i: (b, 0, i))

    def const(shape):
        return pl.BlockSpec(shape, lambda b, i: (0, 0), pipeline_mode=pl.Buffered(1))

    bt = lambda w, dt: jax.ShapeDtypeStruct((B, T, w), dt)
    tb = lambda h, dt: jax.ShapeDtypeStruct((B, h, T), dt)
    return pl.pallas_call(
        _inproj_kernel, grid=(B, nt),
        in_specs=[rows(D_MODEL), rows(LANES), rows(LANES), const(w_main.shape), const(w_ik2.shape),
                  const(w_vt.shape), const(w_it.shape), const(w_fl.shape), const(b_fl.shape)],
        out_specs=[rows(HEADS_W)] * 5 + [rows(D_MODEL)] * 2 + [rows(LANES), cols(HEADS_W), cols(HEADS_W),
                                                                cols(N_HEADS), rows(HEADS_W)],
        out_shape=[bt(HEADS_W, BF16)] * 5 + [bt(D_MODEL, BF16)] * 2 + [bt(LANES, BF16), tb(HEADS_W, BF16),
                                                                      tb(HEADS_W, BF16), tb(N_HEADS, F32),
                                                                      bt(HEADS_W, BF16)],
        scratch_shapes=[pltpu.VMEM((1, HEADS_W), F32)],
        compiler_params=_cparams("parallel", "arbitrary"), name="in_projection",
    )(x, cos_t, sin_t, w_main, w_ik2, w_vt, w_it, w_fl, b_fl)


def _stack_pair(qb):
    qf = qb.astype(F32)
    lane = lax.broadcasted_iota(I32, qf.shape, 1)
    lo = lane < HEAD_DIM
    return jnp.concatenate([jnp.where(lo, qf, 0.0), jnp.where(lo, 0.0, qf)], axis=0)


def _softmax_step(s, vb, j, m_sc, l_sc, acc_sc):
    tq = s.shape[1] // 2
    m_old = m_sc[j]
    m_new = jnp.maximum(m_old, jnp.max(s, axis=0, keepdims=True))
    alpha = jnp.exp2(m_old - m_new)
    p = jnp.exp2(s - m_new)
    l_sc[j] = alpha * l_sc[j] + jnp.sum(p, axis=0, keepdims=True)
    pb = p.astype(BF16)
    pv = jnp.concatenate([jnp.dot(vb[:HEAD_DIM], pb[:, :tq], preferred_element_type=F32),
                          jnp.dot(vb[HEAD_DIM:], pb[:, tq:], preferred_element_type=F32)], axis=1)
    acc_sc[j] = alpha * acc_sc[j] + pv
    m_sc[j] = m_new


def _init_softmax(m_sc, l_sc, acc_sc):
    m_sc[...] = jnp.full(m_sc.shape, -jnp.inf, F32)
    l_sc[...] = jnp.zeros(l_sc.shape, F32)
    acc_sc[...] = jnp.zeros(acc_sc.shape, F32)


def _finish_pair(o_ref, j, tq, l_sc, acc_sc):
    out_t = acc_sc[j] / l_sc[j]
    blk_t = jnp.concatenate([out_t[:, :tq], out_t[:, tq:]], axis=0)
    o_ref[:, LANES * j:LANES * (j + 1)] = blk_t.T.astype(BF16)


def _softmax_scratch(tq, q_width):
    return [pltpu.VMEM((N_PAIRS, q_width, 2 * tq), BF16), pltpu.VMEM((2, N_PAIRS, tq, 2 * tq), F32),
            pltpu.VMEM((N_PAIRS, 1, 2 * tq), F32), pltpu.VMEM((N_PAIRS, 1, 2 * tq), F32),
            pltpu.VMEM((N_PAIRS, HEAD_DIM, 2 * tq), F32)]


def _unrolled_loop(n, step, carry, unroll=2):
    def main(h, c):
        for u in range(unroll):
            c = step(unroll * h + u, c)
        return c

    n_main = n // unroll
    carry = lax.fori_loop(0, n_main, main, carry)
    return lax.fori_loop(n_main * unroll, n, step, carry)


def _pipelined_attention(n_steps, scores, update):
    scores(0, 0)
    n_mid = n_steps - 1

    def body(h, carry):
        kv = 2 * h
        for u in range(2):
            scores(kv + u + 1, (u + 1) & 1)
            update(kv + u, u & 1, False)
        return carry

    lax.fori_loop(0, n_mid // 2, body, 0)
    odd = n_mid % 2 == 1

    @pl.when(odd)
    def _():
        scores(n_mid, 1)
        update(n_mid - 1, 0, False)
        update(n_mid, 1, True)

    @pl.when(jnp.logical_not(odd))
    def _():
        update(n_mid, 0, True)


def _fox_kernel(q_ref, k_ref, aux_ref, vt_ref, o_ref, qt_sc, s_sc, m_sc, l_sc, acc_sc, *, tq):
    qi = pl.program_id(1)
    q0 = qi * tq
    row2 = lax.broadcasted_iota(I32, (2 * tq, LANES), 0)
    lane2 = lax.broadcasted_iota(I32, (2 * tq, LANES), 1)
    sel = jnp.where(row2 < tq, jnp.where(lane2 < 3, 1.0, 0.0),
                    jnp.where(lane2 < 3, 0.0, jnp.where(lane2 < _AUX_USED, 1.0, 0.0)))
    for j in range(N_PAIRS):
        qa = jnp.concatenate([_stack_pair(q_ref[:, LANES * j:LANES * (j + 1)]), sel], axis=1)
        qt_sc[j] = qa.T.astype(BF16)
    _init_softmax(m_sc, l_sc, acc_sc)

    def scores(kv, slot):
        ks = pl.multiple_of(kv * tq, tq)
        for j in range(N_PAIRS):
            cs = slice(LANES * j, LANES * (j + 1))
            kb = jnp.concatenate([k_ref[pl.ds(ks, tq), cs], aux_ref[pl.ds(ks, tq), cs]], axis=1)
            s_sc[slot, j] = jnp.dot(kb, qt_sc[j], preferred_element_type=F32)

    def update(kv, slot, last):
        ks = pl.multiple_of(kv * tq, tq)
        if last:
            krow = ks + lax.broadcasted_iota(I32, (tq, 2 * tq), 0)
            qcol = q0 + (lax.broadcasted_iota(I32, (tq, 2 * tq), 1) & (tq - 1))
            keep = krow <= qcol
        for j in range(N_PAIRS):
            s = s_sc[slot, j]
            if last:
                s = jnp.where(keep, s, NEG)
            _softmax_step(s, vt_ref[LANES * j:LANES * (j + 1), pl.ds(ks, tq)], j, m_sc, l_sc, acc_sc)

    _pipelined_attention(qi + 1, scores, update)
    for j in range(N_PAIRS):
        _finish_pair(o_ref, j, tq, l_sc, acc_sc)


def _fox_attention(fq, fk, aux, fvt, tq):
    B, T, W = fq.shape
    qspec = pl.BlockSpec((None, tq, W), lambda b, i: (b, i, 0))
    full = pl.BlockSpec((None, T, W), lambda b, i: (b, 0, 0))
    fullt = pl.BlockSpec((None, W, T), lambda b, i: (b, 0, 0))
    return pl.pallas_call(
        functools.partial(_fox_kernel, tq=tq), grid=(B, T // tq),
        in_specs=[qspec, full, full, fullt], out_specs=qspec,
        out_shape=jax.ShapeDtypeStruct((B, T, W), BF16), scratch_shapes=_softmax_scratch(tq, 2 * LANES),
        compiler_params=_cparams("parallel", "arbitrary"), name="fox_attention")(fq, fk, aux, fvt)


_COUNT_CHAINS = 4


def _dsa_kernel(iq_ref, ik2_ref, iwt_ref, q_ref, k_ref, vt_ref, o_ref, key_sc, hi_sc, lo_sc, bias_sc, qt_sc, s_sc,
                m_sc, l_sc, acc_sc, *, tq, top_k):
    qi = pl.program_id(1)
    q0 = qi * tq
    col = lax.broadcasted_iota(I32, (1, tq), 1)
    limit = (lax.shift_right_logical(q0 + col, CHUNK.bit_length() - 1) + 1) * CHUNK
    iw = iwt_ref[...]

    for j in range(N_PAIRS):
        qt_sc[j] = _stack_pair(iq_ref[:, LANES * j:LANES * (j + 1)]).T.astype(BF16)

    def score_tile(kv, masked):
        ks = pl.multiple_of(kv * tq, tq)
        kb = ik2_ref[pl.ds(ks, tq), :]
        sc = jnp.zeros((tq, tq), F32)
        for j in range(N_PAIRS):
            s2 = jnp.dot(kb, qt_sc[j], preferred_element_type=F32)
            sc = sc + jnp.maximum(s2[:, :tq], 0.0) * iw[2 * j:2 * j + 1, :]
            sc = sc + jnp.maximum(s2[:, tq:], 0.0) * iw[2 * j + 1:2 * j + 2, :]
        bits = pltpu.bitcast(sc, I32)
        key = bits ^ ((bits >> 31) & jnp.int32(0x7FFFFFFF))
        key = jnp.where(bits == jnp.int32(INT_MIN), 0, key)
        if masked:
            krow = ks + lax.broadcasted_iota(I32, key.shape, 0)
            key = jnp.where(krow < limit, key, jnp.int32(INT_MIN))
        key_sc[pl.ds(ks, tq), :] = key
        hi_sc[pl.ds(ks, tq), :] = (key >> 16).astype(I16)
        lo_sc[pl.ds(ks, tq), :] = ((key & 0xFFFF) + I16_MIN).astype(I16)

    def score_body(kv, carry):
        score_tile(kv, False)
        return carry

    _unrolled_loop(qi, score_body, 0)
    score_tile(qi, True)

    def count(plane_sc, pred):
        def body(g, accs):
            slab = plane_sc[pl.ds(pl.multiple_of(g * tq, tq), tq), :]
            accs = list(accs)
            for r in range(tq // PACKED_ROWS):
                blk = slab[PACKED_ROWS * r:PACKED_ROWS * (r + 1), :]
                hit = jnp.where(pred(blk), jnp.int16(1), jnp.int16(0))
                accs[r % _COUNT_CHAINS] = accs[r % _COUNT_CHAINS] + hit
            return tuple(accs)

        zero = jnp.zeros((PACKED_ROWS, tq), I16)
        accs = lax.fori_loop(0, qi + 1, body, (zero,) * _COUNT_CHAINS)
        return jnp.sum(sum(a.astype(I32) for a in accs), axis=0, keepdims=True)

    def rows16(v):
        return jnp.broadcast_to(v.astype(I16), (PACKED_ROWS, tq))

    def search(plane_sc, above):
        def body(i, v):
            cand = v + lax.shift_left(jnp.int32(1), 15 - i)
            cand16 = rows16(cand)
            cnt = above + count(plane_sc, lambda blk: blk >= cand16)
            return jnp.where(cnt >= top_k, cand, v)

        return lax.fori_loop(0, 16, body, jnp.full((1, tq), I16_MIN, I32))

    t_hi = search(hi_sc, 0)
    t_hi16 = rows16(t_hi)
    above_hi = count(hi_sc, lambda blk: blk > t_hi16)

    def mask_low(g, carry):
        rows = pl.ds(pl.multiple_of(g * tq, tq), tq)
        lo_sc[rows, :] = jnp.where(hi_sc[rows, :] == t_hi.astype(I16), lo_sc[rows, :], jnp.int16(I16_MIN))
        return carry

    lax.fori_loop(0, qi + 1, mask_low, 0)
    t_lo = search(lo_sc, above_hi)
    t_lo16 = rows16(t_lo)
    thr = t_hi * 65536 + (t_lo - I16_MIN)
    n_ge = above_hi + count(lo_sc, lambda blk: blk >= t_lo16)
    tie_walk = jnp.maximum(jnp.where(n_ge > top_k, 1, 0), jnp.where(t_lo == I16_MIN, 1, 0))
    any_tie_walk = jnp.max(tie_walk) > 0

    for j in range(N_PAIRS):
        qt_sc[j] = _stack_pair(q_ref[:, LANES * j:LANES * (j + 1)]).T.astype(BF16)
    _init_softmax(m_sc, l_sc, acc_sc)

    def scores(kv, slot):
        ks = pl.multiple_of(kv * tq, tq)
        for j in range(N_PAIRS):
            s_sc[slot, j] = jnp.dot(k_ref[pl.ds(ks, tq), LANES * j:LANES * (j + 1)], qt_sc[j],
                                    preferred_element_type=F32)

    def attend(select):
        def update(kv, slot, last):
            del last
            ks = pl.multiple_of(kv * tq, tq)
            keep_selected = select(ks)
            for j in range(N_PAIRS):
                _softmax_step(keep_selected(s_sc[slot, j]), vt_ref[LANES * j:LANES * (j + 1), pl.ds(ks, tq)], j,
                              m_sc, l_sc, acc_sc)

        _pipelined_attention(qi + 1, scores, update)

    @pl.when(jnp.logical_not(any_tie_walk))
    def _():
        def select(ks):
            b = jnp.where(key_sc[pl.ds(ks, tq), :] >= thr, 0.0, NEG)
            b2 = jnp.concatenate([b, b], axis=1)
            return lambda s: s + b2

        attend(select)

    @pl.when(any_tie_walk)
    def _():
        need = (top_k - above_hi - count(lo_sc, lambda blk: blk > t_lo16)).astype(F32)
        r = lax.broadcasted_iota(I32, (tq, tq), 0)
        c = lax.broadcasted_iota(I32, (tq, tq), 1)
        tri = jnp.where(c < r, 1.0, 0.0).astype(BF16)

        def bias_tile(kv, seen, masked):
            ks = pl.multiple_of(kv * tq, tq)
            key = key_sc[pl.ds(ks, tq), :]
            eqf = jnp.where(key == thr, 1.0, 0.0)
            before = jnp.dot(tri, eqf.astype(BF16), preferred_element_type=F32) + seen
            b = jnp.where(key > thr, 0.0, jnp.where(key == thr, jnp.where(before < need, 0.0, NEG), NEG))
            if masked:
                krow = ks + lax.broadcasted_iota(I32, key.shape, 0)
                b = jnp.where(krow < limit, b, NEG)
            bias_sc[pl.ds(ks, tq), :] = b
            return seen + jnp.sum(eqf, axis=0, keepdims=True)

        seen = _unrolled_loop(qi, lambda kv, s: bias_tile(kv, s, False), jnp.zeros((1, tq), F32))
        bias_tile(qi, seen, True)

        def select(ks):
            b = bias_sc[pl.ds(ks, tq), :]
            b2 = jnp.concatenate([b, b], axis=1)
            return lambda s: s + b2

        attend(select)

    for j in range(N_PAIRS):
        _finish_pair(o_ref, j, tq, l_sc, acc_sc)


def _dsa_attention(iq, ik2, iwt, dq, dk, dvt, tq, top_k):
    B, T, W = dq.shape
    qspec = pl.BlockSpec((None, tq, W), lambda b, i: (b, i, 0))
    full = pl.BlockSpec((None, T, W), lambda b, i: (b, 0, 0))
    fullt = pl.BlockSpec((None, W, T), lambda b, i: (b, 0, 0))
    return pl.pallas_call(
        functools.partial(_dsa_kernel, tq=tq, top_k=top_k), grid=(B, T // tq),
        in_specs=[qspec, pl.BlockSpec((None, T, LANES), lambda b, i: (b, 0, 0)),
                  pl.BlockSpec((None, N_HEADS, tq), lambda b, i: (b, 0, i)), qspec, full, fullt],
        out_specs=qspec, out_shape=jax.ShapeDtypeStruct((B, T, W), BF16),
        scratch_shapes=[pltpu.VMEM((T, tq), I32), pltpu.VMEM((T, tq), I16), pltpu.VMEM((T, tq), I16),
                        pltpu.VMEM((T, tq), F32)] + _softmax_scratch(tq, LANES),
        compiler_params=_cparams("parallel", "arbitrary"), name="dsa_attention")(iq, ik2, iwt, dq, dk, dvt)


TOKEN_ROWS = D_MODEL // LANES
assert TOKEN_ROWS == SUBLANES


def _store_token_tiles(ref, y):
    tm = y.shape[0]
    for s in range(TOKEN_ROWS):
        ref[pl.ds(s, tm, stride=TOKEN_ROWS), :] = y[:, LANES * s:LANES * (s + 1)]


def _load_token_tiles(ref, tm):
    return jnp.concatenate([ref[pl.ds(s, tm, stride=TOKEN_ROWS), :] for s in range(TOKEN_ROWS)], axis=1)


def _merge_kernel(oa_ref, ob_ref, ga_ref, gb_ref, x_ref, wa_ref, wb_ref, wo_ref, g_ref, b_ref, o_ref, o2_ref,
                  *, alpha, token_tiles):
    a = jnp.dot(oa_ref[...], wa_ref[...], preferred_element_type=F32)
    b = jnp.dot(ob_ref[...], wb_ref[...], preferred_element_type=F32)
    merged = ga_ref[...].astype(F32) * a + gb_ref[...].astype(F32) * b
    mix = jnp.dot(merged.astype(BF16), wo_ref[...], preferred_element_type=F32)
    y = _layer_norm(alpha * x_ref[...] + mix, g_ref[...], b_ref[...])
    o_ref[...] = y
    if token_tiles:
        _store_token_tiles(o2_ref, y)
    else:
        o2_ref[...] = y.astype(BF16)


def _merge_project(oa, ob, ga, gb, x, wa, wb, wo, g, b, alpha, tm, token_tiles):
    n = x.shape[0]

    def rows(w):
        return pl.BlockSpec((tm, w), lambda i: (i, 0))

    def const(shape):
        return pl.BlockSpec(shape, lambda i: (0, 0), pipeline_mode=pl.Buffered(1))

    if token_tiles:
        out2_spec = pl.BlockSpec((tm * TOKEN_ROWS, LANES), lambda i: (i, 0))
        out2_shape = jax.ShapeDtypeStruct((n * TOKEN_ROWS, LANES), F32)
    else:
        out2_spec, out2_shape = rows(D_MODEL), jax.ShapeDtypeStruct((n, D_MODEL), BF16)
    return pl.pallas_call(
        functools.partial(_merge_kernel, alpha=alpha, token_tiles=token_tiles), grid=(n // tm,),
        in_specs=[rows(HEADS_W), rows(HEADS_W), rows(D_MODEL), rows(D_MODEL), rows(D_MODEL), const(wa.shape),
                  const(wb.shape), const(wo.shape), const(g.shape), const(b.shape)],
        out_specs=[rows(D_MODEL), out2_spec], out_shape=[jax.ShapeDtypeStruct((n, D_MODEL), F32), out2_shape],
        compiler_params=_cparams("parallel"), name="merge_project")(oa, ob, ga, gb, x, wa, wb, wo, g, b)


def _swiglu_partial(xb, wg, wu, wd):
    h = jnp.dot(xb, wg, preferred_element_type=F32)
    u = jnp.dot(xb, wu, preferred_element_type=F32)
    hid = (h * jax.nn.sigmoid(h)) * u
    return jnp.dot(hid.astype(BF16), wd, preferred_element_type=F32)


def _swiglu(xb, wg_ref, wu_ref, wd_ref, tf):
    acc = None
    for f in range(D_FF // tf):
        part = _swiglu_partial(xb, wg_ref[:, f * tf:(f + 1) * tf], wu_ref[:, f * tf:(f + 1) * tf],
                               wd_ref[f * tf:(f + 1) * tf, :])
        acc = part if acc is None else acc + part
    return acc


def _ffn_kernel(xb_ref, x_ref, wg_ref, wu_ref, wd_ref, g_ref, b_ref, o_ref, *, alpha, tf):
    ff = _swiglu(xb_ref[...], wg_ref, wu_ref, wd_ref, tf)
    o_ref[...] = _layer_norm(alpha * x_ref[...] + ff, g_ref[...], b_ref[...])


def _dense_ffn(xb, x, wg, wu, wd, g, b, alpha, tm, tf):
    n = x.shape[0]
    rows = pl.BlockSpec((tm, D_MODEL), lambda i: (i, 0))

    def const(shape):
        return pl.BlockSpec(shape, lambda i: (0, 0), pipeline_mode=pl.Buffered(1))

    return pl.pallas_call(
        functools.partial(_ffn_kernel, alpha=alpha, tf=tf), grid=(n // tm,),
        in_specs=[rows, rows, const(wg.shape), const(wu.shape), const(wd.shape), const(g.shape), const(b.shape)],
        out_specs=rows, out_shape=jax.ShapeDtypeStruct((n, D_MODEL), F32),
        compiler_params=_cparams("parallel"), name="dense_ffn")(xb, x, wg, wu, wd, g, b)


def _router_kernel(x_ref, r_ref, idx_ref, gate_ref, cnt_ref, seen_sc):
    i = pl.program_id(0)

    @pl.when(i == 0)
    def _():
        seen_sc[...] = jnp.zeros_like(seen_sc)

    xs = _split3(x_ref[...])
    logits = jnp.zeros((x_ref.shape[0], LANES), F32)
    for a, b in ((1, 0), (0, 1), (0, 0)):
        logits = logits + jnp.dot(xs[a], r_ref[b], preferred_element_type=F32)
    lane = lax.broadcasted_iota(I32, logits.shape, 1)
    lg = jnp.where(lane < N_EXPERTS, logits, -jnp.inf)
    v1 = jnp.max(lg, axis=-1, keepdims=True)
    i1 = jnp.min(jnp.where(lg == v1, lane, LANES), axis=-1, keepdims=True)
    lg2 = jnp.where(lane == i1, -jnp.inf, lg)
    v2 = jnp.max(lg2, axis=-1, keepdims=True)
    i2 = jnp.min(jnp.where(lg2 == v2, lane, LANES), axis=-1, keepdims=True)
    e = jnp.exp(v2 - v1)
    g1 = 1.0 / (1.0 + e)
    g2 = e / (1.0 + e)
    hit1 = lane == i1
    hit2 = lane == i2
    cnt = jnp.where(hit1, 1.0, jnp.where(hit2, 1.0, 0.0))
    tm = cnt.shape[0]
    r = lax.broadcasted_iota(I32, (tm, tm), 0)
    c = lax.broadcasted_iota(I32, (tm, tm), 1)
    tri = jnp.where(c < r, 1.0, 0.0).astype(BF16)
    before = jnp.dot(tri, cnt.astype(BF16), preferred_element_type=F32) + seen_sc[...]
    rank1 = jnp.sum(jnp.where(hit1, before, 0.0), axis=-1, keepdims=True).astype(I32)
    rank2 = jnp.sum(jnp.where(hit2, before, 0.0), axis=-1, keepdims=True).astype(I32)
    seen = seen_sc[...] + jnp.sum(cnt, axis=0, keepdims=True)
    seen_sc[...] = seen
    idx_ref[...] = jnp.where(lane == 0, i1, jnp.where(lane == 1, i2, jnp.where(lane == 2, rank1,
                                                                                  jnp.where(lane == 3, rank2, 0))))
    gate_ref[...] = jnp.where(lane == 0, g1, jnp.where(lane == 1, g2, 0.0))
    cnt_ref[...] = jnp.broadcast_to(seen, cnt_ref.shape).astype(I32)


def _route(x, router, tm):
    n = x.shape[0]
    r_pad = jnp.concatenate([router, jnp.zeros((D_MODEL, LANES - N_EXPERTS), F32)], axis=1)
    r3 = jnp.stack(_split3(r_pad))
    rows = pl.BlockSpec((tm, LANES), lambda i: (i, 0))
    return pl.pallas_call(
        _router_kernel, grid=(n // tm,),
        in_specs=[pl.BlockSpec((tm, D_MODEL), lambda i: (i, 0)), pl.BlockSpec(r3.shape, lambda i: (0, 0, 0))],
        out_specs=[rows, rows, pl.BlockSpec((SUBLANES, LANES), lambda i: (0, 0))],
        out_shape=[jax.ShapeDtypeStruct((n, LANES), I32), jax.ShapeDtypeStruct((n, LANES), F32),
                   jax.ShapeDtypeStruct((SUBLANES, LANES), I32)],
        scratch_shapes=[pltpu.VMEM((1, LANES), F32)],
        compiler_params=_cparams("arbitrary"), name="moe_router")(x, r3)


def _tile_rows(idx):
    return pl.ds(pl.multiple_of(idx * TOKEN_ROWS, TOKEN_ROWS), TOKEN_ROWS)


def _dispatch_kernel(slot_ref, x_ref, init_ref, xs_ref, sem, *, tm):
    del init_ref
    base = pl.program_id(0) * tm

    def start(r, carry):
        src = x_ref.at[_tile_rows(r), :]
        for j in range(2):
            pltpu.make_async_copy(src, xs_ref.at[_tile_rows(slot_ref[2 * (base + r) + j]), :], sem).start(priority=j)
        return carry

    lax.fori_loop(0, tm, start, 0, unroll=8)
    for _ in range(2):
        pltpu.make_async_copy(x_ref, xs_ref.at[pl.ds(0, tm * TOKEN_ROWS), :], sem).wait()


def _dispatch(slots_flat, xt, n_slots, tm):
    n = xt.shape[0] // TOKEN_ROWS
    any_spec = pl.BlockSpec(memory_space=pl.ANY)
    gs = pltpu.PrefetchScalarGridSpec(
        num_scalar_prefetch=1, grid=(n // tm,),
        in_specs=[pl.BlockSpec((tm * TOKEN_ROWS, LANES), lambda i, slots: (i, 0)), any_spec],
        out_specs=any_spec, scratch_shapes=[pltpu.SemaphoreType.DMA(())])
    return pl.pallas_call(
        functools.partial(_dispatch_kernel, tm=tm), grid_spec=gs,
        out_shape=jax.ShapeDtypeStruct((n_slots * TOKEN_ROWS, LANES), F32), input_output_aliases={2: 0},
        compiler_params=_cparams("arbitrary"), name="moe_dispatch",
    )(slots_flat, xt, jnp.zeros((n_slots * TOKEN_ROWS, LANES), F32))


def _moe_ffn_kernel(be_ref, nv_ref, xs_ref, wg_ref, wu_ref, wd_ref, y_ref, *, tm, tf):
    del be_ref
    live = pl.program_id(0) < nv_ref[0]

    @pl.when(live)
    def _():
        xb = _load_token_tiles(xs_ref, tm).astype(BF16)
        _store_token_tiles(y_ref, _swiglu(xb, wg_ref, wu_ref, wd_ref, tf))

    @pl.when(jnp.logical_not(live))
    def _():
        y_ref[...] = jnp.zeros_like(y_ref)


def _moe_ffn(block_expert, n_valid, xs, wg, wu, wd, tm, tf):
    p = xs.shape[0] // TOKEN_ROWS
    tiles = pl.BlockSpec((tm * TOKEN_ROWS, LANES), lambda i, be, nv: (i, 0))

    def expert(shape):
        return pl.BlockSpec((None,) + shape[1:], lambda i, be, nv: (be[i], 0, 0), pipeline_mode=pl.Buffered(1))

    gs = pltpu.PrefetchScalarGridSpec(
        num_scalar_prefetch=2, grid=(p // tm,),
        in_specs=[tiles, expert(wg.shape), expert(wu.shape), expert(wd.shape)], out_specs=tiles)
    return pl.pallas_call(
        functools.partial(_moe_ffn_kernel, tm=tm, tf=tf), grid_spec=gs, out_shape=jax.ShapeDtypeStruct(xs.shape, F32),
        compiler_params=_cparams("arbitrary"), name="moe_ffn")(block_expert, n_valid, xs, wg, wu, wd)


def _combine_kernel(slot_ref, x_ref, gate_ref, g_ref, b_ref, y_ref, o_ref, ybuf, sem, *, alpha, tm):
    i = pl.program_id(0)

    def gather(step, par):
        base = step * tm

        def start(r, carry):
            for j in range(2):
                src = y_ref.at[_tile_rows(slot_ref[2 * (base + r) + j]), :]
                pltpu.make_async_copy(src, ybuf.at[par, j, _tile_rows(r), :], sem.at[par]).start(priority=j)
            return carry

        lax.fori_loop(0, tm, start, 0, unroll=8)

    @pl.when(i == 0)
    def _():
        gather(0, 0)

    @pl.when(i + 1 < pl.num_programs(0))
    def _():
        gather(i + 1, (i + 1) & 1)

    par = i & 1
    for j in range(2):
        pltpu.make_async_copy(y_ref.at[pl.ds(0, tm * TOKEN_ROWS), :], ybuf.at[par, j], sem.at[par]).wait()
    gt = gate_ref[...]
    ff = gt[:, 0:1] * _load_token_tiles(ybuf.at[par, 0], tm) + gt[:, 1:2] * _load_token_tiles(ybuf.at[par, 1], tm)
    o_ref[...] = _layer_norm(alpha * x_ref[...] + ff, g_ref[...], b_ref[...])


def _combine(slots_flat, x, gates, g, b, y, alpha, tm):
    n = x.shape[0]
    rows = pl.BlockSpec((tm, D_MODEL), lambda i, slots: (i, 0))
    vec = pl.BlockSpec((1, D_MODEL), lambda i, slots: (0, 0))
    gs = pltpu.PrefetchScalarGridSpec(
        num_scalar_prefetch=1, grid=(n // tm,),
        in_specs=[rows, pl.BlockSpec((tm, LANES), lambda i, slots: (i, 0)), vec, vec,
                  pl.BlockSpec(memory_space=pl.ANY)],
        out_specs=rows, scratch_shapes=[pltpu.VMEM((2, 2, tm * TOKEN_ROWS, LANES), F32),
                                        pltpu.SemaphoreType.DMA((2,))])
    return pl.pallas_call(
        functools.partial(_combine_kernel, alpha=alpha, tm=tm), grid_spec=gs,
        out_shape=jax.ShapeDtypeStruct((n, D_MODEL), F32), compiler_params=_cparams("arbitrary"),
        name="moe_combine")(slots_flat, x, gates, g, b, y)


def _moe_layer(x, xt, router, wg, wu, wd, g, b, alpha, tiles):
    n = x.shape[0]
    tmb = tiles["moe_rows"]
    idx, gates, counts = _route(x, router, tiles["route_rows"])
    counts = counts[0, :N_EXPERTS]
    padded = ((counts + tmb - 1) // tmb) * tmb
    pad_end = jnp.cumsum(padded)
    pad_start = pad_end - padded
    slots = jnp.stack([pad_start[idx[:, 0]] + idx[:, 2], pad_start[idx[:, 1]] + idx[:, 3]], axis=1)
    slots_flat = slots.reshape(-1).astype(I32)
    n_slots = 2 * n + N_EXPERTS * tmb
    n_blocks = n_slots // tmb
    block_expert = jnp.clip(jnp.searchsorted(pad_end, jnp.arange(n_blocks, dtype=I32) * tmb, side="right"),
                            0, N_EXPERTS - 1).astype(I32)
    n_valid = (pad_end[-1:] // tmb).astype(I32)
    xs = _dispatch(slots_flat, xt, n_slots, tiles["dma_rows"])
    y = _moe_ffn(block_expert, n_valid, xs, wg, wu, wd, tmb, tiles["ffn_cols"])
    return _combine(slots_flat, x, gates, g, b, y, alpha, tiles["dma_rows"])


def _tiles(n, t):
    return dict(
        proj_rows=min(256, t), attn_q=min(256, t), row_tile=min(512, n),
        ffn_cols=512, route_rows=min(256, n), moe_rows=min(512, n), dma_rows=min(256, n))


def kernel(x, positions, w_in, b_forget, w_branch_a, w_branch_b, w_out, ln_mix_g, ln_mix_b, ln_ffn_g, ln_ffn_b,
           ffn_w_gate, ffn_w_up, ffn_w_down, moe_router, moe_w_gate, moe_w_up, moe_w_down):
    B, T, D = x.shape
    assert D == D_MODEL and T % CHUNK == 0
    n = B * T
    depth = w_in.shape[0]
    top_k = min(TOPK_MAX, T // 4)
    alpha = (2.0 * depth) ** 0.25
    tiles = _tiles(n, T)
    cos_t, sin_t = _rope_tables(positions)
    xf = x.reshape(n, D)
    for layer in range(depth):
        weights = _prep_mix_weights(w_in[layer], b_forget[layer])
        (dq, dk, iq, fq, fk, ga, gb, ik2, dvt, fvt, iwt, aux) = _in_projection(
            xf.reshape(B, T, D), cos_t, sin_t, weights, tiles["proj_rows"])
        o_a = _dsa_attention(iq, ik2, iwt, dq, dk, dvt, tiles["attn_q"], top_k)
        o_b = _fox_attention(fq, fk, aux, fvt, tiles["attn_q"])
        dense = layer % 2 == 0
        x1, x1_alt = _merge_project(
            o_a.reshape(n, HEADS_W), o_b.reshape(n, HEADS_W), ga.reshape(n, D), gb.reshape(n, D), xf,
            w_branch_a[layer].astype(BF16), w_branch_b[layer].astype(BF16), w_out[layer].astype(BF16),
            ln_mix_g[layer][None, :], ln_mix_b[layer][None, :], alpha, tiles["row_tile"], token_tiles=not dense)
        j = layer // 2
        g, b = ln_ffn_g[layer][None, :], ln_ffn_b[layer][None, :]
        if dense:
            xf = _dense_ffn(x1_alt, x1, ffn_w_gate[j].astype(BF16), ffn_w_up[j].astype(BF16),
                            ffn_w_down[j].astype(BF16), g, b, alpha, tiles["row_tile"], tiles["ffn_cols"])
        else:
            xf = _moe_layer(x1, x1_alt, moe_router[j], moe_w_gate[j].astype(BF16), moe_w_up[j].astype(BF16),
                            moe_w_down[j].astype(BF16), g, b, alpha, tiles)
    return xf.reshape(B, T, D)
```
